```python
import jax, jax.numpy as jnp
from jax import lax
import numpy as np

D_MODEL = 1024
BATCH = 16
SEQ = 4096
DEPTH = 4

GRID_W = 64
N_EVEN = (DEPTH + 1) // 2
N_ODD = DEPTH // 2
REC_HEAD_DIM = 128
A_WIDTH = D_MODEL // 2
A_HEADS = A_WIDTH // REC_HEAD_DIM
A_DK = REC_HEAD_DIM
A_KEY = A_HEADS * A_DK
B_WIDTH = D_MODEL - A_WIDTH
B_HEADS = B_WIDTH // REC_HEAD_DIM
B_DH = REC_HEAD_DIM
MIX_WIDTH = A_WIDTH + B_WIDTH
_IN_SIZES = (A_KEY, A_KEY, A_KEY, A_WIDTH, A_WIDTH,
             B_WIDTH, B_WIDTH, B_WIDTH, B_WIDTH, 4 * B_HEADS)
IN_COLS = 3 * A_KEY + 2 * A_WIDTH + 4 * B_WIDTH + 4 * B_HEADS
CHUNK = 64
CONV_W = 5
NA_DH = 32
NA_HEADS = D_MODEL // NA_DH
WIN_R = 8
WIN_C = 16
COL_BLOCK = 16
COL_BAND = COL_BLOCK + WIN_C
FFN_HIDDEN = -(-8 * D_MODEL // (3 * 256)) * 256
ALPHA = (2.0 * DEPTH) ** 0.25
BETA = (8.0 * DEPTH) ** -0.25
LN_EPS = 1e-5
GN_EPS = 1e-6
NEG_BIG = -1e30
LB_FLOOR = 1e-30

kernel_name = "hgrn2_mlstm_natten_hybrid_encoder"


def _heads(t, n_heads):
    b, s, _ = t.shape
    return t.reshape(b, s, n_heads, -1).transpose(0, 2, 1, 3)


def _merge(t):
    b, n, s, d = t.shape
    return t.transpose(0, 2, 1, 3).reshape(b, s, n * d)


def _to_chunks(t):
    b, h, s = t.shape[:3]
    return jnp.moveaxis(t.reshape(b, h, s // CHUNK, CHUNK, *t.shape[3:]), 2, 0)


def _from_chunks(t):
    t = jnp.moveaxis(t, 0, 2)
    return t.reshape(t.shape[0], t.shape[1], -1, *t.shape[4:])


def _flip(t):
    return jnp.flip(t, axis=2)


def _bidirectional(scan_fn, fwd_args, bwd_args):
    return scan_fn(*fwd_args) + _flip(scan_fn(*[_flip(a) for a in bwd_args]))


def _hgrn2_scan(q, k, v, log_f):
    bsz, nh, _, dk = q.shape
    dv = v.shape[-1]
    mask = jnp.tril(jnp.ones((CHUNK, CHUNK), bool))[:, :, None]

    def step(state, xs):
        qc, kc, vc, lfc = xs
        b = jnp.cumsum(lfc, axis=2)
        rel = b[:, :, :, None, :] - b[:, :, None, :, :]
        decay = jnp.exp(jnp.where(mask, rel, NEG_BIG))
        scores = jnp.einsum('bhtd,bhsd,bhtsd->bhts', qc, kc, decay)
        o = (jnp.einsum('bhts,bhsv->bhtv', scores, vc)
             + jnp.einsum('bhtd,bhdv->bhtv', qc * jnp.exp(b), state))
        b_last = b[:, :, -1:, :]
        state = (jnp.exp(b_last[:, :, 0, :])[..., None] * state
                 + jnp.einsum('bhsd,bhsv->bhdv', kc * jnp.exp(b_last - b), vc))
        return state, o

    s0 = jnp.zeros((bsz, nh, dk, dv), jnp.float32)
    _, o = lax.scan(step, s0, (_to_chunks(q), _to_chunks(k), _to_chunks(v), _to_chunks(log_f)))
    return _from_chunks(o)


def _mlstm_scan(q, k, v, log_i, log_f):
    bsz, nh, _, dh = q.shape
    mask = jnp.tril(jnp.ones((CHUNK, CHUNK), bool))

    def step(carry, xs):
        c_st, n_st, m_st = carry
        qc, kc, vc, ic, fc = xs
        b = jnp.cumsum(fc, axis=-1)
        d_intra = jnp.where(mask, b[..., :, None] - b[..., None, :] + ic[..., None, :], NEG_BIG)
        d_inter = b + m_st[..., None]
        m_t = jnp.maximum(d_inter, jnp.max(d_intra, axis=-1))
        w_intra = jnp.exp(d_intra - m_t[..., None])
        w_inter = jnp.exp(d_inter - m_t)
        qk = jnp.einsum('bhtd,bhsd->bhts', qc, kc) * w_intra
        num = (jnp.einsum('bhts,bhsv->bhtv', qk, vc)
               + w_inter[..., None] * jnp.einsum('bhtd,bhdv->bhtv', qc, c_st))
        den = jnp.sum(qk, axis=-1) + w_inter * jnp.einsum('bhtd,bhd->bht', qc, n_st)
        h = num / jnp.maximum(jnp.abs(den), jnp.exp(-m_t))[..., None]
        b_last = b[..., -1]
        g = b_last[..., None] - b + ic
        m_new = jnp.maximum(b_last + m_st, jnp.max(g, axis=-1))
        w_old = jnp.exp(b_last + m_st - m_new)
        w_s = jnp.exp(g - m_new[..., None])
        c_st = w_old[..., None, None] * c_st + jnp.einsum('bhs,bhsd,bhsv->bhdv', w_s, kc, vc)
        n_st = w_old[..., None] * n_st + jnp.einsum('bhs,bhsd->bhd', w_s, kc)
        return (c_st, n_st, m_new), h

    carry0 = (jnp.zeros((bsz, nh, dh, dh), jnp.float32),
              jnp.zeros((bsz, nh, dh), jnp.float32),
              jnp.zeros((bsz, nh), jnp.float32))
    _, h = lax.scan(step, carry0, (_to_chunks(q), _to_chunks(k), _to_chunks(v),
                                   _to_chunks(log_i), _to_chunks(log_f)))
    return _from_chunks(h)


def _centred_dwconv(t, w):
    pad = CONV_W // 2
    s = t.shape[1]
    tp = jnp.pad(t, ((0, 0), (pad, pad), (0, 0)))
    out = tp[:, 0:s] * w[0]
    for j in range(1, CONV_W):
        out = out + tp[:, j:j + s] * w[j]
    return out


def _even_mixer(x, w_in, gate_bias, lb_fwd, lb_bwd, conv_w, gn_a, gn_b, w_out):
    f32 = jnp.float32
    proj = x @ w_in
    (aq, af_f, af_b, ai, ag, bq, bk, bv, bo, bg) = jnp.split(
        proj, np.cumsum(_IN_SIZES)[:-1].tolist(), axis=-1)

    q_a = _heads(aq, A_HEADS).astype(f32)
    v_a = _heads(ai, A_HEADS).astype(f32)

    def log_forget(z, lb):
        lb = lb.reshape(A_HEADS, 1, A_DK)
        return jnp.logaddexp(jnp.log(jnp.maximum(lb, LB_FLOOR)), jnp.log1p(-lb)
                             + jax.nn.log_sigmoid(_heads(z, A_HEADS).astype(f32)))

    lf_f = log_forget(af_f, lb_fwd)
    lf_b = log_forget(af_b, lb_bwd)
    o_a = _bidirectional(_hgrn2_scan, (q_a, -jnp.expm1(lf_f), v_a, lf_f),
                         (q_a, -jnp.expm1(lf_b), v_a, lf_b))
    o_a = o_a * lax.rsqrt(jnp.mean(jnp.square(o_a), -1, keepdims=True) + GN_EPS)
    o_a = _merge(o_a) * gn_a * jax.nn.silu(ag.astype(f32))

    qk = jax.nn.silu(_centred_dwconv(jnp.concatenate([bq, bk], -1), conv_w))
    q_b = _heads(qk[..., :B_WIDTH], B_HEADS).astype(f32)
    k_b = _heads(qk[..., B_WIDTH:], B_HEADS).astype(f32) * (B_DH ** -0.5)
    v_b = _heads(bv, B_HEADS).astype(f32)
    gates = (bg + gate_bias).astype(f32)
    gates = gates.reshape(gates.shape[0], gates.shape[1], 4, B_HEADS).transpose(2, 0, 3, 1)
    li_f, li_b = gates[0], gates[1]
    lfm_f, lfm_b = jax.nn.log_sigmoid(gates[2]), jax.nn.log_sigmoid(gates[3])
    h_b = _bidirectional(_mlstm_scan, (q_b, k_b, v_b, li_f, lfm_f),
                         (q_b, k_b, v_b, li_b, lfm_b))
    mu = jnp.mean(h_b, -1, keepdims=True)
    h_b = (h_b - mu) * lax.rsqrt(jnp.mean(jnp.square(h_b - mu), -1, keepdims=True) + GN_EPS)
    h_b = _merge(h_b) * gn_b * jax.nn.sigmoid(bo.astype(f32))

    return jnp.concatenate([o_a, h_b], -1).astype(x.dtype) @ w_out


def _neighbourhood_attention(x, w_qkv, rpb, w_out):
    bsz, seq, _ = x.shape
    rows = seq // GRID_W
    wr = min(WIN_R, rows)
    qkv = (x @ w_qkv).reshape(bsz, rows, GRID_W, 3, NA_HEADS, NA_DH)
    qkv = qkv.transpose(3, 0, 4, 1, 2, 5)
    q, k, v = qkv[0] * (NA_DH ** -0.5), qkv[1], qkv[2]

    n_cb = GRID_W // COL_BLOCK
    qcol = np.arange(GRID_W).reshape(n_cb, COL_BLOCK)
    c0 = np.clip(qcol - WIN_C // 2, 0, GRID_W - WIN_C)
    band = np.clip(c0[:, 0], 0, GRID_W - COL_BAND)[:, None] + np.arange(COL_BAND)
    col_valid = (band[:, None, :] >= c0[:, :, None]) & (band[:, None, :] < c0[:, :, None] + WIN_C)
    col_idx = np.clip(band[:, None, :] - qcol[:, :, None] + WIN_C - 1, 0, 2 * WIN_C - 2)
    valid = jnp.asarray(col_valid)[:, :, None, :]
    rpb_c = rpb.astype(jnp.float32)[:, :, col_idx]

    def one_row(r):
        r0 = jnp.clip(r - wr // 2, 0, rows - wr)
        kr = lax.dynamic_slice_in_dim(k, r0, wr, axis=2)[:, :, :, band]
        vr = lax.dynamic_slice_in_dim(v, r0, wr, axis=2)[:, :, :, band]
        qr = lax.dynamic_index_in_dim(q, r, axis=2, keepdims=False)
        qr = qr.reshape(bsz, NA_HEADS, n_cb, COL_BLOCK, NA_DH)
        row_idx = r0 + jnp.arange(wr) - r + WIN_R - 1
        bias = jnp.take(rpb_c, row_idx, axis=1).transpose(0, 2, 3, 1, 4)
        s = jnp.einsum('bhnqd,bhrnkd->bhnqrk', qr, kr).astype(jnp.float32) + bias
        s = jnp.where(valid, s, NEG_BIG).reshape(bsz, NA_HEADS, n_cb, COL_BLOCK, wr * COL_BAND)
        p = jax.nn.softmax(s, axis=-1).reshape(bsz, NA_HEADS, n_cb, COL_BLOCK, wr, COL_BAND)
        o = jnp.einsum('bhnqrk,bhrnkd->bhnqd', p.astype(v.dtype), vr)
        return o.reshape(bsz, NA_HEADS, GRID_W, NA_DH)

    o = lax.map(one_row, jnp.arange(rows))
    o = o.transpose(1, 0, 3, 2, 4).reshape(bsz, seq, D_MODEL)
    return o @ w_out


def _layer_norm(t, g, b):
    t32 = t.astype(jnp.float32)
    mu = jnp.mean(t32, -1, keepdims=True)
    var = jnp.mean(jnp.square(t32 - mu), -1, keepdims=True)
    return ((t32 - mu) * lax.rsqrt(var + LN_EPS) * g + b).astype(t.dtype)


def _swiglu(t, wg, wu, wd):
    return (jax.nn.silu(t @ wg) * (t @ wu)) @ wd


def setup_inputs(seed: int = 0) -> dict:
    key = jax.random.key(seed)
    ks = jax.random.split(key, 20)
    f32 = jnp.float32

    def nrm(k, shape, scale):
        return jax.random.normal(k, shape, f32) * scale

    x = nrm(ks[0], (BATCH, SEQ, D_MODEL), 1.0)
    w_in_even = nrm(ks[1], (N_EVEN, D_MODEL, IN_COLS), D_MODEL ** -0.5)
    f_bias = jnp.tile(jnp.linspace(3.0, 6.0, B_HEADS, dtype=f32), 2)
    gate_bias_even = jnp.concatenate(
        [nrm(ks[2], (N_EVEN, 2 * B_HEADS), 0.1),
         f_bias + nrm(ks[3], (N_EVEN, 2 * B_HEADS), 0.1)], axis=-1)
    lb_raw = nrm(ks[4], (2, N_EVEN, A_KEY), 0.5)
    conv_qk = nrm(ks[5], (N_EVEN, CONV_W, 2 * B_WIDTH), CONV_W ** -0.5)
    gn_hgrn = 1.0 + nrm(ks[6], (N_EVEN, A_WIDTH), 0.02)
    gn_mlstm = 1.0 + nrm(ks[7], (N_EVEN, B_WIDTH), 0.02)
    w_out_even = nrm(ks[8], (N_EVEN, MIX_WIDTH, D_MODEL), BETA * MIX_WIDTH ** -0.5)
    w_qkv_odd = nrm(ks[9], (N_ODD, D_MODEL, 3 * D_MODEL), D_MODEL ** -0.5)
    rpb_odd = nrm(ks[10], (N_ODD, NA_HEADS, 2 * WIN_R - 1, 2 * WIN_C - 1), 0.05)
    w_out_odd = nrm(ks[11], (N_ODD, D_MODEL, D_MODEL), BETA * D_MODEL ** -0.5)
    ln_mix_g = 1.0 + nrm(ks[12], (DEPTH, D_MODEL), 0.02)
    ln_mix_b = nrm(ks[13], (DEPTH, D_MODEL), 0.02)
    ln_ffn_g = 1.0 + nrm(ks[14], (DEPTH, D_MODEL), 0.02)
    ln_ffn_b = nrm(ks[15], (DEPTH, D_MODEL), 0.02)
    w_ffn_gate = nrm(ks[16], (DEPTH, D_MODEL, FFN_HIDDEN), D_MODEL ** -0.5)
    w_ffn_up = nrm(ks[17], (DEPTH, D_MODEL, FFN_HIDDEN), D_MODEL ** -0.5)
    w_ffn_down = nrm(ks[18], (DEPTH, FFN_HIDDEN, D_MODEL), BETA * FFN_HIDDEN ** -0.5)
    return {"x": x, "w_in_even": w_in_even, "gate_bias_even": gate_bias_even,
            "lb_raw": lb_raw, "conv_qk": conv_qk, "gn_hgrn": gn_hgrn, "gn_mlstm": gn_mlstm,
            "w_out_even": w_out_even, "w_qkv_odd": w_qkv_odd, "rpb_odd": rpb_odd,
            "w_out_odd": w_out_odd, "ln_mix_g": ln_mix_g, "ln_mix_b": ln_mix_b,
            "ln_ffn_g": ln_ffn_g, "ln_ffn_b": ln_ffn_b, "w_ffn_gate": w_ffn_gate,
            "w_ffn_up": w_ffn_up, "w_ffn_down": w_ffn_down}


def reference(x, w_in_even, gate_bias_even, lb_raw, conv_qk, gn_hgrn, gn_mlstm,
              w_out_even, w_qkv_odd, rpb_odd, w_out_odd, ln_mix_g, ln_mix_b,
              ln_ffn_g, ln_ffn_b, w_ffn_gate, w_ffn_up, w_ffn_down):
    soft = jax.nn.softmax(lb_raw.astype(jnp.float32), axis=1)
    lower_bounds = jnp.cumsum(soft, axis=1) - soft[:, :1]
    h = x
    for layer in range(DEPTH):
        j = layer // 2
        if layer % 2 == 0:
            mix = _even_mixer(h, w_in_even[j], gate_bias_even[j], lower_bounds[0, j],
                              lower_bounds[1, j], conv_qk[j], gn_hgrn[j], gn_mlstm[j],
                              w_out_even[j])
        else:
            mix = _neighbourhood_attention(h, w_qkv_odd[j], rpb_odd[j], w_out_odd[j])
        h = _layer_norm(ALPHA * h + mix, ln_mix_g[layer], ln_mix_b[layer])
        h = _layer_norm(ALPHA * h + _swiglu(h, w_ffn_gate[layer], w_ffn_up[layer], w_ffn_down[layer]),
                        ln_ffn_g[layer], ln_ffn_b[layer])
    return h
```

```python
import functools

import numpy as np
import jax
import jax.numpy as jnp
from jax import lax
from jax.experimental import pallas as pl
from jax.experimental.pallas import tpu as pltpu

F32 = jnp.float32
BF16 = jnp.bfloat16

GRID_W = 64
HEAD = 128
N_REC_HEADS = 4
CHUNK = 64
CONV_W = 5
NA_DH = 32
NA_GROUP = HEAD // NA_DH
WIN_R = 8
WIN_C = 16
LN_EPS = 1e-5
GN_EPS = 1e-6
NEG_BIG = -1e30
LB_FLOOR = 1e-30

V7X_LANES = 128
V7X_VMEM_LIMIT_BYTES = 56 * 1024 * 1024

_LEVELS = (32, 16, 8, 4, 2, 1)


def _cparams(n_grid_axes):
    return pltpu.CompilerParams(
        dimension_semantics=("arbitrary",) * n_grid_axes,
        vmem_limit_bytes=V7X_VMEM_LIMIT_BYTES)


def _dot(a, b):
    return jnp.dot(a, b, preferred_element_type=F32)


def _dot_nt(a, b):
    return lax.dot_general(a, b, (((1,), (1,)), ((), ())), preferred_element_type=F32)


def _dot_tn(a, b):
    return lax.dot_general(a, b, (((0,), (0,)), ((), ())), preferred_element_type=F32)


def _split3(x):
    hi = x.astype(BF16)
    r1 = x - hi.astype(F32)
    mid = r1.astype(BF16)
    lo = (r1 - mid.astype(F32)).astype(BF16)
    return hi, mid, lo


def _log_sigmoid(z):
    return jnp.minimum(z, 0.0) - jnp.log1p(jnp.exp(-jnp.abs(z)))


def _logaddexp(a, b):
    return jnp.maximum(a, b) + jnp.log1p(jnp.exp(-jnp.abs(a - b)))


def _sigmoid(z):
    return 1.0 / (1.0 + jnp.exp(-z))


def _layer_norm(t, g, b):
    mu = jnp.mean(t, axis=-1, keepdims=True)
    c = t - mu
    var = jnp.mean(c * c, axis=-1, keepdims=True)
    return c * lax.rsqrt(var + LN_EPS) * g + b


def _mm_kernel(x_ref, w_ref, o_ref, *, scale, transpose_out):
    acc = _dot(x_ref[...].astype(BF16), w_ref[...])
    if scale != 1.0:
        acc = acc * scale
    if transpose_out:
        acc = acc.T
    o_ref[...] = acc.astype(o_ref.dtype)


def _matmul(x, w, *, out_dtype, tm, tn, scale=1.0, transpose_out=False):
    t, k = x.shape
    n = w.shape[1]
    assert t % tm == 0 and n % tn == 0
    if transpose_out:
        out_shape = jax.ShapeDtypeStruct((n, t), out_dtype)
        out_spec = pl.BlockSpec((tn, tm), lambda i, j: (j, i))
    else:
        out_shape = jax.ShapeDtypeStruct((t, n), out_dtype)
        out_spec = pl.BlockSpec((tm, tn), lambda i, j: (i, j))
    return pl.pallas_call(
        functools.partial(_mm_kernel, scale=scale, transpose_out=transpose_out),
        grid=(t // tm, n // tn),
        in_specs=[pl.BlockSpec((tm, k), lambda i, j: (i, 0)),
                  pl.BlockSpec((k, tn), lambda i, j: (0, j))],
        out_specs=out_spec,
        out_shape=out_shape,
        compiler_params=_cparams(2),
    )(x, w)


def _hgrn_constants():
    L = CHUNK
    t = np.arange(L)
    a_rows, rowsel, masks = [], [], []
    for c in _LEVELS:
        odd = (t // c) % 2 == 1
        rho = (t // (2 * c)) * 2 * c + c - 1
        u = t[None, :]
        a = np.where(odd[:, None], (u > rho[:, None]) & (u <= t[:, None]),
                     (u > t[:, None]) & (u <= rho[:, None]))
        a_rows.append(a.astype(np.float32))
        rowsel.append(np.broadcast_to(odd[:, None], (L, HEAD)).astype(np.float32))
        same = (t[:, None] // (2 * c)) == (t[None, :] // (2 * c))
        masks.append((odd[:, None] & ~odd[None, :] & same).astype(np.float32))
    masks.append(np.eye(L, dtype=np.float32))
    a_rows.append((t[None, :] <= t[:, None]).astype(np.float32))
    a_rows.append((t[None, :] > t[:, None]).astype(np.float32))
    a_f = np.stack(a_rows)
    rs_f = np.stack(rowsel)
    m_f = np.stack(masks)
    a = np.stack([a_f, a_f[:, ::-1, ::-1]]).reshape(2, 8 * L, L)
    a3 = np.concatenate([a, a, a], axis=-1)
    rs = np.stack([rs_f, rs_f[:, ::-1]])
    m = np.stack([m_f, m_f[:, ::-1, ::-1]])
    return a3, rs, m


def _hgrn_kernel(q_ref, ff_ref, fb_ref, v_ref, g_ref, lb_ref, gn_ref, a3_ref, rs_ref, mk_ref,
                 o_ref, acc_s, st_s, *, layer_j, nchunk):
    L = CHUNK
    nlev = len(_LEVELS)
    f_refs = (ff_ref, fb_ref)

    lbr = lb_ref[...]
    e = jnp.exp(lbr - jnp.max(lbr, axis=1, keepdims=True))
    soft = e / jnp.sum(e, axis=1, keepdims=True)
    cum = soft[:, 0:1, :]
    for i in range(1, layer_j + 1):
        cum = cum + soft[:, i:i + 1, :]
    lb = cum - soft[:, 0:1, :]
    log_lb = jnp.log(jnp.maximum(lb, LB_FLOOR))
    log_1m = jnp.log1p(-lb)

    st_s[...] = jnp.zeros_like(st_s)

    def chunk_step(d, c, first_visit):
        rows = pl.ds(pl.multiple_of(c * L, L), L)
        q = q_ref[rows, :]
        v = v_ref[rows, :]
        z = f_refs[d][rows, :]
        lf = _logaddexp(log_lb[d], log_1m[d] + _log_sigmoid(z))
        k = 1.0 - jnp.exp(lf)
        hi, mid, lo = _split3(lf)
        dall = _dot(a3_ref[d], jnp.concatenate([hi, mid, lo], axis=0))
        eall = jnp.exp(dall)
        scores = _dot_nt(q.astype(BF16), k.astype(BF16)) * mk_ref[d, nlev]
        for li in range(nlev):
            y = (jnp.where(rs_ref[d, li] > 0.5, q, k) * eall[li * L:(li + 1) * L]).astype(BF16)
            scores = scores + _dot_nt(y, y) * mk_ref[d, li]
        e_pre = eall[nlev * L:(nlev + 1) * L]
        e_suf = eall[(nlev + 1) * L:(nlev + 2) * L]
        st = st_s[d]
        o = _dot(scores.astype(BF16), v.astype(BF16)) + _dot_nt((q * e_pre).astype(BF16), st.astype(BF16))
        last = L - 1 if d == 0 else 0
        st_s[d] = st * e_pre[last:last + 1, :] + _dot_tn(v.astype(BF16), (k * e_suf).astype(BF16))
        if first_visit:
            acc_s[rows, :] = o
        else:
            acc_s[rows, :] = acc_s[rows, :] + o

    half = nchunk // 2

    def body_first(it, carry):
        chunk_step(0, it, True)
        chunk_step(1, nchunk - 1 - it, True)
        return carry

    def body_second(it, carry):
        chunk_step(0, it, False)
        chunk_step(1, nchunk - 1 - it, False)
        return carry

    lax.fori_loop(0, half, body_first, 0)
    lax.fori_loop(half, nchunk, body_second, 0)

    gn = gn_ref[...]
    blk = 8 * L

    def fin(i, carry):
        rows = pl.ds(pl.multiple_of(i * blk, blk), blk)
        o = acc_s[rows, :]
        g = g_ref[rows, :]
        o = o * lax.rsqrt(jnp.mean(o * o, axis=-1, keepdims=True) + GN_EPS)
        o_ref[rows, :] = (o * gn * (g * _sigmoid(g))).astype(o_ref.dtype)
        return carry

    lax.fori_loop(0, (nchunk * L) // blk, fin, 0)


def _hgrn(proj, lb_raw, gn, *, batch, seq, layer_j):
    nchunk = seq // CHUNK
    assert nchunk % 2 == 0 and seq % (8 * CHUNK) == 0
    a3, rs, mk = _hgrn_constants()
    n_even = lb_raw.shape[1]
    nh = N_REC_HEADS

    def col(kind):
        return pl.BlockSpec((seq, HEAD), lambda b, h: (b, kind * nh + h))

    const3 = lambda b, h: (0, 0, 0)
    const4 = lambda b, h: (0, 0, 0, 0)
    return pl.pallas_call(
        functools.partial(_hgrn_kernel, layer_j=layer_j, nchunk=nchunk),
        grid=(batch, nh),
        in_specs=[col(0), col(1), col(2), col(3), col(4),
                  pl.BlockSpec((2, n_even, HEAD), lambda b, h: (0, 0, h)),
                  pl.BlockSpec((1, HEAD), lambda b, h: (0, h)),
                  pl.BlockSpec(a3.shape, const3),
                  pl.BlockSpec(rs.shape, const4),
                  pl.BlockSpec(mk.shape, const4)],
        out_specs=pl.BlockSpec((seq, HEAD), lambda b, h: (b, h)),
        out_shape=jax.ShapeDtypeStruct((batch * seq, nh * HEAD), BF16),
        scratch_shapes=[pltpu.VMEM((seq, HEAD), F32), pltpu.VMEM((2, HEAD, HEAD), F32)],
        compiler_params=_cparams(2),
    )(proj, proj, proj, proj, proj, lb_raw, gn.reshape(1, -1),
      jnp.asarray(a3, BF16), jnp.asarray(rs), jnp.asarray(mk))


def _mlstm_constants():
    L = CHUNK
    t = np.arange(L)
    ut = (t[:, None] <= t[None, :]).astype(np.float32)
    cum = np.stack([ut, ut[::-1, ::-1]])
    tril = (t[None, :] <= t[:, None]).astype(np.float32)
    causal = np.stack([tril, tril[::-1, ::-1]])
    return cum, causal, np.eye(L, dtype=np.float32)


def _mlstm_kernel(xq_ref, xk_ref, v_ref, og_ref, gates_ref, gb_ref, cwq_ref, cwk_ref, gn_ref,
                  cum_ref, cm_ref, eye_ref, o_ref, q_s, k_s, acc_s, b_s, c_s, m_s, *, nchunk):
    L = CHUNK
    pad = CONV_W // 2
    eye = eye_ref[...] > 0.5

    def row_to_col(r):
        return jnp.sum(jnp.where(eye, jnp.broadcast_to(r, (L, L)), 0.0), axis=1, keepdims=True)

    def conv_chunk(c, carry):
        rows = pl.ds(pl.multiple_of(c * L, L), L)
        prev = pl.ds(pl.multiple_of(jnp.maximum(c * L - 8, 0), 8), 8)
        nxt = pl.ds(pl.multiple_of(jnp.minimum(c * L + L, (nchunk - 1) * L + L - 8), 8), 8)
        has_prev = (c > 0).astype(F32)
        has_next = (c < nchunk - 1).astype(F32)
        for x_ref, w_ref, dst, scale in ((xq_ref, cwq_ref, q_s, 1.0), (xk_ref, cwk_ref, k_s, HEAD ** -0.5)):
            win = jnp.concatenate([x_ref[prev, :] * has_prev, x_ref[rows, :], x_ref[nxt, :] * has_next], axis=0)
            w = w_ref[...]
            acc = win[8 - pad:8 - pad + L] * w[0:1, :]
            for j in range(1, CONV_W):
                acc = acc + win[8 - pad + j:8 - pad + j + L] * w[j:j + 1, :]
            y = acc * _sigmoid(acc)
            dst[rows, :] = y * scale if scale != 1.0 else y
        return carry

    lax.fori_loop(0, nchunk, conv_chunk, 0)

    for d in range(2):
        lf2 = _log_sigmoid(gates_ref[2 + d] + gb_ref[2 + d])
        hi, mid, lo = _split3(lf2)
        cm = cum_ref[d]
        b_s[d] = _dot(hi, cm) + _dot(mid, cm) + _dot(lo, cm)

    c_s[...] = jnp.zeros_like(c_s)
    m_s[...] = jnp.zeros_like(m_s)
    lane = lax.broadcasted_iota(jnp.int32, (L, HEAD), 1)
    ones_col = (lane == 0).astype(F32)

    def chunk_step(d, c, first_visit):
        rows = pl.ds(pl.multiple_of(c * L, L), L)
        br = b_s[d, pl.ds(c, 1), :]
        li = gates_ref[d, pl.ds(c, 1), :] + gb_ref[d]
        causal = cm_ref[d] > 0.5
        b_col = row_to_col(br)
        d_intra = jnp.where(causal, b_col + (li - br), NEG_BIG)
        m_st = m_s[d]
        d_inter = b_col + m_st
        m_t = jnp.maximum(d_inter, jnp.max(d_intra, axis=1, keepdims=True))
        w_intra = jnp.exp(d_intra - m_t)
        w_inter = jnp.exp(d_inter - m_t)
        q = q_s[rows, :]
        k = k_s[rows, :]
        vaug = jnp.concatenate([v_ref[rows, :], ones_col], axis=1).astype(BF16)
        qb = q.astype(BF16)
        qk = _dot_nt(qb, k.astype(BF16)) * w_intra
        cst = c_s[d]
        nd = _dot(qk.astype(BF16), vaug) + w_inter * _dot(qb, cst.astype(BF16))
        num = nd[:, :HEAD]
        den = nd[:, HEAD:HEAD + 1]
        h = num / jnp.maximum(jnp.abs(den), jnp.exp(-m_t))
        last = L - 1 if d == 0 else 0
        b_last = br[:, last:last + 1]
        g_row = b_last - br + li
        m_new = jnp.maximum(b_last + m_st, jnp.max(g_row, axis=1, keepdims=True))
        w_old = jnp.exp(b_last + m_st - m_new)
        w_s = row_to_col(jnp.exp(g_row - m_new))
        c_s[d] = w_old * cst + _dot_tn((k * w_s).astype(BF16), vaug)
        m_s[d] = m_new
        if first_visit:
            acc_s[rows, :] = h
        else:
            acc_s[rows, :] = acc_s[rows, :] + h

    half = nchunk // 2

    def body_first(it, carry):
        chunk_step(0, it, True)
        chunk_step(1, nchunk - 1 - it, True)
        return carry

    def body_second(it, carry):
        chunk_step(0, it, False)
        chunk_step(1, nchunk - 1 - it, False)
        return carry

    lax.fori_loop(0, half, body_first, 0)
    lax.fori_loop(half, nchunk, body_second, 0)

    gn = gn_ref[...]
    blk = 8 * L

    def fin(i, carry):
        rows = pl.ds(pl.multiple_of(i * blk, blk), blk)
        h = acc_s[rows, :]
        mu = jnp.mean(h, axis=-1, keepdims=True)
        cen = h - mu
        hn = cen * lax.rsqrt(jnp.mean(cen * cen, axis=-1, keepdims=True) + GN_EPS)
        o_ref[rows, :] = (hn * gn * _sigmoid(og_ref[rows, :])).astype(o_ref.dtype)
        return carry

    lax.fori_loop(0, (nchunk * L) // blk, fin, 0)


def _mlstm(proj, gates, gate_bias, conv_w, gn, *, batch, seq):
    nchunk = seq // CHUNK
    assert nchunk % 2 == 0 and seq % (8 * CHUNK) == 0
    nh = N_REC_HEADS
    width = nh * HEAD
    cum, causal, eye = _mlstm_constants()
    g5 = gates.reshape(batch, nchunk, CHUNK, 4, nh).transpose(0, 4, 3, 1, 2)
    gb = jnp.broadcast_to(gate_bias.reshape(4, nh).T[:, :, None, None], (nh, 4, 1, CHUNK))
    cw = jnp.pad(conv_w, ((0, 8 - CONV_W), (0, 0)))
    first_col = 5 * nh

    def col(kind):
        return pl.BlockSpec((seq, HEAD), lambda b, h: (b, first_col + kind * nh + h))

    const2 = lambda b, h: (0, 0)
    const3 = lambda b, h: (0, 0, 0)
    return pl.pallas_call(
        functools.partial(_mlstm_kernel, nchunk=nchunk),
        grid=(batch, nh),
        in_specs=[col(0), col(1), col(2), col(3),
                  pl.BlockSpec((None, None, 4, nchunk, CHUNK), lambda b, h: (b, h, 0, 0, 0)),
                  pl.BlockSpec((None, 4, 1, CHUNK), lambda b, h: (h, 0, 0, 0)),
                  pl.BlockSpec((8, HEAD), lambda b, h: (0, h)),
                  pl.BlockSpec((8, HEAD), lambda b, h: (0, nh + h)),
                  pl.BlockSpec((1, HEAD), lambda b, h: (0, h)),
                  pl.BlockSpec(cum.shape, const3),
                  pl.BlockSpec(causal.shape, const3),
                  pl.BlockSpec(eye.shape, const2)],
        out_specs=pl.BlockSpec((seq, HEAD), lambda b, h: (b, h)),
        out_shape=jax.ShapeDtypeStruct((batch * seq, width), BF16),
        scratch_shapes=[pltpu.VMEM((seq, HEAD), F32), pltpu.VMEM((seq, HEAD), F32),
                        pltpu.VMEM((seq, HEAD), F32), pltpu.VMEM((2, nchunk, CHUNK), F32),
                        pltpu.VMEM((2, HEAD, 2 * HEAD), F32), pltpu.VMEM((2, 1, 1), F32)],
        compiler_params=_cparams(2),
    )(proj, proj, proj, proj, g5, gb, cw, cw, gn.reshape(1, -1),
      jnp.asarray(cum, BF16), jnp.asarray(causal), jnp.asarray(eye))


N_SLAB = 5
N_PAIR_TILES = 17


def _na_bias_tables(rpb):
    w = GRID_W
    qc = np.arange(w)[:, None]
    kc = np.arange(w)[None, :]
    c0 = np.clip(qc - WIN_C // 2, 0, w - WIN_C)
    valid = (kc >= c0) & (kc < c0 + WIN_C)
    cidx = np.clip(kc - qc + WIN_C - 1, 0, 2 * WIN_C - 2)
    tiles = jnp.where(jnp.asarray(valid)[None, None], rpb.astype(F32)[:, :, cidx], NEG_BIG)
    neg = jnp.full_like(tiles[:, 0], NEG_BIG)
    both = jnp.concatenate([tiles[:, :-1], tiles[:, 1:]], axis=-1)
    inner_first = WIN_R - 1 - WIN_R // 2
    inner_last = inner_first + WIN_R - 1
    left = jnp.concatenate([neg, tiles[:, inner_first]], axis=-1)[:, None]
    right = jnp.concatenate([tiles[:, inner_last], neg], axis=-1)[:, None]
    none = jnp.concatenate([neg, neg], axis=-1)[:, None]
    return jnp.concatenate([both, left, right, none], axis=1)


def _na_kernel(q_ref, kt_ref, v_ref, bias_ref, o_ref, *, n_rows):
    w = GRID_W
    pr = 2 * w
    n_pairs = n_rows // 2
    lane_head = lax.broadcasted_iota(jnp.int32, (pr, HEAD), 1) // NA_DH
    head_masks = [lane_head == hh for hh in range(NA_GROUP)]

    def pair_step(p, carry):
        q = q_ref[pl.ds(pl.multiple_of(p * pr, pr), pr), :]
        zero = jnp.zeros_like(q)
        qm = jnp.concatenate([jnp.where(head_masks[hh], q, zero) for hh in range(NA_GROUP)], axis=0)
        sp0 = jnp.clip(p - 2, 0, n_pairs - N_SLAB)
        s_list = []
        for i in range(N_SLAB):
            kt = kt_ref[:, pl.ds(pl.multiple_of((sp0 + i) * pr, pr), pr)]
            s = _dot(qm, kt)
            e_row = 2 * (sp0 + i)
            tiles = []
            for hh in range(NA_GROUP):
                for qr in range(2):
                    r = 2 * p + qr
                    r0 = jnp.clip(r - WIN_R // 2, 0, n_rows - WIN_R)
                    in_e = (e_row >= r0) & (e_row < r0 + WIN_R)
                    in_o = (e_row + 1 >= r0) & (e_row + 1 < r0 + WIN_R)
                    idx = jnp.where(in_e & in_o, e_row - r + WIN_R - 1,
                                    jnp.where(in_o, N_PAIR_TILES - 3, jnp.where(in_e, N_PAIR_TILES - 2, N_PAIR_TILES - 1)))
                    tiles.append(bias_ref[hh, idx])
            s_list.append(s + jnp.concatenate(tiles, axis=0))
        m = s_list[0]
        for s in s_list[1:]:
            m = jnp.maximum(m, s)
        m = jnp.max(m, axis=1, keepdims=True)
        acc = None
        l = None
        for i in range(N_SLAB):
            pexp = jnp.exp(s_list[i] - m)
            l = pexp if l is None else l + pexp
            vs = v_ref[pl.ds(pl.multiple_of((sp0 + i) * pr, pr), pr), :]
            pv = _dot(pexp.astype(BF16), vs)
            acc = pv if acc is None else acc + pv
        o = acc / jnp.sum(l, axis=1, keepdims=True)
        out = jnp.where(head_masks[0], o[0:pr], 0.0)
        for hh in range(1, NA_GROUP):
            out = out + jnp.where(head_masks[hh], o[hh * pr:(hh + 1) * pr], 0.0)
        o_ref[pl.ds(pl.multiple_of(p * pr, pr), pr), :] = out.astype(o_ref.dtype)
        return carry

    lax.fori_loop(0, n_pairs, pair_step, 0)


def _neighbourhood_attention(q, kt, v, bias, *, batch, seq):
    n_rows = seq // GRID_W
    assert n_rows % 2 == 0 and n_rows >= 2 * N_SLAB and n_rows >= WIN_R
    d_model = q.shape[1]
    n_groups = d_model // HEAD
    return pl.pallas_call(
        functools.partial(_na_kernel, n_rows=n_rows),
        grid=(batch, n_groups),
        in_specs=[pl.BlockSpec((seq, HEAD), lambda b, g: (b, g)),
                  pl.BlockSpec((HEAD, seq), lambda b, g: (g, b)),
                  pl.BlockSpec((seq, HEAD), lambda b, g: (b, g)),
                  pl.BlockSpec((NA_GROUP, N_PAIR_TILES, GRID_W, 2 * GRID_W), lambda b, g: (g, 0, 0, 0))],
        out_specs=pl.BlockSpec((seq, HEAD), lambda b, g: (b, g)),
        out_shape=jax.ShapeDtypeStruct((batch * seq, d_model), BF16),
        compiler_params=_cparams(2),
    )(q, kt, v, bias)


def _ffn_kernel(*refs, n_mix, alpha):
    x_ref = refs[0]
    mix_refs = refs[1:1 + 2 * n_mix]
    (lmg_ref, lmb_ref, lfg_ref, lfb_ref, wg_ref, wu_ref, wd_ref, o_ref, h_s, hb_s, acc_s) = refs[1 + 2 * n_mix:]
    j = pl.program_id(1)

    @pl.when(j == 0)
    def _():
        mix = _dot(mix_refs[0][...], mix_refs[1][...])
        for i in range(1, n_mix):
            mix = mix + _dot(mix_refs[2 * i][...], mix_refs[2 * i + 1][...])
        h = _layer_norm(alpha * x_ref[...] + mix, lmg_ref[...], lmb_ref[...])
        h_s[...] = h
        hb_s[...] = h.astype(BF16)
        acc_s[...] = jnp.zeros_like(acc_s)

    hb = hb_s[...]
    g = _dot(hb, wg_ref[...])
    u = _dot(hb, wu_ref[...])
    a = (g * _sigmoid(g) * u).astype(BF16)
    acc_s[...] += _dot(a, wd_ref[...])

    @pl.when(j == pl.num_programs(1) - 1)
    def _():
        o_ref[...] = _layer_norm(alpha * h_s[...] + acc_s[...], lfg_ref[...], lfb_ref[...])


def _mixer_out_ffn(x, mix_pairs, ln_mix_g, ln_mix_b, ln_ffn_g, ln_ffn_b, wg, wu, wd, *, alpha, tm, th):
    t, d = x.shape
    hid = wg.shape[1]
    assert t % tm == 0 and hid % th == 0
    row = lambda i, j: (i, 0)
    const = lambda i, j: (0, 0)
    in_specs = [pl.BlockSpec((tm, d), row)]
    args = [x]
    for o, w in mix_pairs:
        in_specs += [pl.BlockSpec((tm, o.shape[1]), row), pl.BlockSpec(w.shape, const)]
        args += [o, w]
    in_specs += [pl.BlockSpec((1, d), const)] * 4
    args += [ln_mix_g.reshape(1, d), ln_mix_b.reshape(1, d), ln_ffn_g.reshape(1, d), ln_ffn_b.reshape(1, d)]
    in_specs += [pl.BlockSpec((d, th), lambda i, j: (0, j)), pl.BlockSpec((d, th), lambda i, j: (0, j)),
                 pl.BlockSpec((th, d), lambda i, j: (j, 0))]
    args += [wg, wu, wd]
    return pl.pallas_call(
        functools.partial(_ffn_kernel, n_mix=len(mix_pairs), alpha=alpha),
        grid=(t // tm, hid // th),
        in_specs=in_specs,
        out_specs=pl.BlockSpec((tm, d), row),
        out_shape=jax.ShapeDtypeStruct((t, d), F32),
        scratch_shapes=[pltpu.VMEM((tm, d), F32), pltpu.VMEM((tm, d), BF16), pltpu.VMEM((tm, d), F32)],
        compiler_params=_cparams(2),
    )(*args)


def _row_tile(t):
    for tm in (1024, 512, 256, 128):
        if t % tm == 0:
            return tm
    raise ValueError(f"token count {t} is not a multiple of 128")


def kernel(x, w_in_even, gate_bias_even, lb_raw, conv_qk, gn_hgrn, gn_mlstm, w_out_even, w_qkv_odd, rpb_odd,
           w_out_odd, ln_mix_g, ln_mix_b, ln_ffn_g, ln_ffn_b, w_ffn_gate, w_ffn_up, w_ffn_down):
    batch, seq, d_model = x.shape
    depth = ln_mix_g.shape[0]
    alpha = (2.0 * depth) ** 0.25
    t = batch * seq
    tm = _row_tile(t)
    a_width = N_REC_HEADS * HEAD
    main_cols = 9 * a_width
    n_gate = 4 * N_REC_HEADS
    hid = w_ffn_gate.shape[-1]
    th = 256 if hid % 256 == 0 else 128
    tn_main = 1536 if main_cols % 1536 == 0 else 512

    h = x.reshape(t, d_model)
    for layer in range(depth):
        j = layer // 2
        if layer % 2 == 0:
            w_in = w_in_even[j]
            w_main = w_in[:, :main_cols].astype(BF16)
            w_gate = jnp.pad(w_in[:, main_cols:], ((0, 0), (0, V7X_LANES - n_gate))).astype(BF16)
            proj = _matmul(h, w_main, out_dtype=F32, tm=tm, tn=tn_main)
            gates = _matmul(h, w_gate, out_dtype=F32, tm=tm, tn=V7X_LANES)[:, :n_gate]
            o_a = _hgrn(proj, lb_raw, gn_hgrn[j], batch=batch, seq=seq, layer_j=j)
            h_b = _mlstm(proj, gates, gate_bias_even[j], conv_qk[j], gn_mlstm[j], batch=batch, seq=seq)
            w_out = w_out_even[j].astype(BF16)
            mix_pairs = [(o_a, w_out[:a_width]), (h_b, w_out[a_width:])]
        else:
            w_qkv = w_qkv_odd[j].astype(BF16)
            q = _matmul(h, w_qkv[:, :d_model], out_dtype=BF16, tm=tm, tn=d_model, scale=NA_DH ** -0.5)
            kt = _matmul(h, w_qkv[:, d_model:2 * d_model], out_dtype=BF16, tm=tm, tn=d_model, transpose_out=True)
            v = _matmul(h, w_qkv[:, 2 * d_model:], out_dtype=BF16, tm=tm, tn=d_model)
            bias = _na_bias_tables(rpb_odd[j])
            o = _neighbourhood_attention(q, kt, v, bias, batch=batch, seq=seq)
            mix_pairs = [(o, w_out_odd[j].astype(BF16))]
        h = _mixer_out_ffn(h, mix_pairs, ln_mix_g[layer], ln_mix_b[layer], ln_ffn_g[layer], ln_ffn_b[layer],
                           w_ffn_gate[layer].astype(BF16), w_ffn_up[layer].astype(BF16),
                           w_ffn_down[layer].astype(BF16), alpha=alpha, tm=tm, th=th)
    return h.reshape(batch, seq, d_model)
```

```python
import functools

import numpy as np
import jax
import jax.numpy as jnp
from jax import lax
from jax.experimental import pallas as pl
from jax.experimental.pallas import tpu as pltpu

F32 = jnp.float32
BF16 = jnp.bfloat16

GRID_W = 64
HEAD = 128
N_REC_HEADS = 4
CHUNK = 64
CONV_W = 5
NA_DH = 32
NA_GROUP = HEAD // NA_DH
WIN_R = 8
WIN_C = 16
LN_EPS = 1e-5
GN_EPS = 1e-6
NEG_BIG = -1e30
LB_FLOOR = 1e-30
LOG2_E = 1.4426950408889634

V7X_LANES = 128
V7X_VMEM_LIMIT_BYTES = 56 * 1024 * 1024

_LEVELS = (32, 16, 8, 4, 2, 1)


def _cparams(n_grid_axes):
    return pltpu.CompilerParams(
        dimension_semantics=("arbitrary",) * n_grid_axes,
        vmem_limit_bytes=V7X_VMEM_LIMIT_BYTES)


def _dot(a, b):
    return jnp.dot(a, b, preferred_element_type=F32)


def _dot_nt(a, b):
    return lax.dot_general(a, b, (((1,), (1,)), ((), ())), preferred_element_type=F32)


def _dot_tn(a, b):
    return lax.dot_general(a, b, (((0,), (0,)), ((), ())), preferred_element_type=F32)


def _split3(x):
    hi = x.astype(BF16)
    r1 = x - hi.astype(F32)
    mid = r1.astype(BF16)
    lo = (r1 - mid.astype(F32)).astype(BF16)
    return hi, mid, lo


def _log_sigmoid(z):
    return jnp.minimum(z, 0.0) - jnp.log1p(jnp.exp(-jnp.abs(z)))


def _logaddexp(a, b):
    return jnp.maximum(a, b) + jnp.log1p(jnp.exp(-jnp.abs(a - b)))


def _sigmoid(z):
    return 1.0 / (1.0 + jnp.exp(-z))


def _layer_norm(t, g, b):
    mu = jnp.mean(t, axis=-1, keepdims=True)
    c = t - mu
    var = jnp.mean(c * c, axis=-1, keepdims=True)
    return c * lax.rsqrt(var + LN_EPS) * g + b


def _mm_kernel(x_ref, w_ref, o_ref, *, scale, transpose_out):
    acc = _dot(x_ref[...].astype(BF16), w_ref[...])
    if scale != 1.0:
        acc = acc * scale
    if transpose_out:
        acc = acc.T
    o_ref[...] = acc.astype(o_ref.dtype)


def _matmul(x, w, *, out_dtype, tm, tn, scale=1.0, transpose_out=False):
    t, k = x.shape
    n = w.shape[1]
    assert t % tm == 0 and n % tn == 0
    if transpose_out:
        out_shape = jax.ShapeDtypeStruct((n, t), out_dtype)
        out_spec = pl.BlockSpec((tn, tm), lambda i, j: (j, i))
    else:
        out_shape = jax.ShapeDtypeStruct((t, n), out_dtype)
        out_spec = pl.BlockSpec((tm, tn), lambda i, j: (i, j))
    return pl.pallas_call(
        functools.partial(_mm_kernel, scale=scale, transpose_out=transpose_out),
        grid=(t // tm, n // tn),
        in_specs=[pl.BlockSpec((tm, k), lambda i, j: (i, 0)),
                  pl.BlockSpec((k, tn), lambda i, j: (0, j))],
        out_specs=out_spec,
        out_shape=out_shape,
        compiler_params=_cparams(2),
    )(x, w)


def _hgrn_constants():
    L = CHUNK
    t = np.arange(L)
    a_rows, rowsel, masks = [], [], []
    for c in _LEVELS:
        odd = (t // c) % 2 == 1
        rho = (t // (2 * c)) * 2 * c + c - 1
        u = t[None, :]
        a = np.where(odd[:, None], (u > rho[:, None]) & (u <= t[:, None]),
                     (u > t[:, None]) & (u <= rho[:, None]))
        a_rows.append(a.astype(np.float32))
        rowsel.append(np.broadcast_to(odd[:, None], (L, HEAD)).astype(np.float32))
        same = (t[:, None] // (2 * c)) == (t[None, :] // (2 * c))
        masks.append((odd[:, None] & ~odd[None, :] & same).astype(np.float32))
    masks.append(np.eye(L, dtype=np.float32))
    a_rows.append((t[None, :] <= t[:, None]).astype(np.float32))
    a_rows.append((t[None, :] > t[:, None]).astype(np.float32))
    a_f = np.stack(a_rows)
    rs_f = np.stack(rowsel)
    m_f = np.stack(masks)
    a = np.stack([a_f, a_f[:, ::-1, ::-1]]).reshape(2, 8 * L, L)
    a3 = np.concatenate([a, a, a], axis=-1)
    rs = np.stack([rs_f, rs_f[:, ::-1]])
    m = np.stack([m_f, m_f[:, ::-1, ::-1]])
    return a3, rs, m


def _hgrn_kernel(q_ref, ff_ref, fb_ref, v_ref, g_ref, lb_ref, gn_ref, a3_ref, rs_ref, mk_ref,
                 o_ref, acc_s, st_s, y_s, sc_s, dec_s, fw_s, dec3_s, *, layer_j, nchunk):
    L = CHUNK
    nlev = len(_LEVELS)
    f_refs = (ff_ref, fb_ref)
    Y_QPRE, Y_KSUF, Y_Q, Y_K = nlev, nlev + 1, nlev + 2, nlev + 3

    lbr = lb_ref[...]
    e = jnp.exp(lbr - jnp.max(lbr, axis=1, keepdims=True))
    soft = e / jnp.sum(e, axis=1, keepdims=True)
    cum = soft[:, 0:1, :]
    for i in range(1, layer_j + 1):
        cum = cum + soft[:, i:i + 1, :]
    lb = cum - soft[:, 0:1, :]
    log_lb = jnp.log(jnp.maximum(lb, LB_FLOOR))
    log_1m = jnp.log1p(-lb)

    st_s[...] = jnp.zeros_like(st_s)

    def chunk_rows(d, step):
        step = jnp.minimum(step, nchunk - 1)
        c = step if d == 0 else nchunk - 1 - step
        return pl.ds(pl.multiple_of(c * L, L), L)

    def stage1(d, step, slot):
        rows = chunk_rows(d, step)
        q = q_ref[rows, :]
        z = f_refs[d][rows, :]
        lf = _logaddexp(log_lb[d], log_1m[d] + _log_sigmoid(z))
        k = 1.0 - jnp.exp(lf)
        hi, mid, lo = _split3(lf)
        dall = _dot(a3_ref[d], jnp.concatenate([hi, mid, lo], axis=0))
        eall = jnp.exp(dall)
        for li in range(nlev):
            y_s[slot, d, li] = (jnp.where(rs_ref[d, li] > 0.5, q, k) * eall[li * L:(li + 1) * L]).astype(BF16)
        e_pre = eall[nlev * L:(nlev + 1) * L]
        e_suf = eall[(nlev + 1) * L:(nlev + 2) * L]
        y_s[slot, d, Y_QPRE] = (q * e_pre).astype(BF16)
        y_s[slot, d, Y_KSUF] = (k * e_suf).astype(BF16)
        y_s[slot, d, Y_Q] = q.astype(BF16)
        y_s[slot, d, Y_K] = k.astype(BF16)
        last = L - 1 if d == 0 else 0
        dec_s[slot, d] = jnp.broadcast_to(e_pre[last:last + 1, :], (8, HEAD))

    def stage2(d, slot):
        scores = _dot_nt(y_s[slot, d, Y_Q], y_s[slot, d, Y_K]) * mk_ref[d, nlev]
        for li in range(nlev):
            y = y_s[slot, d, li]
            scores = scores + _dot_nt(y, y) * mk_ref[d, li]
        sc_s[slot, d] = scores.astype(BF16)
        fw_s[slot, d, 0] = y_s[slot, d, Y_QPRE]
        fw_s[slot, d, 1] = y_s[slot, d, Y_KSUF]
        dec3_s[slot, d] = dec_s[slot, d]

    def stage3(d, step, slot, first_visit):
        rows = chunk_rows(d, step)
        v = v_ref[rows, :].astype(BF16)
        st = st_s[d]
        o = _dot(sc_s[slot, d], v) + _dot_nt(fw_s[slot, d, 0], st.astype(BF16))
        st_s[d] = st * dec3_s[slot, d][0:1, :] + _dot_tn(v, fw_s[slot, d, 1])
        if first_visit:
            acc_s[rows, :] = o
        else:
            acc_s[rows, :] = acc_s[rows, :] + o

    def make_body(first_visit):
        def body(it, carry):
            for par in range(2):
                for d in range(2):
                    stage3(d, 2 * it + par, par, first_visit)
            for par in range(2):
                for d in range(2):
                    stage2(d, par)
            for par in range(2):
                for d in range(2):
                    stage1(d, 2 * it + 4 + par, par)
            return carry
        return body

    for d in range(2):
        for par in range(2):
            stage1(d, par, par)
    for d in range(2):
        for par in range(2):
            stage2(d, par)
    for d in range(2):
        for par in range(2):
            stage1(d, 2 + par, par)
    half = nchunk // 2
    lax.fori_loop(0, half // 2, make_body(True), 0)
    lax.fori_loop(half // 2, half, make_body(False), 0)

    gn = gn_ref[...]
    blk = 8 * L

    def fin(i, carry):
        rows = pl.ds(pl.multiple_of(i * blk, blk), blk)
        o = acc_s[rows, :]
        g = g_ref[rows, :]
        o = o * lax.rsqrt(jnp.mean(o * o, axis=-1, keepdims=True) + GN_EPS)
        o_ref[rows, :] = (o * gn * (g * _sigmoid(g))).astype(o_ref.dtype)
        return carry

    lax.fori_loop(0, (nchunk * L) // blk, fin, 0)


def _hgrn(proj, lb_raw, gn, *, batch, seq, layer_j):
    nchunk = seq // CHUNK
    assert nchunk % 4 == 0 and seq % (8 * CHUNK) == 0
    a3, rs, mk = _hgrn_constants()
    n_even = lb_raw.shape[1]
    nh = N_REC_HEADS

    def col(kind):
        return pl.BlockSpec((seq, HEAD), lambda b, h: (b, kind * nh + h))

    const3 = lambda b, h: (0, 0, 0)
    const4 = lambda b, h: (0, 0, 0, 0)
    return pl.pallas_call(
        functools.partial(_hgrn_kernel, layer_j=layer_j, nchunk=nchunk),
        grid=(batch, nh),
        in_specs=[col(0), col(1), col(2), col(3), col(4),
                  pl.BlockSpec((2, n_even, HEAD), lambda b, h: (0, 0, h)),
                  pl.BlockSpec((1, HEAD), lambda b, h: (0, h)),
                  pl.BlockSpec(a3.shape, const3),
                  pl.BlockSpec(rs.shape, const4),
                  pl.BlockSpec(mk.shape, const4)],
        out_specs=pl.BlockSpec((seq, HEAD), lambda b, h: (b, h)),
        out_shape=jax.ShapeDtypeStruct((batch * seq, nh * HEAD), BF16),
        scratch_shapes=[pltpu.VMEM((seq, HEAD), F32), pltpu.VMEM((2, HEAD, HEAD), F32),
                        pltpu.VMEM((2, 2, len(_LEVELS) + 4, CHUNK, HEAD), BF16),
                        pltpu.VMEM((2, 2, CHUNK, CHUNK), BF16),
                        pltpu.VMEM((2, 2, 8, HEAD), F32),
                        pltpu.VMEM((2, 2, 2, CHUNK, HEAD), BF16),
                        pltpu.VMEM((2, 2, 8, HEAD), F32)],
        compiler_params=_cparams(2),
    )(proj, proj, proj, proj, proj, lb_raw, gn.reshape(1, -1),
      jnp.asarray(a3, BF16), jnp.asarray(rs), jnp.asarray(mk))


def _mlstm_constants():
    L = CHUNK
    t = np.arange(L)
    ut = (t[:, None] <= t[None, :]).astype(np.float32)
    cum = np.stack([ut, ut[::-1, ::-1]])
    tril = (t[None, :] <= t[:, None]).astype(np.float32)
    causal = np.stack([tril, tril[::-1, ::-1]])
    return cum, causal, np.eye(L, dtype=np.float32)


def _mlstm_kernel(xq_ref, xk_ref, v_ref, og_ref, gates_ref, gb_ref, cwq_ref, cwk_ref, gn_ref,
                  cum_ref, cm_ref, eye_ref, o_ref, q_s, k_s, acc_s, b_s, c_s,
                  bl_s, gm_s, mst_s, mnew_s, qkw_s, ks_s, col_s, wold_s, *, nchunk):
    L = CHUNK
    pad = CONV_W // 2
    eye = eye_ref[...] > 0.5

    def row_to_col(r):
        return jnp.sum(jnp.where(eye, jnp.broadcast_to(r, (L, L)), 0.0), axis=1, keepdims=True)

    def conv_chunk(c, carry):
        rows = pl.ds(pl.multiple_of(c * L, L), L)
        prev = pl.ds(pl.multiple_of(jnp.maximum(c * L - 8, 0), 8), 8)
        nxt = pl.ds(pl.multiple_of(jnp.minimum(c * L + L, (nchunk - 1) * L + L - 8), 8), 8)
        has_prev = (c > 0).astype(F32)
        has_next = (c < nchunk - 1).astype(F32)
        for x_ref, w_ref, dst, scale in ((xq_ref, cwq_ref, q_s, 1.0), (xk_ref, cwk_ref, k_s, HEAD ** -0.5)):
            win = jnp.concatenate([x_ref[prev, :] * has_prev, x_ref[rows, :], x_ref[nxt, :] * has_next], axis=0)
            w = w_ref[...]
            acc = win[8 - pad:8 - pad + L] * w[0:1, :]
            for j in range(1, CONV_W):
                acc = acc + win[8 - pad + j:8 - pad + j + L] * w[j:j + 1, :]
            y = acc * _sigmoid(acc)
            dst[rows, :] = y * scale if scale != 1.0 else y
        return carry

    lax.fori_loop(0, nchunk, conv_chunk, 0)

    for d in range(2):
        lf2 = _log_sigmoid(gates_ref[2 + d] + gb_ref[2 + d])
        hi, mid, lo = _split3(lf2)
        cm = cum_ref[d]
        b2 = _dot(hi, cm) + _dot(mid, cm) + _dot(lo, cm)
        b_s[d] = b2
        last = L - 1 if d == 0 else 0
        bl_s[d] = b2[:, last:last + 1]
        gm_s[d] = jnp.max(b2[:, last:last + 1] - b2 + gates_ref[d] + gb_ref[d], axis=1, keepdims=True)

    c_s[...] = jnp.zeros_like(c_s)
    lane = lax.broadcasted_iota(jnp.int32, (L, HEAD), 1)
    ones_col = (lane == 0).astype(F32)

    def stab_step(step, m):
        new = []
        for d in range(2):
            c = step if d == 0 else nchunk - 1 - step
            sl = pl.ds(c, 1)
            mst_s[d, sl, :] = m[d]
            m_new = jnp.maximum(bl_s[d, sl, :] + m[d], gm_s[d, sl, :])
            mnew_s[d, sl, :] = m_new
            new.append(m_new)
        return tuple(new)

    lax.fori_loop(0, nchunk, stab_step, (jnp.zeros((1, 1), F32), jnp.zeros((1, 1), F32)))

    def chunk_index(d, step):
        step = jnp.minimum(step, nchunk - 1)
        return step if d == 0 else nchunk - 1 - step

    def stage_a(d, step, slot):
        c = chunk_index(d, step)
        rows = pl.ds(pl.multiple_of(c * L, L), L)
        br = b_s[d, pl.ds(c, 1), :]
        li = gates_ref[d, pl.ds(c, 1), :] + gb_ref[d]
        m_st = mst_s[d, pl.ds(c, 1), :]
        m_new = mnew_s[d, pl.ds(c, 1), :]
        causal = cm_ref[d] > 0.5
        b_col = row_to_col(br)
        d_intra = jnp.where(causal, b_col + (li - br), NEG_BIG)
        d_inter = b_col + m_st
        m_t = jnp.maximum(d_inter, jnp.max(d_intra, axis=1, keepdims=True))
        w_intra = jnp.exp(d_intra - m_t)
        col_s[slot, d, 0] = jnp.exp(d_inter - m_t)
        col_s[slot, d, 1] = jnp.exp(-m_t)
        k = k_s[rows, :]
        qkw_s[slot, d] = (_dot_nt(q_s[rows, :].astype(BF16), k.astype(BF16)) * w_intra).astype(BF16)
        last = L - 1 if d == 0 else 0
        b_last = br[:, last:last + 1]
        w_s = row_to_col(jnp.exp(b_last - br + li - m_new))
        ks_s[slot, d] = (k * w_s).astype(BF16)
        wold_s[slot, d] = jnp.exp(b_last + m_st - m_new)

    def stage_b(d, step, slot, first_visit):
        c = chunk_index(d, step)
        rows = pl.ds(pl.multiple_of(c * L, L), L)
        vaug = jnp.concatenate([v_ref[rows, :], ones_col], axis=1).astype(BF16)
        cst = c_s[d]
        nd = _dot(qkw_s[slot, d], vaug) + col_s[slot, d, 0] * _dot(q_s[rows, :].astype(BF16), cst.astype(BF16))
        h = nd[:, :HEAD] / jnp.maximum(jnp.abs(nd[:, HEAD:HEAD + 1]), col_s[slot, d, 1])
        c_s[d] = wold_s[slot, d] * cst + _dot_tn(ks_s[slot, d], vaug)
        if first_visit:
            acc_s[rows, :] = h
        else:
            acc_s[rows, :] = acc_s[rows, :] + h

    def make_body(first_visit):
        def body(it, carry):
            for par in range(2):
                for d in range(2):
                    stage_b(d, 2 * it + par, par, first_visit)
            for par in range(2):
                for d in range(2):
                    stage_a(d, 2 * it + 2 + par, par)
            return carry
        return body

    for d in range(2):
        for par in range(2):
            stage_a(d, par, par)
    half = nchunk // 2
    lax.fori_loop(0, half // 2, make_body(True), 0)
    lax.fori_loop(half // 2, half, make_body(False), 0)

    gn = gn_ref[...]
    blk = 8 * L

    def fin(i, carry):
        rows = pl.ds(pl.multiple_of(i * blk, blk), blk)
        h = acc_s[rows, :]
        mu = jnp.mean(h, axis=-1, keepdims=True)
        cen = h - mu
        hn = cen * lax.rsqrt(jnp.mean(cen * cen, axis=-1, keepdims=True) + GN_EPS)
        o_ref[rows, :] = (hn * gn * _sigmoid(og_ref[rows, :])).astype(o_ref.dtype)
        return carry

    lax.fori_loop(0, (nchunk * L) // blk, fin, 0)


def _mlstm(proj, gates, gate_bias, conv_w, gn, *, batch, seq):
    nchunk = seq // CHUNK
    assert nchunk % 4 == 0 and seq % (8 * CHUNK) == 0
    nh = N_REC_HEADS
    width = nh * HEAD
    cum, causal, eye = _mlstm_constants()
    g5 = gates.reshape(batch, nchunk, CHUNK, 4, nh).transpose(0, 4, 3, 1, 2)
    gb = jnp.broadcast_to(gate_bias.reshape(4, nh).T[:, :, None, None], (nh, 4, 1, CHUNK))
    cw = jnp.pad(conv_w, ((0, 8 - CONV_W), (0, 0)))
    first_col = 5 * nh

    def col(kind):
        return pl.BlockSpec((seq, HEAD), lambda b, h: (b, first_col + kind * nh + h))

    const2 = lambda b, h: (0, 0)
    const3 = lambda b, h: (0, 0, 0)
    return pl.pallas_call(
        functools.partial(_mlstm_kernel, nchunk=nchunk),
        grid=(batch, nh),
        in_specs=[col(0), col(1), col(2), col(3),
                  pl.BlockSpec((None, None, 4, nchunk, CHUNK), lambda b, h: (b, h, 0, 0, 0)),
                  pl.BlockSpec((None, 4, 1, CHUNK), lambda b, h: (h, 0, 0, 0)),
                  pl.BlockSpec((8, HEAD), lambda b, h: (0, h)),
                  pl.BlockSpec((8, HEAD), lambda b, h: (0, nh + h)),
                  pl.BlockSpec((1, HEAD), lambda b, h: (0, h)),
                  pl.BlockSpec(cum.shape, const3),
                  pl.BlockSpec(causal.shape, const3),
                  pl.BlockSpec(eye.shape, const2)],
        out_specs=pl.BlockSpec((seq, HEAD), lambda b, h: (b, h)),
        out_shape=jax.ShapeDtypeStruct((batch * seq, width), BF16),
        scratch_shapes=[pltpu.VMEM((seq, HEAD), F32), pltpu.VMEM((seq, HEAD), F32),
                        pltpu.VMEM((seq, HEAD), F32), pltpu.VMEM((2, nchunk, CHUNK), F32),
                        pltpu.VMEM((2, HEAD, 2 * HEAD), F32)]
                       + [pltpu.VMEM((2, nchunk, 1), F32)] * 4
                       + [pltpu.VMEM((2, 2, CHUNK, CHUNK), BF16), pltpu.VMEM((2, 2, CHUNK, HEAD), BF16),
                          pltpu.VMEM((2, 2, 2, CHUNK, 1), F32), pltpu.VMEM((2, 2, 1, 1), F32)],
        compiler_params=_cparams(2),
    )(proj, proj, proj, proj, g5, gb, cw, cw, gn.reshape(1, -1),
      jnp.asarray(cum, BF16), jnp.asarray(causal), jnp.asarray(eye))


N_SLAB = 5
N_PAIR_TILES = 17


def _na_bias_tables(rpb):
    w = GRID_W
    qc = np.arange(w)[:, None]
    kc = np.arange(w)[None, :]
    c0 = np.clip(qc - WIN_C // 2, 0, w - WIN_C)
    valid = (kc >= c0) & (kc < c0 + WIN_C)
    cidx = np.clip(kc - qc + WIN_C - 1, 0, 2 * WIN_C - 2)
    tiles = jnp.where(jnp.asarray(valid)[None, None], rpb.astype(F32)[:, :, cidx] * LOG2_E, NEG_BIG)
    neg = jnp.full_like(tiles[:, 0], NEG_BIG)
    both = jnp.concatenate([tiles[:, :-1], tiles[:, 1:]], axis=-1)
    inner_first = WIN_R - 1 - WIN_R // 2
    inner_last = inner_first + WIN_R - 1
    left = jnp.concatenate([neg, tiles[:, inner_first]], axis=-1)[:, None]
    right = jnp.concatenate([tiles[:, inner_last], neg], axis=-1)[:, None]
    none = jnp.concatenate([neg, neg], axis=-1)[:, None]
    return jnp.concatenate([both, left, right, none], axis=1)


def _na_kernel(q_ref, kt_ref, v_ref, bias_ref, o_ref, s_even, s_odd, *, n_rows):
    w = GRID_W
    pr = 2 * w
    n_pairs = n_rows // 2
    lane_head = lax.broadcasted_iota(jnp.int32, (pr, HEAD), 1) // NA_DH
    head_masks = [lane_head == hh for hh in range(NA_GROUP)]

    def slab_start(p):
        return jnp.clip(p - 2, 0, n_pairs - N_SLAB)

    def logits(p, dst):
        q = q_ref[pl.ds(pl.multiple_of(p * pr, pr), pr), :]
        zero = jnp.zeros_like(q)
        qm = jnp.concatenate([jnp.where(head_masks[hh], q, zero) for hh in range(NA_GROUP)], axis=0)
        sp0 = slab_start(p)
        kt = kt_ref[:, pl.ds(pl.multiple_of(sp0 * pr, pr), N_SLAB * pr)]
        s = _dot(qm, kt)
        for i in range(N_SLAB):
            e_row = 2 * (sp0 + i)
            tiles = []
            for hh in range(NA_GROUP):
                for qr in range(2):
                    r = 2 * p + qr
                    r0 = jnp.clip(r - WIN_R // 2, 0, n_rows - WIN_R)
                    in_e = (e_row >= r0) & (e_row < r0 + WIN_R)
                    in_o = (e_row + 1 >= r0) & (e_row + 1 < r0 + WIN_R)
                    idx = jnp.where(in_e & in_o, e_row - r + WIN_R - 1,
                                    jnp.where(in_o, N_PAIR_TILES - 3, jnp.where(in_e, N_PAIR_TILES - 2, N_PAIR_TILES - 1)))
                    tiles.append(bias_ref[hh, idx])
            dst[i] = s[:, i * pr:(i + 1) * pr] + jnp.concatenate(tiles, axis=0)

    def attend(p, src):
        sp0 = slab_start(p)
        m = src[0]
        for i in range(1, N_SLAB):
            m = jnp.maximum(m, src[i])
        m = jnp.max(m, axis=1, keepdims=True)
        pexp = [jnp.exp2(src[i] - m) for i in range(N_SLAB)]
        l = pexp[0]
        for i in range(1, N_SLAB):
            l = l + pexp[i]
        pcat = jnp.concatenate([pe.astype(BF16) for pe in pexp] + [jnp.zeros((NA_GROUP * pr, pr), BF16)], axis=1)
        vs = v_ref[pl.ds(pl.multiple_of(sp0 * pr, pr), N_SLAB * pr), :]
        vcat = jnp.concatenate([vs, jnp.zeros((pr, HEAD), BF16)], axis=0)
        o = _dot(pcat, vcat) / jnp.sum(l, axis=1, keepdims=True)
        out = jnp.where(head_masks[0], o[0:pr], 0.0)
        for hh in range(1, NA_GROUP):
            out = out + jnp.where(head_masks[hh], o[hh * pr:(hh + 1) * pr], 0.0)
        o_ref[pl.ds(pl.multiple_of(p * pr, pr), pr), :] = out.astype(o_ref.dtype)

    logits(0, s_even)

    def two_pairs(k, carry):
        p = 2 * k
        logits(p + 1, s_odd)
        attend(p, s_even)
        logits(jnp.minimum(p + 2, n_pairs - 1), s_even)
        attend(p + 1, s_odd)
        return carry

    lax.fori_loop(0, n_pairs // 2, two_pairs, 0)


def _neighbourhood_attention(q, kt, v, bias, *, batch, seq):
    n_rows = seq // GRID_W
    assert n_rows % 4 == 0 and n_rows >= 2 * N_SLAB and n_rows >= WIN_R
    d_model = q.shape[1]
    n_groups = d_model // HEAD
    return pl.pallas_call(
        functools.partial(_na_kernel, n_rows=n_rows),
        grid=(batch, n_groups),
        in_specs=[pl.BlockSpec((seq, HEAD), lambda b, g: (b, g)),
                  pl.BlockSpec((HEAD, seq), lambda b, g: (g, b)),
                  pl.BlockSpec((seq, HEAD), lambda b, g: (b, g)),
                  pl.BlockSpec((NA_GROUP, N_PAIR_TILES, GRID_W, 2 * GRID_W), lambda b, g: (g, 0, 0, 0))],
        out_specs=pl.BlockSpec((seq, HEAD), lambda b, g: (b, g)),
        out_shape=jax.ShapeDtypeStruct((batch * seq, d_model), BF16),
        scratch_shapes=[pltpu.VMEM((N_SLAB, NA_GROUP * 2 * GRID_W, 2 * GRID_W), F32)] * 2,
        compiler_params=_cparams(2),
    )(q, kt, v, bias)


def _ffn_kernel(*refs, n_mix, alpha):
    x_ref = refs[0]
    mix_refs = refs[1:1 + 2 * n_mix]
    (lmg_ref, lmb_ref, lfg_ref, lfb_ref, wg_ref, wu_ref, wd_ref, o_ref, h_s, hb_s, acc_s) = refs[1 + 2 * n_mix:]
    j = pl.program_id(1)

    @pl.when(j == 0)
    def _():
        mix = _dot(mix_refs[0][...], mix_refs[1][...])
        for i in range(1, n_mix):
            mix = mix + _dot(mix_refs[2 * i][...], mix_refs[2 * i + 1][...])
        h = _layer_norm(alpha * x_ref[...] + mix, lmg_ref[...], lmb_ref[...])
        h_s[...] = h
        hb_s[...] = h.astype(BF16)
        acc_s[...] = jnp.zeros_like(acc_s)

    hb = hb_s[...]
    g = _dot(hb, wg_ref[...])
    u = _dot(hb, wu_ref[...])
    a = (g * _sigmoid(g) * u).astype(BF16)
    acc_s[...] += _dot(a, wd_ref[...])

    @pl.when(j == pl.num_programs(1) - 1)
    def _():
        o_ref[...] = _layer_norm(alpha * h_s[...] + acc_s[...], lfg_ref[...], lfb_ref[...])


def _mixer_out_ffn(x, mix_pairs, ln_mix_g, ln_mix_b, ln_ffn_g, ln_ffn_b, wg, wu, wd, *, alpha, tm, th):
    t, d = x.shape
    hid = wg.shape[1]
    assert t % tm == 0 and hid % th == 0
    row = lambda i, j: (i, 0)
    const = lambda i, j: (0, 0)
    in_specs = [pl.BlockSpec((tm, d), row)]
    args = [x]
    for o, w in mix_pairs:
        in_specs += [pl.BlockSpec((tm, o.shape[1]), row), pl.BlockSpec(w.shape, const)]
        args += [o, w]
    in_specs += [pl.BlockSpec((1, d), const)] * 4
    args += [ln_mix_g.reshape(1, d), ln_mix_b.reshape(1, d), ln_ffn_g.reshape(1, d), ln_ffn_b.reshape(1, d)]
    in_specs += [pl.BlockSpec((d, th), lambda i, j: (0, j)), pl.BlockSpec((d, th), lambda i, j: (0, j)),
                 pl.BlockSpec((th, d), lambda i, j: (j, 0))]
    args += [wg, wu, wd]
    return pl.pallas_call(
        functools.partial(_ffn_kernel, n_mix=len(mix_pairs), alpha=alpha),
        grid=(t // tm, hid // th),
        in_specs=in_specs,
        out_specs=pl.BlockSpec((tm, d), row),
        out_shape=jax.ShapeDtypeStruct((t, d), F32),
        scratch_shapes=[pltpu.VMEM((tm, d), F32), pltpu.VMEM((tm, d), BF16), pltpu.VMEM((tm, d), F32)],
        compiler_params=_cparams(2),
    )(*args)


def _row_tile(t):
    for tm in (1024, 512, 256, 128):
        if t % tm == 0:
            return tm
    raise ValueError(f"token count {t} is not a multiple of 128")


def kernel(x, w_in_even, gate_bias_even, lb_raw, conv_qk, gn_hgrn, gn_mlstm, w_out_even, w_qkv_odd, rpb_odd,
           w_out_odd, ln_mix_g, ln_mix_b, ln_ffn_g, ln_ffn_b, w_ffn_gate, w_ffn_up, w_ffn_down):
    batch, seq, d_model = x.shape
    depth = ln_mix_g.shape[0]
    alpha = (2.0 * depth) ** 0.25
    t = batch * seq
    tm = _row_tile(t)
    a_width = N_REC_HEADS * HEAD
    main_cols = 9 * a_width
    n_gate = 4 * N_REC_HEADS
    hid = w_ffn_gate.shape[-1]
    th = 256 if hid % 256 == 0 else 128
    tn_main = 1536 if main_cols % 1536 == 0 else 512

    h = x.reshape(t, d_model)
    for layer in range(depth):
        j = layer // 2
        if layer % 2 == 0:
            w_in = w_in_even[j]
            w_main = w_in[:, :main_cols].astype(BF16)
            w_gate = jnp.pad(w_in[:, main_cols:], ((0, 0), (0, V7X_LANES - n_gate))).astype(BF16)
            proj = _matmul(h, w_main, out_dtype=F32, tm=tm, tn=tn_main)
            gates = _matmul(h, w_gate, out_dtype=F32, tm=tm, tn=V7X_LANES)[:, :n_gate]
            o_a = _hgrn(proj, lb_raw, gn_hgrn[j], batch=batch, seq=seq, layer_j=j)
            h_b = _mlstm(proj, gates, gate_bias_even[j], conv_qk[j], gn_mlstm[j], batch=batch, seq=seq)
            w_out = w_out_even[j].astype(BF16)
            mix_pairs = [(o_a, w_out[:a_width]), (h_b, w_out[a_width:])]
        else:
            w_qkv = w_qkv_odd[j].astype(BF16)
            q = _matmul(h, w_qkv[:, :d_model], out_dtype=BF16, tm=tm, tn=d_model, scale=NA_DH ** -0.5 * LOG2_E)
            kt = _matmul(h, w_qkv[:, d_model:2 * d_model], out_dtype=BF16, tm=tm, tn=d_model, transpose_out=True)
            v = _matmul(h, w_qkv[:, 2 * d_model:], out_dtype=BF16, tm=tm, tn=d_model)
            bias = _na_bias_tables(rpb_odd[j])
            o = _neighbourhood_attention(q, kt, v, bias, batch=batch, seq=seq)
            mix_pairs = [(o, w_out_odd[j].astype(BF16))]
        h = _mixer_out_ffn(h, mix_pairs, ln_mix_g[layer], ln_mix_b[layer], ln_ffn_g[layer], ln_ffn_b[layer],
                           w_ffn_gate[layer].astype(BF16), w_ffn_up[layer].astype(BF16),
                           w_ffn_down[layer].astype(BF16), alpha=alpha, tm=tm, th=th)
    return h.reshape(batch, seq, d_model)
```

```python
import functools

import numpy as np
import jax
import jax.numpy as jnp
from jax import lax
from jax.experimental import pallas as pl
from jax.experimental.pallas import tpu as pltpu

F32 = jnp.float32
BF16 = jnp.bfloat16

GRID_W = 64
HEAD = 128
N_REC_HEADS = 4
CHUNK = 64
CONV_W = 5
NA_DH = 32
NA_GROUP = HEAD // NA_DH
WIN_R = 8
WIN_C = 16
LN_EPS = 1e-5
GN_EPS = 1e-6
NEG_BIG = -1e30
LB_FLOOR = 1e-30
LOG2_E = 1.4426950408889634

V7X_LANES = 128
V7X_VMEM_LIMIT_BYTES = 56 * 1024 * 1024

_LEVELS = (32, 16, 8, 4, 2, 1)


def _cparams(n_grid_axes):
    return pltpu.CompilerParams(
        dimension_semantics=("arbitrary",) * n_grid_axes,
        vmem_limit_bytes=V7X_VMEM_LIMIT_BYTES)


def _dot(a, b):
    return jnp.dot(a, b, preferred_element_type=F32)


def _dot_nt(a, b):
    return lax.dot_general(a, b, (((1,), (1,)), ((), ())), preferred_element_type=F32)


def _dot_tn(a, b):
    return lax.dot_general(a, b, (((0,), (0,)), ((), ())), preferred_element_type=F32)


def _split3(x):
    hi = x.astype(BF16)
    r1 = x - hi.astype(F32)
    mid = r1.astype(BF16)
    lo = (r1 - mid.astype(F32)).astype(BF16)
    return hi, mid, lo


def _log_sigmoid(z):
    return jnp.minimum(z, 0.0) - jnp.log1p(jnp.exp(-jnp.abs(z)))


def _logaddexp(a, b):
    return jnp.maximum(a, b) + jnp.log1p(jnp.exp(-jnp.abs(a - b)))


def _sigmoid(z):
    return 1.0 / (1.0 + jnp.exp(-z))


def _layer_norm(t, g, b):
    mu = jnp.mean(t, axis=-1, keepdims=True)
    c = t - mu
    var = jnp.mean(c * c, axis=-1, keepdims=True)
    return c * lax.rsqrt(var + LN_EPS) * g + b


def _mm_kernel(x_ref, w_ref, o_ref, *, scale, transpose_out):
    acc = _dot(x_ref[...].astype(BF16), w_ref[...])
    if scale != 1.0:
        acc = acc * scale
    if transpose_out:
        acc = acc.T
    o_ref[...] = acc.astype(o_ref.dtype)


def _matmul(x, w, *, out_dtype, tm, tn, scale=1.0, transpose_out=False):
    t, k = x.shape
    n = w.shape[1]
    assert t % tm == 0 and n % tn == 0
    if transpose_out:
        out_shape = jax.ShapeDtypeStruct((n, t), out_dtype)
        out_spec = pl.BlockSpec((tn, tm), lambda i, j: (j, i))
    else:
        out_shape = jax.ShapeDtypeStruct((t, n), out_dtype)
        out_spec = pl.BlockSpec((tm, tn), lambda i, j: (i, j))
    return pl.pallas_call(
        functools.partial(_mm_kernel, scale=scale, transpose_out=transpose_out),
        grid=(t // tm, n // tn),
        in_specs=[pl.BlockSpec((tm, k), lambda i, j: (i, 0)),
                  pl.BlockSpec((k, tn), lambda i, j: (0, j))],
        out_specs=out_spec,
        out_shape=out_shape,
        compiler_params=_cparams(2),
    )(x, w)


def _hgrn_constants():
    L = CHUNK
    t = np.arange(L)
    a_rows, rowsel, masks = [], [], []
    for c in _LEVELS:
        odd = (t // c) % 2 == 1
        rho = (t // (2 * c)) * 2 * c + c - 1
        u = t[None, :]
        a = np.where(odd[:, None], (u > rho[:, None]) & (u <= t[:, None]),
                     (u > t[:, None]) & (u <= rho[:, None]))
        a_rows.append(a.astype(np.float32))
        rowsel.append(np.broadcast_to(odd[:, None], (L, HEAD)).astype(np.float32))
        same = (t[:, None] // (2 * c)) == (t[None, :] // (2 * c))
        masks.append((odd[:, None] & ~odd[None, :] & same).astype(np.float32))
    masks.append(np.eye(L, dtype=np.float32))
    a_rows.append((t[None, :] <= t[:, None]).astype(np.float32))
    a_rows.append((t[None, :] > t[:, None]).astype(np.float32))
    a_f = np.stack(a_rows)
    rs_f = np.stack(rowsel)
    m_f = np.stack(masks)
    a = np.stack([a_f, a_f[:, ::-1, ::-1]]).reshape(2, 8 * L, L)
    a3 = np.concatenate([a, a, a], axis=-1)
    rs = np.stack([rs_f, rs_f[:, ::-1]])
    m = np.stack([m_f, m_f[:, ::-1, ::-1]])
    return a3, rs, m


def _hgrn_kernel(q_ref, ff_ref, fb_ref, v_ref, g_ref, lb_ref, gn_ref, a3_ref, rs_ref, mk_ref,
                 o_ref, acc_s, st_s, y_s, sc_s, dec_s, fw_s, dec3_s, *, layer_j, nchunk):
    L = CHUNK
    nlev = len(_LEVELS)
    f_refs = (ff_ref, fb_ref)
    Y_QPRE, Y_KSUF, Y_Q, Y_K = nlev, nlev + 1, nlev + 2, nlev + 3

    lbr = lb_ref[...]
    e = jnp.exp(lbr - jnp.max(lbr, axis=1, keepdims=True))
    soft = e / jnp.sum(e, axis=1, keepdims=True)
    cum = soft[:, 0:1, :]
    for i in range(1, layer_j + 1):
        cum = cum + soft[:, i:i + 1, :]
    lb = cum - soft[:, 0:1, :]
    log_lb = jnp.log(jnp.maximum(lb, LB_FLOOR))
    log_1m = jnp.log1p(-lb)

    st_s[...] = jnp.zeros_like(st_s)

    def chunk_rows(d, step):
        step = jnp.minimum(step, nchunk - 1)
        c = step if d == 0 else nchunk - 1 - step
        return pl.ds(pl.multiple_of(c * L, L), L)

    def stage1(d, step, slot):
        rows = chunk_rows(d, step)
        q = q_ref[rows, :]
        z = f_refs[d][rows, :]
        lf = _logaddexp(log_lb[d], log_1m[d] + _log_sigmoid(z))
        k = 1.0 - jnp.exp(lf)
        hi, mid, lo = _split3(lf)
        dall = _dot(a3_ref[d], jnp.concatenate([hi, mid, lo], axis=0))
        eall = jnp.exp(dall)
        for li in range(nlev):
            y_s[slot, d, li] = (jnp.where(rs_ref[d, li] > 0.5, q, k) * eall[li * L:(li + 1) * L]).astype(BF16)
        e_pre = eall[nlev * L:(nlev + 1) * L]
        e_suf = eall[(nlev + 1) * L:(nlev + 2) * L]
        y_s[slot, d, Y_QPRE] = (q * e_pre).astype(BF16)
        y_s[slot, d, Y_KSUF] = (k * e_suf).astype(BF16)
        y_s[slot, d, Y_Q] = q.astype(BF16)
        y_s[slot, d, Y_K] = k.astype(BF16)
        last = L - 1 if d == 0 else 0
        dec_s[slot, d] = jnp.broadcast_to(e_pre[last:last + 1, :], (8, HEAD))

    def stage2(d, slot):
        scores = _dot_nt(y_s[slot, d, Y_Q], y_s[slot, d, Y_K]) * mk_ref[d, nlev]
        for li in range(nlev):
            y = y_s[slot, d, li]
            scores = scores + _dot_nt(y, y) * mk_ref[d, li]
        sc_s[slot, d] = scores.astype(BF16)
        fw_s[slot, d, 0] = y_s[slot, d, Y_QPRE]
        fw_s[slot, d, 1] = y_s[slot, d, Y_KSUF]
        dec3_s[slot, d] = dec_s[slot, d]

    def stage3(d, step, slot, first_visit):
        rows = chunk_rows(d, step)
        v = v_ref[rows, :].astype(BF16)
        st = st_s[d]
        o = _dot(sc_s[slot, d], v) + _dot_nt(fw_s[slot, d, 0], st.astype(BF16))
        st_s[d] = st * dec3_s[slot, d][0:1, :] + _dot_tn(v, fw_s[slot, d, 1])
        if first_visit:
            acc_s[rows, :] = o
        else:
            acc_s[rows, :] = acc_s[rows, :] + o

    def make_body(first_visit):
        def body(it, carry):
            for par in range(2):
                for d in range(2):
                    stage3(d, 2 * it + par, par, first_visit)
            for par in range(2):
                for d in range(2):
                    stage2(d, par)
            for par in range(2):
                for d in range(2):
                    stage1(d, 2 * it + 4 + par, par)
            return carry
        return body

    for d in range(2):
        for par in range(2):
            stage1(d, par, par)
    for d in range(2):
        for par in range(2):
            stage2(d, par)
    for d in range(2):
        for par in range(2):
            stage1(d, 2 + par, par)
    half = nchunk // 2
    lax.fori_loop(0, half // 2, make_body(True), 0)
    lax.fori_loop(half // 2, half, make_body(False), 0)

    gn = gn_ref[...]
    blk = 8 * L

    def fin(i, carry):
        rows = pl.ds(pl.multiple_of(i * blk, blk), blk)
        o = acc_s[rows, :]
        g = g_ref[rows, :]
        o = o * lax.rsqrt(jnp.mean(o * o, axis=-1, keepdims=True) + GN_EPS)
        o_ref[rows, :] = (o * gn * (g * _sigmoid(g))).astype(o_ref.dtype)
        return carry

    lax.fori_loop(0, (nchunk * L) // blk, fin, 0)


def _hgrn(proj, lb_raw, gn, *, batch, seq, layer_j):
    nchunk = seq // CHUNK
    assert nchunk % 4 == 0 and seq % (8 * CHUNK) == 0
    a3, rs, mk = _hgrn_constants()
    n_even = lb_raw.shape[1]
    nh = N_REC_HEADS

    def col(kind):
        return pl.BlockSpec((seq, HEAD), lambda b, h: (b, kind * nh + h))

    const3 = lambda b, h: (0, 0, 0)
    const4 = lambda b, h: (0, 0, 0, 0)
    return pl.pallas_call(
        functools.partial(_hgrn_kernel, layer_j=layer_j, nchunk=nchunk),
        grid=(batch, nh),
        in_specs=[col(0), col(1), col(2), col(3), col(4),
                  pl.BlockSpec((2, n_even, HEAD), lambda b, h: (0, 0, h)),
                  pl.BlockSpec((1, HEAD), lambda b, h: (0, h)),
                  pl.BlockSpec(a3.shape, const3),
                  pl.BlockSpec(rs.shape, const4),
                  pl.BlockSpec(mk.shape, const4)],
        out_specs=pl.BlockSpec((seq, HEAD), lambda b, h: (b, h)),
        out_shape=jax.ShapeDtypeStruct((batch * seq, nh * HEAD), BF16),
        scratch_shapes=[pltpu.VMEM((seq, HEAD), F32), pltpu.VMEM((2, HEAD, HEAD), F32),
                        pltpu.VMEM((2, 2, len(_LEVELS) + 4, CHUNK, HEAD), BF16),
                        pltpu.VMEM((2, 2, CHUNK, CHUNK), BF16),
                        pltpu.VMEM((2, 2, 8, HEAD), F32),
                        pltpu.VMEM((2, 2, 2, CHUNK, HEAD), BF16),
                        pltpu.VMEM((2, 2, 8, HEAD), F32)],
        compiler_params=_cparams(2),
    )(proj, proj, proj, proj, proj, lb_raw, gn.reshape(1, -1),
      jnp.asarray(a3, BF16), jnp.asarray(rs), jnp.asarray(mk))


def _mlstm_constants():
    L = CHUNK
    t = np.arange(L)
    ut = (t[:, None] <= t[None, :]).astype(np.float32)
    cum = np.stack([ut, ut[::-1, ::-1]])
    tril = (t[None, :] <= t[:, None]).astype(np.float32)
    causal = np.stack([tril, tril[::-1, ::-1]])
    return cum, causal, np.eye(L, dtype=np.float32)


def _mlstm_kernel(xq_ref, xk_ref, v_ref, og_ref, gates_ref, gb_ref, cwq_ref, cwk_ref, gn_ref,
                  cum_ref, cm_ref, eye_ref, o_ref, q_s, k_s, acc_s, b_s, c_s,
                  bl_s, gm_s, mst_s, mnew_s, wold_s, qkw_s, inc_s, nd_s, wi_s, winter_s, wsc_s, floor_s,
                  *, nchunk):
    L = CHUNK
    pad = CONV_W // 2
    eye = eye_ref[...] > 0.5

    def conv_chunk(c, carry):
        rows = pl.ds(pl.multiple_of(c * L, L), L)
        prev = pl.ds(pl.multiple_of(jnp.maximum(c * L - 8, 0), 8), 8)
        nxt = pl.ds(pl.multiple_of(jnp.minimum(c * L + L, (nchunk - 1) * L + L - 8), 8), 8)
        has_prev = jnp.where(c > 0, 1.0, 0.0).astype(F32)
        has_next = jnp.where(c < nchunk - 1, 1.0, 0.0).astype(F32)
        for x_ref, w_ref, dst, scale in ((xq_ref, cwq_ref, q_s, 1.0), (xk_ref, cwk_ref, k_s, HEAD ** -0.5)):
            win = jnp.concatenate([x_ref[prev, :] * has_prev, x_ref[rows, :], x_ref[nxt, :] * has_next], axis=0)
            w = w_ref[...]
            acc = win[8 - pad:8 - pad + L] * w[0:1, :]
            for j in range(1, CONV_W):
                acc = acc + win[8 - pad + j:8 - pad + j + L] * w[j:j + 1, :]
            y = acc * _sigmoid(acc)
            dst[rows, :] = y * scale if scale != 1.0 else y
        return carry

    lax.fori_loop(0, nchunk, conv_chunk, 0)

    for d in range(2):
        lf2 = _log_sigmoid(gates_ref[2 + d] + gb_ref[2 + d])
        hi, mid, lo = _split3(lf2)
        cm = cum_ref[d]
        b2 = _dot(hi, cm) + _dot(mid, cm) + _dot(lo, cm)
        b_s[d] = b2
        last = L - 1 if d == 0 else 0
        bl_s[d] = b2[:, last:last + 1]
        gm_s[d] = jnp.max(b2[:, last:last + 1] - b2 + gates_ref[d] + gb_ref[d], axis=1, keepdims=True)

    c_s[...] = jnp.zeros_like(c_s)
    lane = lax.broadcasted_iota(jnp.int32, (L, HEAD), 1)
    ones_col = (lane == 0).astype(F32)

    def stab_step(step, m):
        new = []
        for d in range(2):
            c = step if d == 0 else nchunk - 1 - step
            sl = pl.ds(c, 1)
            mst_s[d, sl, :] = m[d]
            m_new = jnp.maximum(bl_s[d, sl, :] + m[d], gm_s[d, sl, :])
            mnew_s[d, sl, :] = m_new
            new.append(m_new)
        return tuple(new)

    lax.fori_loop(0, nchunk, stab_step, (jnp.zeros((1, 1), F32), jnp.zeros((1, 1), F32)))

    G = 8
    eye_g = jnp.concatenate([eye] * G, axis=0)

    def stack_rows(r):
        return jnp.concatenate([jnp.broadcast_to(r[j:j + 1], (L, r.shape[1])) for j in range(G)], axis=0)

    def rows_to_col(r):
        return jnp.sum(jnp.where(eye_g, stack_rows(r), 0.0), axis=1, keepdims=True)

    def gate_weights(g, carry):
        sl = pl.ds(pl.multiple_of(g * G, G), G)
        rows = pl.ds(pl.multiple_of(g * G * L, G * L), G * L)
        for d in range(2):
            br = b_s[d, sl, :]
            li = gates_ref[d, sl, :] + gb_ref[d]
            m_st = stack_rows(mst_s[d, sl, :])
            m_new = mnew_s[d, sl, :]
            causal = jnp.concatenate([cm_ref[d] > 0.5] * G, axis=0)
            am = jnp.where(causal, stack_rows(li - br), NEG_BIG)
            mx = jnp.maximum(m_st, jnp.max(am, axis=1, keepdims=True))
            wi_s[d, rows, :] = jnp.exp(am - mx)
            winter_s[d, rows, :] = jnp.broadcast_to(jnp.exp(m_st - mx), (G * L, HEAD))
            floor_s[d, rows, :] = jnp.exp(-(rows_to_col(br) + mx))
            last = L - 1 if d == 0 else 0
            b_last = br[:, last:last + 1]
            wsc_s[d, rows, :] = jnp.broadcast_to(rows_to_col(jnp.exp(b_last - br + li - m_new)), (G * L, HEAD))
            wold_s[d, sl, :] = jnp.exp(b_last + mst_s[d, sl, :] - m_new)
        return carry

    lax.fori_loop(0, nchunk // G, gate_weights, 0)

    def chunk_index(d, step):
        step = jnp.minimum(step, nchunk - 1)
        return step if d == 0 else nchunk - 1 - step

    def chunk_rows(d, step):
        return pl.ds(pl.multiple_of(chunk_index(d, step) * L, L), L)

    def value_aug(rows):
        return jnp.concatenate([v_ref[rows, :], ones_col], axis=1).astype(BF16)

    def stage_a(d, step, slot):
        rows = chunk_rows(d, step)
        k = k_s[rows, :]
        qkw_s[slot, d] = (_dot_nt(q_s[rows, :].astype(BF16), k.astype(BF16)) * wi_s[d, rows, :]).astype(BF16)
        inc_s[slot, d] = _dot_tn((k * wsc_s[d, rows, :]).astype(BF16), value_aug(rows))

    def stage_b(d, step, slot):
        rows = chunk_rows(d, step)
        cst = c_s[d]
        w_inter = winter_s[d, rows, :]
        nd_s[slot, d] = (_dot(qkw_s[slot, d], value_aug(rows))
                         + jnp.concatenate([w_inter, w_inter], axis=1)
                         * _dot(q_s[rows, :].astype(BF16), cst.astype(BF16)))
        c_s[d] = wold_s[d, pl.ds(chunk_index(d, step), 1), :] * cst + inc_s[slot, d]

    def stage_c(d, step, slot):
        rows = chunk_rows(d, step)
        nd = nd_s[slot, d]
        h = nd[:, :HEAD] / jnp.maximum(jnp.abs(nd[:, HEAD:HEAD + 1]), floor_s[d, rows, :])
        acc_s[rows, :] = acc_s[rows, :] + h

    def body(it, with_c):
        for par in range(2):
            for d in range(2):
                if with_c:
                    stage_c(d, 2 * it - 2 + par, par)
        for par in range(2):
            for d in range(2):
                stage_b(d, 2 * it + par, par)
        for par in range(2):
            for d in range(2):
                stage_a(d, 2 * it + 2 + par, par)

    def loop_body(it, carry):
        body(it, True)
        return carry

    acc_s[...] = jnp.zeros_like(acc_s)
    for d in range(2):
        for par in range(2):
            stage_a(d, par, par)
    body(0, False)
    lax.fori_loop(1, nchunk // 2, loop_body, 0)
    for d in range(2):
        for par in range(2):
            stage_c(d, nchunk - 2 + par, par)

    gn = gn_ref[...]
    blk = 8 * L

    def fin(i, carry):
        rows = pl.ds(pl.multiple_of(i * blk, blk), blk)
        h = acc_s[rows, :]
        mu = jnp.mean(h, axis=-1, keepdims=True)
        cen = h - mu
        hn = cen * lax.rsqrt(jnp.mean(cen * cen, axis=-1, keepdims=True) + GN_EPS)
        o_ref[rows, :] = (hn * gn * _sigmoid(og_ref[rows, :])).astype(o_ref.dtype)
        return carry

    lax.fori_loop(0, (nchunk * L) // blk, fin, 0)


def _mlstm(proj, gates, gate_bias, conv_w, gn, *, batch, seq):
    nchunk = seq // CHUNK
    assert nchunk % 8 == 0 and seq % (8 * CHUNK) == 0
    nh = N_REC_HEADS
    width = nh * HEAD
    cum, causal, eye = _mlstm_constants()
    g5 = gates.reshape(batch, nchunk, CHUNK, 4, nh).transpose(0, 4, 3, 1, 2)
    gb = jnp.broadcast_to(gate_bias.reshape(4, nh).T[:, :, None, None], (nh, 4, 1, CHUNK))
    cw = jnp.pad(conv_w, ((0, 8 - CONV_W), (0, 0)))
    first_col = 5 * nh

    def col(kind):
        return pl.BlockSpec((seq, HEAD), lambda b, h: (b, first_col + kind * nh + h))

    const2 = lambda b, h: (0, 0)
    const3 = lambda b, h: (0, 0, 0)
    return pl.pallas_call(
        functools.partial(_mlstm_kernel, nchunk=nchunk),
        grid=(batch, nh),
        in_specs=[col(0), col(1), col(2), col(3),
                  pl.BlockSpec((None, None, 4, nchunk, CHUNK), lambda b, h: (b, h, 0, 0, 0)),
                  pl.BlockSpec((None, 4, 1, CHUNK), lambda b, h: (h, 0, 0, 0)),
                  pl.BlockSpec((8, HEAD), lambda b, h: (0, h)),
                  pl.BlockSpec((8, HEAD), lambda b, h: (0, nh + h)),
                  pl.BlockSpec((1, HEAD), lambda b, h: (0, h)),
                  pl.BlockSpec(cum.shape, const3),
                  pl.BlockSpec(causal.shape, const3),
                  pl.BlockSpec(eye.shape, const2)],
        out_specs=pl.BlockSpec((seq, HEAD), lambda b, h: (b, h)),
        out_shape=jax.ShapeDtypeStruct((batch * seq, width), BF16),
        scratch_shapes=[pltpu.VMEM((seq, HEAD), F32), pltpu.VMEM((seq, HEAD), F32),
                        pltpu.VMEM((seq, HEAD), F32), pltpu.VMEM((2, nchunk, CHUNK), F32),
                        pltpu.VMEM((2, HEAD, 2 * HEAD), F32)]
                       + [pltpu.VMEM((2, nchunk, 1), F32)] * 5
                       + [pltpu.VMEM((2, 2, CHUNK, CHUNK), BF16), pltpu.VMEM((2, 2, HEAD, 2 * HEAD), F32),
                          pltpu.VMEM((2, 2, CHUNK, 2 * HEAD), F32),
                          pltpu.VMEM((2, seq, CHUNK), F32), pltpu.VMEM((2, seq, HEAD), F32),
                          pltpu.VMEM((2, seq, HEAD), F32), pltpu.VMEM((2, seq, 1), F32)],
        compiler_params=_cparams(2),
    )(proj, proj, proj, proj, g5, gb, cw, cw, gn.reshape(1, -1),
      jnp.asarray(cum, BF16), jnp.asarray(causal), jnp.asarray(eye))


N_SLAB = 5
N_PAIR_TILES = 17


def _na_bias_tables(rpb):
    w = GRID_W
    qc = np.arange(w)[:, None]
    kc = np.arange(w)[None, :]
    c0 = np.clip(qc - WIN_C // 2, 0, w - WIN_C)
    valid = (kc >= c0) & (kc < c0 + WIN_C)
    cidx = np.clip(kc - qc + WIN_C - 1, 0, 2 * WIN_C - 2)
    tiles = jnp.where(jnp.asarray(valid)[None, None], rpb.astype(F32)[:, :, cidx] * LOG2_E, NEG_BIG)
    neg = jnp.full_like(tiles[:, 0], NEG_BIG)
    both = jnp.concatenate([tiles[:, :-1], tiles[:, 1:]], axis=-1)
    inner_first = WIN_R - 1 - WIN_R // 2
    inner_last = inner_first + WIN_R - 1
    left = jnp.concatenate([neg, tiles[:, inner_first]], axis=-1)[:, None]
    right = jnp.concatenate([tiles[:, inner_last], neg], axis=-1)[:, None]
    none = jnp.concatenate([neg, neg], axis=-1)[:, None]
    return jnp.concatenate([both, left, right, none], axis=1)


def _na_kernel(q_ref, kt_ref, v_ref, bias_ref, o_ref, s_even, s_odd, *, n_rows):
    w = GRID_W
    pr = 2 * w
    n_pairs = n_rows // 2
    lane_head = lax.broadcasted_iota(jnp.int32, (pr, HEAD), 1) // NA_DH
    head_masks = [lane_head == hh for hh in range(NA_GROUP)]

    def slab_start(p):
        return jnp.clip(p - 2, 0, n_pairs - N_SLAB)

    def logits(p, dst):
        q = q_ref[pl.ds(pl.multiple_of(p * pr, pr), pr), :]
        zero = jnp.zeros_like(q)
        qm = jnp.concatenate([jnp.where(head_masks[hh], q, zero) for hh in range(NA_GROUP)], axis=0)
        sp0 = slab_start(p)
        kt = kt_ref[:, pl.ds(pl.multiple_of(sp0 * pr, pr), N_SLAB * pr)]
        s = _dot(qm, kt)
        for i in range(N_SLAB):
            e_row = 2 * (sp0 + i)
            tiles = []
            for hh in range(NA_GROUP):
                for qr in range(2):
                    r = 2 * p + qr
                    r0 = jnp.clip(r - WIN_R // 2, 0, n_rows - WIN_R)
                    in_e = (e_row >= r0) & (e_row < r0 + WIN_R)
                    in_o = (e_row + 1 >= r0) & (e_row + 1 < r0 + WIN_R)
                    idx = jnp.where(in_e & in_o, e_row - r + WIN_R - 1,
                                    jnp.where(in_o, N_PAIR_TILES - 3, jnp.where(in_e, N_PAIR_TILES - 2, N_PAIR_TILES - 1)))
                    tiles.append(bias_ref[hh, idx])
            dst[i] = s[:, i * pr:(i + 1) * pr] + jnp.concatenate(tiles, axis=0)

    def attend(p, src):
        sp0 = slab_start(p)
        m = src[0]
        for i in range(1, N_SLAB):
            m = jnp.maximum(m, src[i])
        m = jnp.max(m, axis=1, keepdims=True)
        pexp = [jnp.exp2(src[i] - m) for i in range(N_SLAB)]
        l = pexp[0]
        for i in range(1, N_SLAB):
            l = l + pexp[i]
        pcat = jnp.concatenate([pe.astype(BF16) for pe in pexp] + [jnp.zeros((NA_GROUP * pr, pr), BF16)], axis=1)
        vs = v_ref[pl.ds(pl.multiple_of(sp0 * pr, pr), N_SLAB * pr), :]
        vcat = jnp.concatenate([vs, jnp.zeros((pr, HEAD), BF16)], axis=0)
        o = _dot(pcat, vcat) / jnp.sum(l, axis=1, keepdims=True)
        out = jnp.where(head_masks[0], o[0:pr], 0.0)
        for hh in range(1, NA_GROUP):
            out = out + jnp.where(head_masks[hh], o[hh * pr:(hh + 1) * pr], 0.0)
        o_ref[pl.ds(pl.multiple_of(p * pr, pr), pr), :] = out.astype(o_ref.dtype)

    logits(0, s_even)

    def two_pairs(k, carry):
        p = 2 * k
        logits(p + 1, s_odd)
        attend(p, s_even)
        logits(jnp.minimum(p + 2, n_pairs - 1), s_even)
        attend(p + 1, s_odd)
        return carry

    lax.fori_loop(0, n_pairs // 2, two_pairs, 0)


def _neighbourhood_attention(q, kt, v, bias, *, batch, seq):
    n_rows = seq // GRID_W
    assert n_rows % 4 == 0 and n_rows >= 2 * N_SLAB and n_rows >= WIN_R
    d_model = q.shape[1]
    n_groups = d_model // HEAD
    return pl.pallas_call(
        functools.partial(_na_kernel, n_rows=n_rows),
        grid=(batch, n_groups),
        in_specs=[pl.BlockSpec((seq, HEAD), lambda b, g: (b, g)),
                  pl.BlockSpec((HEAD, seq), lambda b, g: (g, b)),
                  pl.BlockSpec((seq, HEAD), lambda b, g: (b, g)),
                  pl.BlockSpec((NA_GROUP, N_PAIR_TILES, GRID_W, 2 * GRID_W), lambda b, g: (g, 0, 0, 0))],
        out_specs=pl.BlockSpec((seq, HEAD), lambda b, g: (b, g)),
        out_shape=jax.ShapeDtypeStruct((batch * seq, d_model), BF16),
        scratch_shapes=[pltpu.VMEM((N_SLAB, NA_GROUP * 2 * GRID_W, 2 * GRID_W), F32)] * 2,
        compiler_params=_cparams(2),
    )(q, kt, v, bias)


def _ffn_kernel(*refs, n_mix, alpha, th):
    x_ref = refs[0]
    mix_refs = refs[1:1 + 2 * n_mix]
    lmg_ref, lmb_ref, lfg_ref, lfb_ref, wg_ref, wu_ref, wd_ref, o_ref = refs[1 + 2 * n_mix:]
    mix = _dot(mix_refs[0][...], mix_refs[1][...])
    for i in range(1, n_mix):
        mix = mix + _dot(mix_refs[2 * i][...], mix_refs[2 * i + 1][...])
    h = _layer_norm(alpha * x_ref[...] + mix, lmg_ref[...], lmb_ref[...])
    hb = h.astype(BF16)
    y = None
    for j in range(wg_ref.shape[1] // th):
        cols = slice(j * th, (j + 1) * th)
        g = _dot(hb, wg_ref[:, cols])
        u = _dot(hb, wu_ref[:, cols])
        part = _dot((g * _sigmoid(g) * u).astype(BF16), wd_ref[cols, :])
        y = part if y is None else y + part
    o_ref[...] = _layer_norm(alpha * h + y, lfg_ref[...], lfb_ref[...])


def _mixer_out_ffn(x, mix_pairs, ln_mix_g, ln_mix_b, ln_ffn_g, ln_ffn_b, wg, wu, wd, *, alpha, tm, th):
    t, d = x.shape
    hid = wg.shape[1]
    assert t % tm == 0 and hid % th == 0
    row = lambda i: (i, 0)
    const = lambda i: (0, 0)
    resident = functools.partial(pl.BlockSpec, index_map=const, pipeline_mode=pl.Buffered(1))
    in_specs = [pl.BlockSpec((tm, d), row)]
    args = [x]
    for o, w in mix_pairs:
        in_specs += [pl.BlockSpec((tm, o.shape[1]), row), resident(w.shape)]
        args += [o, w]
    in_specs += [resident((1, d))] * 4
    args += [ln_mix_g.reshape(1, d), ln_mix_b.reshape(1, d), ln_ffn_g.reshape(1, d), ln_ffn_b.reshape(1, d)]
    in_specs += [resident(wg.shape), resident(wu.shape), resident(wd.shape)]
    args += [wg, wu, wd]
    return pl.pallas_call(
        functools.partial(_ffn_kernel, n_mix=len(mix_pairs), alpha=alpha, th=th),
        grid=(t // tm,),
        in_specs=in_specs,
        out_specs=pl.BlockSpec((tm, d), row),
        out_shape=jax.ShapeDtypeStruct((t, d), F32),
        compiler_params=_cparams(1),
    )(*args)


def _row_tile(t):
    for tm in (1024, 512, 256, 128):
        if t % tm == 0:
            return tm
    raise ValueError(f"token count {t} is not a multiple of 128")


def kernel(x, w_in_even, gate_bias_even, lb_raw, conv_qk, gn_hgrn, gn_mlstm, w_out_even, w_qkv_odd, rpb_odd,
           w_out_odd, ln_mix_g, ln_mix_b, ln_ffn_g, ln_ffn_b, w_ffn_gate, w_ffn_up, w_ffn_down):
    batch, seq, d_model = x.shape
    depth = ln_mix_g.shape[0]
    alpha = (2.0 * depth) ** 0.25
    t = batch * seq
    tm = _row_tile(t)
    a_width = N_REC_HEADS * HEAD
    main_cols = 9 * a_width
    n_gate = 4 * N_REC_HEADS
    hid = w_ffn_gate.shape[-1]
    th = hid // 2 if hid % (2 * V7X_LANES) == 0 else hid
    tm_ffn = min(tm, 512)
    tn_main = 1536 if main_cols % 1536 == 0 else 512

    h = x.reshape(t, d_model)
    for layer in range(depth):
        j = layer // 2
        if layer % 2 == 0:
            w_in = w_in_even[j]
            w_main = w_in[:, :main_cols].astype(BF16)
            w_gate = jnp.pad(w_in[:, main_cols:], ((0, 0), (0, V7X_LANES - n_gate))).astype(BF16)
            proj = _matmul(h, w_main, out_dtype=F32, tm=tm, tn=tn_main)
            gates = _matmul(h, w_gate, out_dtype=F32, tm=tm, tn=V7X_LANES)[:, :n_gate]
            o_a = _hgrn(proj, lb_raw, gn_hgrn[j], batch=batch, seq=seq, layer_j=j)
            h_b = _mlstm(proj, gates, gate_bias_even[j], conv_qk[j], gn_mlstm[j], batch=batch, seq=seq)
            w_out = w_out_even[j].astype(BF16)
            mix_pairs = [(o_a, w_out[:a_width]), (h_b, w_out[a_width:])]
        else:
            w_qkv = w_qkv_odd[j].astype(BF16)
            q = _matmul(h, w_qkv[:, :d_model], out_dtype=BF16, tm=tm, tn=d_model, scale=NA_DH ** -0.5 * LOG2_E)
            kt = _matmul(h, w_qkv[:, d_model:2 * d_model], out_dtype=BF16, tm=tm, tn=d_model, transpose_out=True)
            v = _matmul(h, w_qkv[:, 2 * d_model:], out_dtype=BF16, tm=tm, tn=d_model)
            bias = _na_bias_tables(rpb_odd[j])
            o = _neighbourhood_attention(q, kt, v, bias, batch=batch, seq=seq)
            mix_pairs = [(o, w_out_odd[j].astype(BF16))]
        h = _mixer_out_ffn(h, mix_pairs, ln_mix_g[layer], ln_mix_b[layer], ln_ffn_g[layer], ln_ffn_b[layer],
                           w_ffn_gate[layer].astype(BF16), w_ffn_up[layer].astype(BF16),
                           w_ffn_down[layer].astype(BF16), alpha=alpha, tm=tm_ffn, th=th)
    return h.reshape(batch, seq, d_model)
```

```python
import functools

import numpy as np
import jax
import jax.numpy as jnp
from jax import lax
from jax.experimental import pallas as pl
from jax.experimental.pallas import tpu as pltpu

F32 = jnp.float32
BF16 = jnp.bfloat16

GRID_W = 64
HEAD = 128
N_REC_HEADS = 4
CHUNK = 64
CONV_W = 5
GATE_GROUP = 8
NA_DH = 32
NA_GROUP = HEAD // NA_DH
WIN_R = 8
WIN_C = 16
LN_EPS = 1e-5
GN_EPS = 1e-6
NEG_BIG = -1e30
LB_FLOOR = 1e-30
LOG2_E = 1.4426950408889634

V7X_LANES = 128
V7X_VMEM_LIMIT_BYTES = 56 * 1024 * 1024

_LEVELS = (32, 16, 8, 4, 2, 1)


def _cparams(n_grid_axes):
    return pltpu.CompilerParams(
        dimension_semantics=("arbitrary",) * n_grid_axes,
        vmem_limit_bytes=V7X_VMEM_LIMIT_BYTES)


def _dot(a, b):
    return jnp.dot(a, b, preferred_element_type=F32)


def _dot_nt(a, b):
    return lax.dot_general(a, b, (((1,), (1,)), ((), ())), preferred_element_type=F32)


def _dot_tn(a, b):
    return lax.dot_general(a, b, (((0,), (0,)), ((), ())), preferred_element_type=F32)


def _split3(x):
    hi = x.astype(BF16)
    r1 = x - hi.astype(F32)
    mid = r1.astype(BF16)
    lo = (r1 - mid.astype(F32)).astype(BF16)
    return hi, mid, lo


def _log_sigmoid(z):
    return jnp.minimum(z, 0.0) - jnp.log1p(jnp.exp(-jnp.abs(z)))


def _sigmoid(z):
    return 1.0 / (1.0 + jnp.exp(-z))


def _layer_norm(t, g, b):
    mu = jnp.mean(t, axis=-1, keepdims=True)
    c = t - mu
    var = jnp.mean(c * c, axis=-1, keepdims=True)
    return c * lax.rsqrt(var + LN_EPS) * g + b


def _mm_kernel(x_ref, w_ref, o_ref, *, scale, transpose_out):
    acc = _dot(x_ref[...].astype(BF16), w_ref[...])
    if scale != 1.0:
        acc = acc * scale
    if transpose_out:
        acc = acc.T
    o_ref[...] = acc.astype(o_ref.dtype)


def _matmul(x, w, *, out_dtype, tm, tn, scale=1.0, transpose_out=False):
    t, k = x.shape
    n = w.shape[1]
    assert t % tm == 0 and n % tn == 0
    if transpose_out:
        out_shape = jax.ShapeDtypeStruct((n, t), out_dtype)
        out_spec = pl.BlockSpec((tn, tm), lambda i, j: (j, i))
    else:
        out_shape = jax.ShapeDtypeStruct((t, n), out_dtype)
        out_spec = pl.BlockSpec((tm, tn), lambda i, j: (i, j))
    return pl.pallas_call(
        functools.partial(_mm_kernel, scale=scale, transpose_out=transpose_out),
        grid=(t // tm, n // tn),
        in_specs=[pl.BlockSpec((tm, k), lambda i, j: (i, 0)),
                  pl.BlockSpec((k, tn), lambda i, j: (0, j))],
        out_specs=out_spec,
        out_shape=out_shape,
        compiler_params=_cparams(2),
    )(x, w)


def _hgrn_constants():
    L = CHUNK
    t = np.arange(L)
    a_rows, rowsel, masks = [], [], []
    for c in _LEVELS:
        odd = (t // c) % 2 == 1
        rho = (t // (2 * c)) * 2 * c + c - 1
        u = t[None, :]
        a = np.where(odd[:, None], (u > rho[:, None]) & (u <= t[:, None]),
                     (u > t[:, None]) & (u <= rho[:, None]))
        a_rows.append(a.astype(np.float32))
        rowsel.append(np.broadcast_to(odd[:, None], (L, HEAD)).astype(np.float32))
        same = (t[:, None] // (2 * c)) == (t[None, :] // (2 * c))
        masks.append((odd[:, None] & ~odd[None, :] & same).astype(np.float32))
    masks.append(np.eye(L, dtype=np.float32))
    a_rows.append((t[None, :] <= t[:, None]).astype(np.float32))
    a_rows.append((t[None, :] > t[:, None]).astype(np.float32))
    a_f = np.stack(a_rows)
    rs_f = np.stack(rowsel)
    m_f = np.stack(masks)
    a = np.stack([a_f, a_f[:, ::-1, ::-1]]).reshape(2, 8 * L, L)
    a3 = np.concatenate([a, a, a, np.zeros_like(a)], axis=-1)
    rs = np.stack([rs_f, rs_f[:, ::-1]])
    m = np.stack([m_f, m_f[:, ::-1, ::-1]])
    return a3, rs, m


def _hgrn_kernel(q_ref, ff_ref, fb_ref, v_ref, g_ref, lb_ref, gn_ref, a3_ref, rs_ref, mk_ref,
                 o_ref, acc_s, st_s, y_s, sc_s, dec_s, fw_s, inc_s, dec3_s, *, layer_j, nchunk):
    L = CHUNK
    nlev = len(_LEVELS)
    f_refs = (ff_ref, fb_ref)
    Y_QPRE, Y_KSUF, Y_Q, Y_K = nlev, nlev + 1, nlev + 2, nlev + 3

    lbr = lb_ref[...]
    e = jnp.exp(lbr - jnp.max(lbr, axis=1, keepdims=True))
    soft = e / jnp.sum(e, axis=1, keepdims=True)
    cum = soft[:, 0:1, :]
    for i in range(1, layer_j + 1):
        cum = cum + soft[:, i:i + 1, :]
    lb = cum - soft[:, 0:1, :]
    lb_floor = jnp.maximum(lb, LB_FLOOR)
    one_m_lb = 1.0 - lb

    st_s[...] = jnp.zeros_like(st_s)

    def chunk_rows(d, step):
        step = jnp.minimum(step, nchunk - 1)
        c = step if d == 0 else nchunk - 1 - step
        return pl.ds(pl.multiple_of(c * L, L), L)

    def stage1(d, step0):
        qs, ks, parts = [], [], []
        for par in range(2):
            rows = chunk_rows(d, step0 + par)
            qs.append(q_ref[rows, :])
            f = lb_floor[d] + one_m_lb[d] * _sigmoid(f_refs[d][rows, :])
            ks.append(1.0 - f)
            hi, mid, lo = _split3(jnp.log(f) * LOG2_E)
            parts.append(jnp.concatenate([hi, mid, lo, jnp.zeros_like(hi)], axis=0))
        dall = _dot(a3_ref[d], jnp.concatenate(parts, axis=1))
        for slot in range(2):
            q, k = qs[slot], ks[slot]
            eall = jnp.exp2(dall[:, slot * HEAD:(slot + 1) * HEAD])
            for li, c in enumerate(_LEVELS):
                if c % 8 == 0:
                    first_q = 1 if d == 0 else 0
                    x = jnp.concatenate([(q if b % 2 == first_q else k)[b * c:(b + 1) * c] for b in range(L // c)],
                                        axis=0)
                else:
                    x = jnp.where(rs_ref[d, li] > 0.5, q, k)
                y_s[slot, d, li] = (x * eall[li * L:(li + 1) * L]).astype(BF16)
            e_pre = eall[nlev * L:(nlev + 1) * L]
            e_suf = eall[(nlev + 1) * L:(nlev + 2) * L]
            y_s[slot, d, Y_QPRE] = (q * e_pre).astype(BF16)
            y_s[slot, d, Y_KSUF] = (k * e_suf).astype(BF16)
            y_s[slot, d, Y_Q] = q.astype(BF16)
            y_s[slot, d, Y_K] = k.astype(BF16)
            last = L - 1 if d == 0 else 0
            dec_s[slot, d] = jnp.broadcast_to(e_pre[last:last + 1, :], (8, HEAD))

    def stage2(d, step, slot):
        scores = _dot_nt(y_s[slot, d, Y_Q], y_s[slot, d, Y_K]) * mk_ref[d, nlev]
        for li in range(nlev):
            y = y_s[slot, d, li]
            scores = scores + _dot_nt(y, y) * mk_ref[d, li]
        sc_s[slot, d] = scores.astype(BF16)
        fw_s[slot, d] = y_s[slot, d, Y_QPRE]
        inc_s[slot, d] = _dot_tn(v_ref[chunk_rows(d, step), :].astype(BF16), y_s[slot, d, Y_KSUF])
        dec3_s[slot, d] = dec_s[slot, d]

    def stage3(d, step, slot):
        rows = chunk_rows(d, step)
        st = st_s[d]
        o = _dot(sc_s[slot, d], v_ref[rows, :].astype(BF16)) + _dot_nt(fw_s[slot, d], st.astype(BF16))
        st_s[d] = st * dec3_s[slot, d][0:1, :] + inc_s[slot, d]
        acc_s[rows, :] = acc_s[rows, :] + o

    def body(it, carry):
        for par in range(2):
            for d in range(2):
                stage3(d, 2 * it + par, par)
        for par in range(2):
            for d in range(2):
                stage2(d, 2 * it + 2 + par, par)
        for d in range(2):
            stage1(d, 2 * it + 4)
        return carry

    acc_s[...] = jnp.zeros_like(acc_s)
    for d in range(2):
        stage1(d, 0)
    for d in range(2):
        for par in range(2):
            stage2(d, par, par)
    for d in range(2):
        stage1(d, 2)
    lax.fori_loop(0, nchunk // 2, body, 0)

    gn = gn_ref[...]
    blk = 8 * L

    def fin(i, carry):
        rows = pl.ds(pl.multiple_of(i * blk, blk), blk)
        o = acc_s[rows, :]
        g = g_ref[rows, :]
        o = o * lax.rsqrt(jnp.mean(o * o, axis=-1, keepdims=True) + GN_EPS)
        o_ref[rows, :] = (o * gn * (g * _sigmoid(g))).astype(o_ref.dtype)
        return carry

    lax.fori_loop(0, (nchunk * L) // blk, fin, 0)


def _hgrn(proj, lb_raw, gn, *, batch, seq, layer_j):
    nchunk = seq // CHUNK
    assert nchunk % 2 == 0 and seq % (8 * CHUNK) == 0
    a3, rs, mk = _hgrn_constants()
    n_even = lb_raw.shape[1]
    nh = N_REC_HEADS

    def col(kind):
        return pl.BlockSpec((seq, HEAD), lambda b, h: (b, kind * nh + h))

    const3 = lambda b, h: (0, 0, 0)
    const4 = lambda b, h: (0, 0, 0, 0)
    return pl.pallas_call(
        functools.partial(_hgrn_kernel, layer_j=layer_j, nchunk=nchunk),
        grid=(batch, nh),
        in_specs=[col(0), col(1), col(2), col(3), col(4),
                  pl.BlockSpec((2, n_even, HEAD), lambda b, h: (0, 0, h)),
                  pl.BlockSpec((1, HEAD), lambda b, h: (0, h)),
                  pl.BlockSpec(a3.shape, const3),
                  pl.BlockSpec(rs.shape, const4),
                  pl.BlockSpec(mk.shape, const4)],
        out_specs=pl.BlockSpec((seq, HEAD), lambda b, h: (b, h)),
        out_shape=jax.ShapeDtypeStruct((batch * seq, nh * HEAD), BF16),
        scratch_shapes=[pltpu.VMEM((seq, HEAD), F32), pltpu.VMEM((2, HEAD, HEAD), F32),
                        pltpu.VMEM((2, 2, len(_LEVELS) + 4, CHUNK, HEAD), BF16),
                        pltpu.VMEM((2, 2, CHUNK, CHUNK), BF16),
                        pltpu.VMEM((2, 2, 8, HEAD), F32),
                        pltpu.VMEM((2, 2, CHUNK, HEAD), BF16),
                        pltpu.VMEM((2, 2, HEAD, HEAD), F32),
                        pltpu.VMEM((2, 2, 8, HEAD), F32)],
        compiler_params=_cparams(2),
    )(proj, proj, proj, proj, proj, lb_raw, gn.reshape(1, -1),
      jnp.asarray(a3, BF16), jnp.asarray(rs), jnp.asarray(mk))


def _mlstm_constants():
    L = CHUNK
    t = np.arange(L)
    ut = (t[:, None] <= t[None, :]).astype(np.float32)
    cum = np.stack([ut, ut[::-1, ::-1]])
    tril = (t[None, :] <= t[:, None]).astype(np.float32)
    causal = np.stack([tril, tril[::-1, ::-1]])
    sel = np.zeros((HEAD, 4 * GATE_GROUP * HEAD), np.float32)
    for q in range(4):
        for p in range(3):
            for j in range(GATE_GROUP):
                sel[(q * 3 + p) * GATE_GROUP + j, (q * GATE_GROUP + j) * HEAD:(q * GATE_GROUP + j + 1) * HEAD] = 1.0
    return cum, causal, sel


def _mlstm_kernel(xq_ref, xk_ref, v_ref, og_ref, gates_ref, gb_ref, cwq_ref, cwk_ref, gn_ref,
                  cum_ref, cm_ref, sel_ref, o_ref, q_s, k_s, acc_s, b_s, c_s, cma_s,
                  bl_s, gm_s, mst_s, mnew_s, wold_s, qkw_s, inc_s, nd_s, wi_s, winter_s, wsc_s, floor_s,
                  *, nchunk):
    L = CHUNK
    pad = CONV_W // 2

    def conv_chunk(c, carry):
        rows = pl.ds(pl.multiple_of(c * L, L), L)
        prev = pl.ds(pl.multiple_of(jnp.maximum(c * L - 8, 0), 8), 8)
        nxt = pl.ds(pl.multiple_of(jnp.minimum(c * L + L, (nchunk - 1) * L + L - 8), 8), 8)
        has_prev = jnp.where(c > 0, 1.0, 0.0).astype(F32)
        has_next = jnp.where(c < nchunk - 1, 1.0, 0.0).astype(F32)
        for x_ref, w_ref, dst, scale in ((xq_ref, cwq_ref, q_s, 1.0), (xk_ref, cwk_ref, k_s, HEAD ** -0.5)):
            win = jnp.concatenate([x_ref[prev, :] * has_prev, x_ref[rows, :], x_ref[nxt, :] * has_next], axis=0)
            w = w_ref[...]
            acc = win[8 - pad:8 - pad + L] * w[0:1, :]
            for j in range(1, CONV_W):
                acc = acc + win[8 - pad + j:8 - pad + j + L] * w[j:j + 1, :]
            y = acc * _sigmoid(acc)
            dst[rows, :] = y * scale if scale != 1.0 else y
        return carry

    lax.fori_loop(0, nchunk, conv_chunk, 0)

    for d in range(2):
        lf2 = _log_sigmoid(gates_ref[2 + d] + gb_ref[2 + d])
        hi, mid, lo = _split3(lf2)
        cm = cum_ref[d]
        b2 = _dot(hi, cm) + _dot(mid, cm) + _dot(lo, cm)
        b_s[d] = b2
        last = L - 1 if d == 0 else 0
        bl_s[d] = b2[:, last:last + 1]
        li2 = gates_ref[d] + gb_ref[d]
        gm_s[d] = jnp.max(b2[:, last:last + 1] - b2 + li2, axis=1, keepdims=True)
        x = jnp.concatenate([li2 - b2, jnp.full((nchunk, HEAD - L), NEG_BIG, F32)], axis=1)
        lane_x = lax.broadcasted_iota(jnp.int32, x.shape, 1)
        sh = 1
        while sh < L:
            if d == 0:
                x = jnp.maximum(x, jnp.where(lane_x >= sh, pltpu.roll(x, sh, axis=1), NEG_BIG))
            else:
                x = jnp.maximum(x, pltpu.roll(x, HEAD - sh, axis=1))
            sh *= 2
        cma_s[d] = x[:, :L]

    c_s[...] = jnp.zeros_like(c_s)
    lane = lax.broadcasted_iota(jnp.int32, (L, HEAD), 1)
    ones_col = (lane == 0).astype(F32)

    def stab_step(step, m):
        new = []
        for d in range(2):
            c = step if d == 0 else nchunk - 1 - step
            sl = pl.ds(c, 1)
            mst_s[d, sl, :] = m[d]
            m_new = jnp.maximum(bl_s[d, sl, :] + m[d], gm_s[d, sl, :])
            mnew_s[d, sl, :] = m_new
            new.append(m_new)
        return tuple(new)

    lax.fori_loop(0, nchunk, stab_step, (jnp.zeros((1, 1), F32), jnp.zeros((1, 1), F32)))

    G = GATE_GROUP
    causal = [cm_ref[d] > 0.5 for d in range(2)]

    def gate_weights(g, carry):
        sl = pl.ds(pl.multiple_of(g * G, G), G)
        for d in range(2):
            br = b_s[d, sl, :]
            li = gates_ref[d, sl, :] + gb_ref[d]
            a = li - br
            m_st = mst_s[d, sl, :]
            m_new = mnew_s[d, sl, :]
            last = L - 1 if d == 0 else 0
            b_last = br[:, last:last + 1]
            mx = jnp.maximum(m_st, cma_s[d, sl, :])
            quantities = (mx, jnp.exp(m_st - mx), jnp.exp(-(br + mx)), jnp.exp(b_last - br + li - m_new))
            parts = [p.astype(F32) for qty in quantities for p in _split3(qty)]
            parts.append(jnp.zeros((HEAD - len(parts) * G, L), F32))
            cols = _dot_tn(jnp.concatenate(parts, axis=0).astype(BF16), sel_ref[...])
            for j in range(G):
                rows = pl.ds(pl.multiple_of((g * G + j) * L, L), L)
                mx_col, winter, floor, wsc = (cols[:, (q * G + j) * HEAD:(q * G + j + 1) * HEAD] for q in range(4))
                am = jnp.where(causal[d], jnp.broadcast_to(a[j:j + 1], (L, L)), NEG_BIG)
                wi_s[d, rows, :] = jnp.exp(am - mx_col[:, :L])
                winter_s[d, rows, :] = winter
                floor_s[d, rows, :] = floor[:, 0:1]
                wsc_s[d, rows, :] = wsc
            wold_s[d, sl, :] = jnp.exp(b_last + m_st - m_new)
        return carry

    lax.fori_loop(0, nchunk // G, gate_weights, 0)

    def chunk_index(d, step):
        step = jnp.minimum(step, nchunk - 1)
        return step if d == 0 else nchunk - 1 - step

    def chunk_rows(d, step):
        return pl.ds(pl.multiple_of(chunk_index(d, step) * L, L), L)

    def value_aug(rows):
        return jnp.concatenate([v_ref[rows, :], ones_col], axis=1).astype(BF16)

    def stage_a(d, step, slot):
        rows = chunk_rows(d, step)
        k = k_s[rows, :]
        qkw_s[slot, d] = (_dot_nt(q_s[rows, :].astype(BF16), k.astype(BF16)) * wi_s[d, rows, :]).astype(BF16)
        inc_s[slot, d] = _dot_tn((k * wsc_s[d, rows, :]).astype(BF16), value_aug(rows))

    def stage_b(d, step, slot):
        rows = chunk_rows(d, step)
        cst = c_s[d]
        w_inter = winter_s[d, rows, :]
        nd_s[slot, d] = (_dot(qkw_s[slot, d], value_aug(rows))
                         + jnp.concatenate([w_inter, w_inter], axis=1)
                         * _dot(q_s[rows, :].astype(BF16), cst.astype(BF16)))
        c_s[d] = wold_s[d, pl.ds(chunk_index(d, step), 1), :] * cst + inc_s[slot, d]

    def stage_c(d, step, slot):
        rows = chunk_rows(d, step)
        nd = nd_s[slot, d]
        h = nd[:, :HEAD] / jnp.maximum(jnp.abs(nd[:, HEAD:HEAD + 1]), floor_s[d, rows, :])
        acc_s[rows, :] = acc_s[rows, :] + h

    def body(it, with_c):
        for par in range(2):
            for d in range(2):
                if with_c:
                    stage_c(d, 2 * it - 2 + par, par)
        for par in range(2):
            for d in range(2):
                stage_b(d, 2 * it + par, par)
        for par in range(2):
            for d in range(2):
                stage_a(d, 2 * it + 2 + par, par)

    def loop_body(it, carry):
        body(it, True)
        return carry

    acc_s[...] = jnp.zeros_like(acc_s)
    for d in range(2):
        for par in range(2):
            stage_a(d, par, par)
    body(0, False)
    lax.fori_loop(1, nchunk // 2, loop_body, 0)
    for d in range(2):
        for par in range(2):
            stage_c(d, nchunk - 2 + par, par)

    gn = gn_ref[...]
    blk = 8 * L

    def fin(i, carry):
        rows = pl.ds(pl.multiple_of(i * blk, blk), blk)
        h = acc_s[rows, :]
        mu = jnp.mean(h, axis=-1, keepdims=True)
        cen = h - mu
        hn = cen * lax.rsqrt(jnp.mean(cen * cen, axis=-1, keepdims=True) + GN_EPS)
        o_ref[rows, :] = (hn * gn * _sigmoid(og_ref[rows, :])).astype(o_ref.dtype)
        return carry

    lax.fori_loop(0, (nchunk * L) // blk, fin, 0)


def _mlstm(proj, gates, gate_bias, conv_w, gn, *, batch, seq):
    nchunk = seq // CHUNK
    assert nchunk % GATE_GROUP == 0 and nchunk % 2 == 0 and seq % (8 * CHUNK) == 0
    nh = N_REC_HEADS
    width = nh * HEAD
    cum, causal, sel = _mlstm_constants()
    g5 = gates.reshape(batch, nchunk, CHUNK, 4, nh).transpose(0, 4, 3, 1, 2)
    gb = jnp.broadcast_to(gate_bias.reshape(4, nh).T[:, :, None, None], (nh, 4, 1, CHUNK))
    cw = jnp.pad(conv_w, ((0, 8 - CONV_W), (0, 0)))
    first_col = 5 * nh

    def col(kind):
        return pl.BlockSpec((seq, HEAD), lambda b, h: (b, first_col + kind * nh + h))

    const2 = lambda b, h: (0, 0)
    const3 = lambda b, h: (0, 0, 0)
    return pl.pallas_call(
        functools.partial(_mlstm_kernel, nchunk=nchunk),
        grid=(batch, nh),
        in_specs=[col(0), col(1), col(2), col(3),
                  pl.BlockSpec((None, None, 4, nchunk, CHUNK), lambda b, h: (b, h, 0, 0, 0)),
                  pl.BlockSpec((None, 4, 1, CHUNK), lambda b, h: (h, 0, 0, 0)),
                  pl.BlockSpec((8, HEAD), lambda b, h: (0, h)),
                  pl.BlockSpec((8, HEAD), lambda b, h: (0, nh + h)),
                  pl.BlockSpec((1, HEAD), lambda b, h: (0, h)),
                  pl.BlockSpec(cum.shape, const3),
                  pl.BlockSpec(causal.shape, const3),
                  pl.BlockSpec(sel.shape, const2)],
        out_specs=pl.BlockSpec((seq, HEAD), lambda b, h: (b, h)),
        out_shape=jax.ShapeDtypeStruct((batch * seq, width), BF16),
        scratch_shapes=[pltpu.VMEM((seq, HEAD), F32), pltpu.VMEM((seq, HEAD), F32),
                        pltpu.VMEM((seq, HEAD), F32), pltpu.VMEM((2, nchunk, CHUNK), F32),
                        pltpu.VMEM((2, HEAD, 2 * HEAD), F32), pltpu.VMEM((2, nchunk, CHUNK), F32)]
                       + [pltpu.VMEM((2, nchunk, 1), F32)] * 5
                       + [pltpu.VMEM((2, 2, CHUNK, CHUNK), BF16), pltpu.VMEM((2, 2, HEAD, 2 * HEAD), F32),
                          pltpu.VMEM((2, 2, CHUNK, 2 * HEAD), F32),
                          pltpu.VMEM((2, seq, CHUNK), F32), pltpu.VMEM((2, seq, HEAD), F32),
                          pltpu.VMEM((2, seq, HEAD), F32), pltpu.VMEM((2, seq, 1), F32)],
        compiler_params=_cparams(2),
    )(proj, proj, proj, proj, g5, gb, cw, cw, gn.reshape(1, -1),
      jnp.asarray(cum, BF16), jnp.asarray(causal), jnp.asarray(sel, BF16))


N_SLAB = 5
N_PAIR_TILES = 17


def _na_bias_tables(rpb):
    w = GRID_W
    qc = np.arange(w)[:, None]
    kc = np.arange(w)[None, :]
    c0 = np.clip(qc - WIN_C // 2, 0, w - WIN_C)
    valid = (kc >= c0) & (kc < c0 + WIN_C)
    cidx = np.clip(kc - qc + WIN_C - 1, 0, 2 * WIN_C - 2)
    tiles = jnp.where(jnp.asarray(valid)[None, None], rpb.astype(F32)[:, :, cidx] * LOG2_E, NEG_BIG)
    neg = jnp.full_like(tiles[:, 0], NEG_BIG)
    both = jnp.concatenate([tiles[:, :-1], tiles[:, 1:]], axis=-1)
    inner_first = WIN_R - 1 - WIN_R // 2
    inner_last = inner_first + WIN_R - 1
    left = jnp.concatenate([neg, tiles[:, inner_first]], axis=-1)[:, None]
    right = jnp.concatenate([tiles[:, inner_last], neg], axis=-1)[:, None]
    none = jnp.concatenate([neg, neg], axis=-1)[:, None]
    return jnp.concatenate([both, left, right, none], axis=1)


def _na_kernel(q_ref, kt_ref, v_ref, bias_ref, o_ref, s_even, s_odd, *, n_rows):
    w = GRID_W
    pr = 2 * w
    n_pairs = n_rows // 2
    lane_head = lax.broadcasted_iota(jnp.int32, (pr, HEAD), 1) // NA_DH
    head_masks = [lane_head == hh for hh in range(NA_GROUP)]

    def slab_start(p):
        return jnp.clip(p - 2, 0, n_pairs - N_SLAB)

    def logits(p, dst):
        q = q_ref[pl.ds(pl.multiple_of(p * pr, pr), pr), :]
        zero = jnp.zeros_like(q)
        qm = jnp.concatenate([jnp.where(head_masks[hh], q, zero) for hh in range(NA_GROUP)], axis=0)
        sp0 = slab_start(p)
        kt = kt_ref[:, pl.ds(pl.multiple_of(sp0 * pr, pr), N_SLAB * pr)]
        s = _dot(qm, kt)
        for i in range(N_SLAB):
            e_row = 2 * (sp0 + i)
            tiles = []
            for hh in range(NA_GROUP):
                for qr in range(2):
                    r = 2 * p + qr
                    r0 = jnp.clip(r - WIN_R // 2, 0, n_rows - WIN_R)
                    in_e = (e_row >= r0) & (e_row < r0 + WIN_R)
                    in_o = (e_row + 1 >= r0) & (e_row + 1 < r0 + WIN_R)
                    idx = jnp.where(in_e & in_o, e_row - r + WIN_R - 1,
                                    jnp.where(in_o, N_PAIR_TILES - 3, jnp.where(in_e, N_PAIR_TILES - 2, N_PAIR_TILES - 1)))
                    tiles.append(bias_ref[hh, idx])
            dst[i] = s[:, i * pr:(i + 1) * pr] + jnp.concatenate(tiles, axis=0)

    def attend(p, src):
        sp0 = slab_start(p)
        m = src[0]
        for i in range(1, N_SLAB):
            m = jnp.maximum(m, src[i])
        m = jnp.max(m, axis=1, keepdims=True)
        pexp = [jnp.exp2(src[i] - m) for i in range(N_SLAB)]
        l = pexp[0]
        for i in range(1, N_SLAB):
            l = l + pexp[i]
        pcat = jnp.concatenate([pe.astype(BF16) for pe in pexp] + [jnp.zeros((NA_GROUP * pr, pr), BF16)], axis=1)
        vs = v_ref[pl.ds(pl.multiple_of(sp0 * pr, pr), N_SLAB * pr), :]
        vcat = jnp.concatenate([vs, jnp.zeros((pr, HEAD), BF16)], axis=0)
        o = _dot(pcat, vcat) / jnp.sum(l, axis=1, keepdims=True)
        out = jnp.where(head_masks[0], o[0:pr], 0.0)
        for hh in range(1, NA_GROUP):
            out = out + jnp.where(head_masks[hh], o[hh * pr:(hh + 1) * pr], 0.0)
        o_ref[pl.ds(pl.multiple_of(p * pr, pr), pr), :] = out.astype(o_ref.dtype)

    logits(0, s_even)

    def two_pairs(k, carry):
        p = 2 * k
        logits(p + 1, s_odd)
        attend(p, s_even)
        logits(jnp.minimum(p + 2, n_pairs - 1), s_even)
        attend(p + 1, s_odd)
        return carry

    lax.fori_loop(0, n_pairs // 2, two_pairs, 0)


def _neighbourhood_attention(q, kt, v, bias, *, batch, seq):
    n_rows = seq // GRID_W
    assert n_rows % 4 == 0 and n_rows >= 2 * N_SLAB and n_rows >= WIN_R
    d_model = q.shape[1]
    n_groups = d_model // HEAD
    return pl.pallas_call(
        functools.partial(_na_kernel, n_rows=n_rows),
        grid=(batch, n_groups),
        in_specs=[pl.BlockSpec((seq, HEAD), lambda b, g: (b, g)),
                  pl.BlockSpec((HEAD, seq), lambda b, g: (g, b)),
                  pl.BlockSpec((seq, HEAD), lambda b, g: (b, g)),
                  pl.BlockSpec((NA_GROUP, N_PAIR_TILES, GRID_W, 2 * GRID_W), lambda b, g: (g, 0, 0, 0))],
        out_specs=pl.BlockSpec((seq, HEAD), lambda b, g: (b, g)),
        out_shape=jax.ShapeDtypeStruct((batch * seq, d_model), BF16),
        scratch_shapes=[pltpu.VMEM((N_SLAB, NA_GROUP * 2 * GRID_W, 2 * GRID_W), F32)] * 2,
        compiler_params=_cparams(2),
    )(q, kt, v, bias)


def _ffn_kernel(*refs, n_mix, alpha, th):
    x_ref = refs[0]
    mix_refs = refs[1:1 + 2 * n_mix]
    lmg_ref, lmb_ref, lfg_ref, lfb_ref, wg_ref, wu_ref, wd_ref, o_ref = refs[1 + 2 * n_mix:]
    mix = _dot(mix_refs[0][...], mix_refs[1][...])
    for i in range(1, n_mix):
        mix = mix + _dot(mix_refs[2 * i][...], mix_refs[2 * i + 1][...])
    h = _layer_norm(alpha * x_ref[...] + mix, lmg_ref[...], lmb_ref[...])
    hb = h.astype(BF16)
    y = None
    for j in range(wg_ref.shape[1] // th):
        cols = slice(j * th, (j + 1) * th)
        g = _dot(hb, wg_ref[:, cols])
        u = _dot(hb, wu_ref[:, cols])
        part = _dot((g * _sigmoid(g) * u).astype(BF16), wd_ref[cols, :])
        y = part if y is None else y + part
    o_ref[...] = _layer_norm(alpha * h + y, lfg_ref[...], lfb_ref[...])


def _mixer_out_ffn(x, mix_pairs, ln_mix_g, ln_mix_b, ln_ffn_g, ln_ffn_b, wg, wu, wd, *, alpha, tm, th):
    t, d = x.shape
    hid = wg.shape[1]
    assert t % tm == 0 and hid % th == 0
    row = lambda i: (i, 0)
    const = lambda i: (0, 0)
    resident = functools.partial(pl.BlockSpec, index_map=const, pipeline_mode=pl.Buffered(1))
    in_specs = [pl.BlockSpec((tm, d), row)]
    args = [x]
    for o, w in mix_pairs:
        in_specs += [pl.BlockSpec((tm, o.shape[1]), row), resident(w.shape)]
        args += [o, w]
    in_specs += [resident((1, d))] * 4
    args += [ln_mix_g.reshape(1, d), ln_mix_b.reshape(1, d), ln_ffn_g.reshape(1, d), ln_ffn_b.reshape(1, d)]
    in_specs += [resident(wg.shape), resident(wu.shape), resident(wd.shape)]
    args += [wg, wu, wd]
    return pl.pallas_call(
        functools.partial(_ffn_kernel, n_mix=len(mix_pairs), alpha=alpha, th=th),
        grid=(t // tm,),
        in_specs=in_specs,
        out_specs=pl.BlockSpec((tm, d), row),
        out_shape=jax.ShapeDtypeStruct((t, d), F32),
        compiler_params=_cparams(1),
    )(*args)


def _row_tile(t):
    for tm in (1024, 512, 256, 128):
        if t % tm == 0:
            return tm
    raise ValueError(f"token count {t} is not a multiple of 128")


def kernel(x, w_in_even, gate_bias_even, lb_raw, conv_qk, gn_hgrn, gn_mlstm, w_out_even, w_qkv_odd, rpb_odd,
           w_out_odd, ln_mix_g, ln_mix_b, ln_ffn_g, ln_ffn_b, w_ffn_gate, w_ffn_up, w_ffn_down):
    batch, seq, d_model = x.shape
    depth = ln_mix_g.shape[0]
    alpha = (2.0 * depth) ** 0.25
    t = batch * seq
    tm = _row_tile(t)
    a_width = N_REC_HEADS * HEAD
    main_cols = 9 * a_width
    n_gate = 4 * N_REC_HEADS
    hid = w_ffn_gate.shape[-1]
    th = hid // 2 if hid % (2 * V7X_LANES) == 0 else hid
    tm_ffn = min(tm, 512)
    tn_main = 1536 if main_cols % 1536 == 0 else 512

    h = x.reshape(t, d_model)
    for layer in range(depth):
        j = layer // 2
        if layer % 2 == 0:
            w_in = w_in_even[j]
            w_main = w_in[:, :main_cols].astype(BF16)
            w_gate = jnp.pad(w_in[:, main_cols:], ((0, 0), (0, V7X_LANES - n_gate))).astype(BF16)
            proj = _matmul(h, w_main, out_dtype=F32, tm=tm, tn=tn_main)
            gates = _matmul(h, w_gate, out_dtype=F32, tm=tm, tn=V7X_LANES)[:, :n_gate]
            o_a = _hgrn(proj, lb_raw, gn_hgrn[j], batch=batch, seq=seq, layer_j=j)
            h_b = _mlstm(proj, gates, gate_bias_even[j], conv_qk[j], gn_mlstm[j], batch=batch, seq=seq)
            w_out = w_out_even[j].astype(BF16)
            mix_pairs = [(o_a, w_out[:a_width]), (h_b, w_out[a_width:])]
        else:
            w_qkv = w_qkv_odd[j].astype(BF16)
            q = _matmul(h, w_qkv[:, :d_model], out_dtype=BF16, tm=tm, tn=d_model, scale=NA_DH ** -0.5 * LOG2_E)
            kt = _matmul(h, w_qkv[:, d_model:2 * d_model], out_dtype=BF16, tm=tm, tn=d_model, transpose_out=True)
            v = _matmul(h, w_qkv[:, 2 * d_model:], out_dtype=BF16, tm=tm, tn=d_model)
            bias = _na_bias_tables(rpb_odd[j])
            o = _neighbourhood_attention(q, kt, v, bias, batch=batch, seq=seq)
            mix_pairs = [(o, w_out_odd[j].astype(BF16))]
        h = _mixer_out_ffn(h, mix_pairs, ln_mix_g[layer], ln_mix_b[layer], ln_ffn_g[layer], ln_ffn_b[layer],
                           w_ffn_gate[layer].astype(BF16), w_ffn_up[layer].astype(BF16),
                           w_ffn_down[layer].astype(BF16), alpha=alpha, tm=tm_ffn, th=th)
    return h.reshape(batch, seq, d_model)
```

```python
import functools

import numpy as np
import jax
import jax.numpy as jnp
from jax import lax
from jax.experimental import pallas as pl
from jax.experimental.pallas import tpu as pltpu

F32 = jnp.float32
BF16 = jnp.bfloat16

GRID_W = 64
HEAD = 128
N_REC_HEADS = 4
CHUNK = 64
CONV_W = 5
GATE_GROUP = 8
NA_DH = 32
NA_GROUP = HEAD // NA_DH
WIN_R = 8
WIN_C = 16
LN_EPS = 1e-5
GN_EPS = 1e-6
NEG_BIG = -1e30
LB_FLOOR = 1e-30
LOG2_E = 1.4426950408889634

V7X_LANES = 128
V7X_VMEM_LIMIT_BYTES = 56 * 1024 * 1024

_LEVELS = (32, 16, 8, 4, 2, 1)


def _cparams(n_grid_axes):
    return pltpu.CompilerParams(
        dimension_semantics=("arbitrary",) * n_grid_axes,
        vmem_limit_bytes=V7X_VMEM_LIMIT_BYTES)


def _dot(a, b):
    return jnp.dot(a, b, preferred_element_type=F32)


def _dot_nt(a, b):
    return lax.dot_general(a, b, (((1,), (1,)), ((), ())), preferred_element_type=F32)


def _dot_tn(a, b):
    return lax.dot_general(a, b, (((0,), (0,)), ((), ())), preferred_element_type=F32)


def _split3(x):
    hi = x.astype(BF16)
    r1 = x - hi.astype(F32)
    mid = r1.astype(BF16)
    lo = (r1 - mid.astype(F32)).astype(BF16)
    return hi, mid, lo


def _log_sigmoid(z):
    return jnp.minimum(z, 0.0) - jnp.log1p(jnp.exp(-jnp.abs(z)))


def _sigmoid(z):
    return 1.0 / (1.0 + jnp.exp(-z))


def _layer_norm(t, g, b):
    mu = jnp.mean(t, axis=-1, keepdims=True)
    c = t - mu
    var = jnp.mean(c * c, axis=-1, keepdims=True)
    return c * lax.rsqrt(var + LN_EPS) * g + b


def _mm_kernel(x_ref, w_ref, o_ref):
    o_ref[...] = _dot(x_ref[...].astype(BF16), w_ref[...]).astype(o_ref.dtype)


def _matmul(x, w, *, out_dtype, tm, tn):
    t, k = x.shape
    n = w.shape[1]
    assert t % tm == 0 and n % tn == 0
    return pl.pallas_call(
        _mm_kernel,
        grid=(t // tm, n // tn),
        in_specs=[pl.BlockSpec((tm, k), lambda i, j: (i, 0)),
                  pl.BlockSpec((k, tn), lambda i, j: (0, j))],
        out_specs=pl.BlockSpec((tm, tn), lambda i, j: (i, j)),
        out_shape=jax.ShapeDtypeStruct((t, n), out_dtype),
        compiler_params=_cparams(2),
    )(x, w)


def _qkv_kernel(x_ref, w_ref, q_ref, kt_ref, v_ref, xb_s, *, q_scale):
    j = pl.program_id(1)

    @pl.when(j == 0)
    def _():
        xb_s[...] = x_ref[...].astype(BF16)

    acc = _dot(xb_s[...], w_ref[...])

    @pl.when(j == 0)
    def _():
        q_ref[...] = (acc * q_scale).astype(q_ref.dtype)

    @pl.when(j == 1)
    def _():
        kt_ref[...] = acc.T.astype(kt_ref.dtype)

    @pl.when(j == 2)
    def _():
        v_ref[...] = acc.astype(v_ref.dtype)


def _qkv_projection(x, w_qkv, *, q_scale, tm):
    t, k = x.shape
    d = w_qkv.shape[1] // 3
    assert t % tm == 0
    return pl.pallas_call(
        functools.partial(_qkv_kernel, q_scale=q_scale),
        grid=(t // tm, 3),
        in_specs=[pl.BlockSpec((tm, k), lambda i, j: (i, 0)),
                  pl.BlockSpec((k, d), lambda i, j: (0, j))],
        out_specs=[pl.BlockSpec((tm, d), lambda i, j: (i, 0)),
                   pl.BlockSpec((d, tm), lambda i, j: (0, i)),
                   pl.BlockSpec((tm, d), lambda i, j: (i, 0))],
        out_shape=[jax.ShapeDtypeStruct((t, d), BF16), jax.ShapeDtypeStruct((d, t), BF16),
                   jax.ShapeDtypeStruct((t, d), BF16)],
        scratch_shapes=[pltpu.VMEM((tm, k), BF16)],
        compiler_params=_cparams(2),
    )(x, w_qkv)


def _hgrn_constants():
    L = CHUNK
    t = np.arange(L)
    a_rows, rowsel, masks = [], [], []
    for c in _LEVELS:
        odd = (t // c) % 2 == 1
        rho = (t // (2 * c)) * 2 * c + c - 1
        u = t[None, :]
        a = np.where(odd[:, None], (u > rho[:, None]) & (u <= t[:, None]),
                     (u > t[:, None]) & (u <= rho[:, None]))
        a_rows.append(a.astype(np.float32))
        rowsel.append(np.broadcast_to(odd[:, None], (L, HEAD)).astype(np.float32))
        same = (t[:, None] // (2 * c)) == (t[None, :] // (2 * c))
        masks.append((odd[:, None] & ~odd[None, :] & same).astype(np.float32))
    masks.append(np.eye(L, dtype=np.float32))
    a_rows.append((t[None, :] <= t[:, None]).astype(np.float32))
    a_rows.append((t[None, :] > t[:, None]).astype(np.float32))
    a_f = np.stack(a_rows)
    rs_f = np.stack(rowsel)
    m_f = np.stack(masks)
    a = np.stack([a_f, a_f[:, ::-1, ::-1]]).reshape(2, 8 * L, L)
    a3 = np.concatenate([a, a, a, np.zeros_like(a)], axis=-1)
    rs = np.stack([rs_f, rs_f[:, ::-1]])
    m = np.stack([m_f, m_f[:, ::-1, ::-1]])
    return a3, rs, m


def _hgrn_kernel(q_ref, ff_ref, fb_ref, v_ref, g_ref, lb_ref, gn_ref, a3_ref, rs_ref, mk_ref,
                 o_ref, acc_s, st_s, y_s, sc_s, dec_s, fw_s, inc_s, dec3_s, *, layer_j, nchunk):
    L = CHUNK
    nlev = len(_LEVELS)
    f_refs = (ff_ref, fb_ref)
    Y_QPRE, Y_KSUF, Y_Q, Y_K = nlev, nlev + 1, nlev + 2, nlev + 3

    lbr = lb_ref[...]
    e = jnp.exp(lbr - jnp.max(lbr, axis=1, keepdims=True))
    soft = e / jnp.sum(e, axis=1, keepdims=True)
    cum = soft[:, 0:1, :]
    for i in range(1, layer_j + 1):
        cum = cum + soft[:, i:i + 1, :]
    lb = cum - soft[:, 0:1, :]
    lb_floor = jnp.maximum(lb, LB_FLOOR)
    one_m_lb = 1.0 - lb

    st_s[...] = jnp.zeros_like(st_s)

    def chunk_rows(d, step):
        step = jnp.minimum(step, nchunk - 1)
        c = step if d == 0 else nchunk - 1 - step
        return pl.ds(pl.multiple_of(c * L, L), L)

    def stage1(d, step0):
        qs, ks, parts = [], [], []
        for par in range(2):
            rows = chunk_rows(d, step0 + par)
            qs.append(q_ref[rows, :].astype(F32))
            f = lb_floor[d] + one_m_lb[d] * _sigmoid(f_refs[d][rows, :].astype(F32))
            ks.append(1.0 - f)
            hi, mid, lo = _split3(jnp.log(f) * LOG2_E)
            parts.append(jnp.concatenate([hi, mid, lo, jnp.zeros_like(hi)], axis=0))
        dall = _dot(a3_ref[d], jnp.concatenate(parts, axis=1))
        for slot in range(2):
            q, k = qs[slot], ks[slot]
            eall = jnp.exp2(dall[:, slot * HEAD:(slot + 1) * HEAD])
            for li, c in enumerate(_LEVELS):
                if c % 8 == 0:
                    first_q = 1 if d == 0 else 0
                    x = jnp.concatenate([(q if b % 2 == first_q else k)[b * c:(b + 1) * c] for b in range(L // c)],
                                        axis=0)
                else:
                    x = jnp.where(rs_ref[d, li] > 0.5, q, k)
                y_s[slot, d, li] = (x * eall[li * L:(li + 1) * L]).astype(BF16)
            e_pre = eall[nlev * L:(nlev + 1) * L]
            e_suf = eall[(nlev + 1) * L:(nlev + 2) * L]
            y_s[slot, d, Y_QPRE] = (q * e_pre).astype(BF16)
            y_s[slot, d, Y_KSUF] = (k * e_suf).astype(BF16)
            y_s[slot, d, Y_Q] = q.astype(BF16)
            y_s[slot, d, Y_K] = k.astype(BF16)
            last = L - 1 if d == 0 else 0
            dec_s[slot, d] = jnp.broadcast_to(e_pre[last:last + 1, :], (8, HEAD))

    def stage2(d, step, slot):
        scores = _dot_nt(y_s[slot, d, Y_Q], y_s[slot, d, Y_K]) * mk_ref[d, nlev]
        for li in range(nlev):
            y = y_s[slot, d, li]
            scores = scores + _dot_nt(y, y) * mk_ref[d, li]
        sc_s[slot, d] = scores.astype(BF16)
        fw_s[slot, d] = y_s[slot, d, Y_QPRE]
        inc_s[slot, d] = _dot_tn(v_ref[chunk_rows(d, step), :].astype(BF16), y_s[slot, d, Y_KSUF])
        dec3_s[slot, d] = dec_s[slot, d]

    def stage3(d, step, slot):
        rows = chunk_rows(d, step)
        st = st_s[d]
        o = _dot(sc_s[slot, d], v_ref[rows, :].astype(BF16)) + _dot_nt(fw_s[slot, d], st.astype(BF16))
        st_s[d] = st * dec3_s[slot, d][0:1, :] + inc_s[slot, d]
        acc_s[rows, :] = acc_s[rows, :] + o

    def body(it, carry):
        for par in range(2):
            for d in range(2):
                stage3(d, 2 * it + par, par)
        for par in range(2):
            for d in range(2):
                stage2(d, 2 * it + 2 + par, par)
        for d in range(2):
            stage1(d, 2 * it + 4)
        return carry

    acc_s[...] = jnp.zeros_like(acc_s)
    for d in range(2):
        stage1(d, 0)
    for d in range(2):
        for par in range(2):
            stage2(d, par, par)
    for d in range(2):
        stage1(d, 2)
    lax.fori_loop(0, nchunk // 2, body, 0)

    gn = gn_ref[...]
    blk = 8 * L

    def fin(i, carry):
        rows = pl.ds(pl.multiple_of(i * blk, blk), blk)
        o = acc_s[rows, :]
        g = g_ref[rows, :].astype(F32)
        o = o * lax.rsqrt(jnp.mean(o * o, axis=-1, keepdims=True) + GN_EPS)
        o_ref[rows, :] = (o * gn * (g * _sigmoid(g))).astype(o_ref.dtype)
        return carry

    lax.fori_loop(0, (nchunk * L) // blk, fin, 0)


def _hgrn(proj, lb_raw, gn, *, batch, seq, layer_j):
    nchunk = seq // CHUNK
    assert nchunk % 2 == 0 and seq % (8 * CHUNK) == 0
    a3, rs, mk = _hgrn_constants()
    n_even = lb_raw.shape[1]
    nh = N_REC_HEADS

    def col(kind):
        return pl.BlockSpec((seq, HEAD), lambda b, h: (b, kind * nh + h))

    const3 = lambda b, h: (0, 0, 0)
    const4 = lambda b, h: (0, 0, 0, 0)
    return pl.pallas_call(
        functools.partial(_hgrn_kernel, layer_j=layer_j, nchunk=nchunk),
        grid=(batch, nh),
        in_specs=[col(0), col(1), col(2), col(3), col(4),
                  pl.BlockSpec((2, n_even, HEAD), lambda b, h: (0, 0, h)),
                  pl.BlockSpec((1, HEAD), lambda b, h: (0, h)),
                  pl.BlockSpec(a3.shape, const3),
                  pl.BlockSpec(rs.shape, const4),
                  pl.BlockSpec(mk.shape, const4)],
        out_specs=pl.BlockSpec((seq, HEAD), lambda b, h: (b, h)),
        out_shape=jax.ShapeDtypeStruct((batch * seq, nh * HEAD), BF16),
        scratch_shapes=[pltpu.VMEM((seq, HEAD), F32), pltpu.VMEM((2, HEAD, HEAD), F32),
                        pltpu.VMEM((2, 2, len(_LEVELS) + 4, CHUNK, HEAD), BF16),
                        pltpu.VMEM((2, 2, CHUNK, CHUNK), BF16),
                        pltpu.VMEM((2, 2, 8, HEAD), F32),
                        pltpu.VMEM((2, 2, CHUNK, HEAD), BF16),
                        pltpu.VMEM((2, 2, HEAD, HEAD), F32),
                        pltpu.VMEM((2, 2, 8, HEAD), F32)],
        compiler_params=_cparams(2),
    )(proj, proj, proj, proj, proj, lb_raw, gn.reshape(1, -1),
      jnp.asarray(a3, BF16), jnp.asarray(rs), jnp.asarray(mk))


def _mlstm_constants():
    L = CHUNK
    t = np.arange(L)
    ut = (t[:, None] <= t[None, :]).astype(np.float32)
    cum = np.stack([ut, ut[::-1, ::-1]])
    tril = (t[None, :] <= t[:, None]).astype(np.float32)
    causal = np.stack([tril, tril[::-1, ::-1]])
    sel = np.zeros((HEAD, 4 * GATE_GROUP * HEAD), np.float32)
    for q in range(4):
        for p in range(3):
            for j in range(GATE_GROUP):
                sel[(q * 3 + p) * GATE_GROUP + j, (q * GATE_GROUP + j) * HEAD:(q * GATE_GROUP + j + 1) * HEAD] = 1.0
    return cum, causal, sel


def _mlstm_kernel(xq_ref, xk_ref, v_ref, og_ref, gates_ref, gb_ref, cwq_ref, cwk_ref, gn_ref,
                  cum_ref, cm_ref, sel_ref, o_ref, q_s, k_s, acc_s, b_s, c_s, cma_s,
                  bl_s, gm_s, mst_s, mnew_s, wold_s, qkw_s, inc_s, nd_s, wi_s, winter_s, wsc_s, floor_s,
                  *, nchunk):
    L = CHUNK
    pad = CONV_W // 2
    halo = 16

    def conv_chunk(c, carry):
        rows = pl.ds(pl.multiple_of(c * L, L), L)
        prev = pl.ds(pl.multiple_of(jnp.maximum(c * L - halo, 0), halo), halo)
        nxt = pl.ds(pl.multiple_of(jnp.minimum(c * L + L, nchunk * L - halo), halo), halo)
        has_prev = jnp.where(c > 0, 1.0, 0.0).astype(F32)
        has_next = jnp.where(c < nchunk - 1, 1.0, 0.0).astype(F32)
        for x_ref, w_ref, dst, scale in ((xq_ref, cwq_ref, q_s, 1.0), (xk_ref, cwk_ref, k_s, HEAD ** -0.5)):
            win = jnp.concatenate([x_ref[prev, :].astype(F32) * has_prev, x_ref[rows, :].astype(F32),
                                   x_ref[nxt, :].astype(F32) * has_next], axis=0)
            w = w_ref[...]
            acc = win[halo - pad:halo - pad + L] * w[0:1, :]
            for j in range(1, CONV_W):
                acc = acc + win[halo - pad + j:halo - pad + j + L] * w[j:j + 1, :]
            y = acc * _sigmoid(acc)
            dst[rows, :] = y * scale if scale != 1.0 else y
        return carry

    lax.fori_loop(0, nchunk, conv_chunk, 0)

    for d in range(2):
        lf2 = _log_sigmoid(gates_ref[2 + d] + gb_ref[2 + d])
        hi, mid, lo = _split3(lf2)
        cm = cum_ref[d]
        b2 = _dot(hi, cm) + _dot(mid, cm) + _dot(lo, cm)
        b_s[d] = b2
        last = L - 1 if d == 0 else 0
        bl_s[d] = b2[:, last:last + 1]
        li2 = gates_ref[d] + gb_ref[d]
        gm_s[d] = jnp.max(b2[:, last:last + 1] - b2 + li2, axis=1, keepdims=True)
        x = jnp.concatenate([li2 - b2, jnp.full((nchunk, HEAD - L), NEG_BIG, F32)], axis=1)
        lane_x = lax.broadcasted_iota(jnp.int32, x.shape, 1)
        sh = 1
        while sh < L:
            if d == 0:
                x = jnp.maximum(x, jnp.where(lane_x >= sh, pltpu.roll(x, sh, axis=1), NEG_BIG))
            else:
                x = jnp.maximum(x, pltpu.roll(x, HEAD - sh, axis=1))
            sh *= 2
        cma_s[d] = x[:, :L]

    c_s[...] = jnp.zeros_like(c_s)
    lane = lax.broadcasted_iota(jnp.int32, (L, HEAD), 1)
    ones_col = (lane == 0).astype(BF16)

    def stab_step(step, m):
        new = []
        for d in range(2):
            c = step if d == 0 else nchunk - 1 - step
            sl = pl.ds(c, 1)
            mst_s[d, sl, :] = m[d]
            m_new = jnp.maximum(bl_s[d, sl, :] + m[d], gm_s[d, sl, :])
            mnew_s[d, sl, :] = m_new
            new.append(m_new)
        return tuple(new)

    lax.fori_loop(0, nchunk, stab_step, (jnp.zeros((1, 1), F32), jnp.zeros((1, 1), F32)))

    G = GATE_GROUP
    causal = [cm_ref[d] > 0.5 for d in range(2)]

    def gate_weights(g, carry):
        sl = pl.ds(pl.multiple_of(g * G, G), G)
        for d in range(2):
            br = b_s[d, sl, :]
            li = gates_ref[d, sl, :] + gb_ref[d]
            a = li - br
            m_st = mst_s[d, sl, :]
            m_new = mnew_s[d, sl, :]
            last = L - 1 if d == 0 else 0
            b_last = br[:, last:last + 1]
            mx = jnp.maximum(m_st, cma_s[d, sl, :])
            quantities = (mx, jnp.exp(m_st - mx), jnp.exp(-(br + mx)), jnp.exp(b_last - br + li - m_new))
            parts = [p.astype(F32) for qty in quantities for p in _split3(qty)]
            parts.append(jnp.zeros((HEAD - len(parts) * G, L), F32))
            cols = _dot_tn(jnp.concatenate(parts, axis=0).astype(BF16), sel_ref[...])
            for j in range(G):
                rows = pl.ds(pl.multiple_of((g * G + j) * L, L), L)
                mx_col, winter, floor, wsc = (cols[:, (q * G + j) * HEAD:(q * G + j + 1) * HEAD] for q in range(4))
                am = jnp.where(causal[d], jnp.broadcast_to(a[j:j + 1], (L, L)), NEG_BIG)
                wi_s[d, rows, :] = jnp.exp(am - mx_col[:, :L])
                winter_s[d, rows, :] = winter
                floor_s[d, rows, :] = floor[:, 0:1]
                wsc_s[d, rows, :] = wsc
            wold_s[d, sl, :] = jnp.exp(b_last + m_st - m_new)
        return carry

    lax.fori_loop(0, nchunk // G, gate_weights, 0)

    def chunk_index(d, step):
        step = jnp.minimum(step, nchunk - 1)
        return step if d == 0 else nchunk - 1 - step

    def chunk_rows(d, step):
        return pl.ds(pl.multiple_of(chunk_index(d, step) * L, L), L)

    def value_aug(rows):
        return jnp.concatenate([v_ref[rows, :], ones_col], axis=1)

    def stage_a(d, step, slot):
        rows = chunk_rows(d, step)
        k = k_s[rows, :]
        qkw_s[slot, d] = (_dot_nt(q_s[rows, :].astype(BF16), k.astype(BF16)) * wi_s[d, rows, :]).astype(BF16)
        inc_s[slot, d] = _dot_tn((k * wsc_s[d, rows, :]).astype(BF16), value_aug(rows))

    def stage_b(d, step, slot):
        rows = chunk_rows(d, step)
        cst = c_s[d]
        w_inter = winter_s[d, rows, :]
        nd_s[slot, d] = (_dot(qkw_s[slot, d], value_aug(rows))
                         + jnp.concatenate([w_inter, w_inter], axis=1)
                         * _dot(q_s[rows, :].astype(BF16), cst.astype(BF16)))
        c_s[d] = wold_s[d, pl.ds(chunk_index(d, step), 1), :] * cst + inc_s[slot, d]

    def stage_c(d, step, slot):
        rows = chunk_rows(d, step)
        nd = nd_s[slot, d]
        h = nd[:, :HEAD] / jnp.maximum(jnp.abs(nd[:, HEAD:HEAD + 1]), floor_s[d, rows, :])
        acc_s[rows, :] = acc_s[rows, :] + h

    def body(it, with_c):
        for par in range(2):
            for d in range(2):
                if with_c:
                    stage_c(d, 2 * it - 2 + par, par)
        for par in range(2):
            for d in range(2):
                stage_b(d, 2 * it + par, par)
        for par in range(2):
            for d in range(2):
                stage_a(d, 2 * it + 2 + par, par)

    def loop_body(it, carry):
        body(it, True)
        return carry

    acc_s[...] = jnp.zeros_like(acc_s)
    for d in range(2):
        for par in range(2):
            stage_a(d, par, par)
    body(0, False)
    lax.fori_loop(1, nchunk // 2, loop_body, 0)
    for d in range(2):
        for par in range(2):
            stage_c(d, nchunk - 2 + par, par)

    gn = gn_ref[...]
    blk = 8 * L

    def fin(i, carry):
        rows = pl.ds(pl.multiple_of(i * blk, blk), blk)
        h = acc_s[rows, :]
        mu = jnp.mean(h, axis=-1, keepdims=True)
        cen = h - mu
        hn = cen * lax.rsqrt(jnp.mean(cen * cen, axis=-1, keepdims=True) + GN_EPS)
        o_ref[rows, :] = (hn * gn * _sigmoid(og_ref[rows, :].astype(F32))).astype(o_ref.dtype)
        return carry

    lax.fori_loop(0, (nchunk * L) // blk, fin, 0)


def _mlstm(proj, gates, gate_bias, conv_w, gn, *, batch, seq):
    nchunk = seq // CHUNK
    assert nchunk % GATE_GROUP == 0 and nchunk % 2 == 0 and seq % (8 * CHUNK) == 0
    nh = N_REC_HEADS
    width = nh * HEAD
    cum, causal, sel = _mlstm_constants()
    g5 = gates.reshape(batch, nchunk, CHUNK, 4, nh).transpose(0, 4, 3, 1, 2)
    gb = jnp.broadcast_to(gate_bias.reshape(4, nh).T[:, :, None, None], (nh, 4, 1, CHUNK))
    cw = jnp.pad(conv_w, ((0, 8 - CONV_W), (0, 0)))
    first_col = 5 * nh

    def col(kind):
        return pl.BlockSpec((seq, HEAD), lambda b, h: (b, first_col + kind * nh + h))

    const2 = lambda b, h: (0, 0)
    const3 = lambda b, h: (0, 0, 0)
    return pl.pallas_call(
        functools.partial(_mlstm_kernel, nchunk=nchunk),
        grid=(batch, nh),
        in_specs=[col(0), col(1), col(2), col(3),
                  pl.BlockSpec((None, None, 4, nchunk, CHUNK), lambda b, h: (b, h, 0, 0, 0)),
                  pl.BlockSpec((None, 4, 1, CHUNK), lambda b, h: (h, 0, 0, 0)),
                  pl.BlockSpec((8, HEAD), lambda b, h: (0, h)),
                  pl.BlockSpec((8, HEAD), lambda b, h: (0, nh + h)),
                  pl.BlockSpec((1, HEAD), lambda b, h: (0, h)),
                  pl.BlockSpec(cum.shape, const3),
                  pl.BlockSpec(causal.shape, const3),
                  pl.BlockSpec(sel.shape, const2)],
        out_specs=pl.BlockSpec((seq, HEAD), lambda b, h: (b, h)),
        out_shape=jax.ShapeDtypeStruct((batch * seq, width), BF16),
        scratch_shapes=[pltpu.VMEM((seq, HEAD), F32), pltpu.VMEM((seq, HEAD), F32),
                        pltpu.VMEM((seq, HEAD), F32), pltpu.VMEM((2, nchunk, CHUNK), F32),
                        pltpu.VMEM((2, HEAD, 2 * HEAD), F32), pltpu.VMEM((2, nchunk, CHUNK), F32)]
                       + [pltpu.VMEM((2, nchunk, 1), F32)] * 5
                       + [pltpu.VMEM((2, 2, CHUNK, CHUNK), BF16), pltpu.VMEM((2, 2, HEAD, 2 * HEAD), F32),
                          pltpu.VMEM((2, 2, CHUNK, 2 * HEAD), F32),
                          pltpu.VMEM((2, seq, CHUNK), F32), pltpu.VMEM((2, seq, HEAD), F32),
                          pltpu.VMEM((2, seq, HEAD), F32), pltpu.VMEM((2, seq, 1), F32)],
        compiler_params=_cparams(2),
    )(proj, proj, proj, proj, g5, gb, cw, cw, gn.reshape(1, -1),
      jnp.asarray(cum, BF16), jnp.asarray(causal), jnp.asarray(sel, BF16))


N_SLAB = 5
N_PAIR_TILES = 17


def _na_bias_tables(rpb):
    w = GRID_W
    qc = np.arange(w)[:, None]
    kc = np.arange(w)[None, :]
    c0 = np.clip(qc - WIN_C // 2, 0, w - WIN_C)
    valid = (kc >= c0) & (kc < c0 + WIN_C)
    cidx = np.clip(kc - qc + WIN_C - 1, 0, 2 * WIN_C - 2)
    tiles = jnp.where(jnp.asarray(valid)[None, None], rpb.astype(F32)[:, :, cidx] * LOG2_E, NEG_BIG)
    neg = jnp.full_like(tiles[:, 0], NEG_BIG)
    both = jnp.concatenate([tiles[:, :-1], tiles[:, 1:]], axis=-1)
    inner_first = WIN_R - 1 - WIN_R // 2
    inner_last = inner_first + WIN_R - 1
    left = jnp.concatenate([neg, tiles[:, inner_first]], axis=-1)[:, None]
    right = jnp.concatenate([tiles[:, inner_last], neg], axis=-1)[:, None]
    none = jnp.concatenate([neg, neg], axis=-1)[:, None]
    return jnp.concatenate([both, left, right, none], axis=1)


def _na_kernel(q_ref, kt_ref, v_ref, bias_ref, o_ref, s_even, s_odd, *, n_rows):
    w = GRID_W
    pr = 2 * w
    n_pairs = n_rows // 2
    lane_head = lax.broadcasted_iota(jnp.int32, (pr, HEAD), 1) // NA_DH
    head_masks = [lane_head == hh for hh in range(NA_GROUP)]

    def slab_start(p):
        return jnp.clip(p - 2, 0, n_pairs - N_SLAB)

    def logits(p, dst):
        q = q_ref[pl.ds(pl.multiple_of(p * pr, pr), pr), :]
        zero = jnp.zeros_like(q)
        qm = jnp.concatenate([jnp.where(head_masks[hh], q, zero) for hh in range(NA_GROUP)], axis=0)
        sp0 = slab_start(p)
        kt = kt_ref[:, pl.ds(pl.multiple_of(sp0 * pr, pr), N_SLAB * pr)]
        s = _dot(qm, kt)
        for i in range(N_SLAB):
            e_row = 2 * (sp0 + i)
            tiles = []
            for hh in range(NA_GROUP):
                for qr in range(2):
                    r = 2 * p + qr
                    r0 = jnp.clip(r - WIN_R // 2, 0, n_rows - WIN_R)
                    in_e = (e_row >= r0) & (e_row < r0 + WIN_R)
                    in_o = (e_row + 1 >= r0) & (e_row + 1 < r0 + WIN_R)
                    idx = jnp.where(in_e & in_o, e_row - r + WIN_R - 1,
                                    jnp.where(in_o, N_PAIR_TILES - 3, jnp.where(in_e, N_PAIR_TILES - 2, N_PAIR_TILES - 1)))
                    tiles.append(bias_ref[hh, idx])
            dst[i] = s[:, i * pr:(i + 1) * pr] + jnp.concatenate(tiles, axis=0)

    def attend(p, src):
        sp0 = slab_start(p)
        m = src[0]
        for i in range(1, N_SLAB):
            m = jnp.maximum(m, src[i])
        m = jnp.max(m, axis=1, keepdims=True)
        pexp = [jnp.exp2(src[i] - m) for i in range(N_SLAB)]
        l = pexp[0]
        for i in range(1, N_SLAB):
            l = l + pexp[i]
        pcat = jnp.concatenate([pe.astype(BF16) for pe in pexp] + [jnp.zeros((NA_GROUP * pr, pr), BF16)], axis=1)
        vs = v_ref[pl.ds(pl.multiple_of(sp0 * pr, pr), N_SLAB * pr), :]
        vcat = jnp.concatenate([vs, jnp.zeros((pr, HEAD), BF16)], axis=0)
        o = _dot(pcat, vcat) / jnp.sum(l, axis=1, keepdims=True)
        out = jnp.where(head_masks[0], o[0:pr], 0.0)
        for hh in range(1, NA_GROUP):
            out = out + jnp.where(head_masks[hh], o[hh * pr:(hh + 1) * pr], 0.0)
        o_ref[pl.ds(pl.multiple_of(p * pr, pr), pr), :] = out.astype(o_ref.dtype)

    logits(0, s_even)

    def two_pairs(k, carry):
        p = 2 * k
        logits(p + 1, s_odd)
        attend(p, s_even)
        logits(jnp.minimum(p + 2, n_pairs - 1), s_even)
        attend(p + 1, s_odd)
        return carry

    lax.fori_loop(0, n_pairs // 2, two_pairs, 0)


def _neighbourhood_attention(q, kt, v, bias, *, batch, seq):
    n_rows = seq // GRID_W
    assert n_rows % 4 == 0 and n_rows >= 2 * N_SLAB and n_rows >= WIN_R
    d_model = q.shape[1]
    n_groups = d_model // HEAD
    return pl.pallas_call(
        functools.partial(_na_kernel, n_rows=n_rows),
        grid=(batch, n_groups),
        in_specs=[pl.BlockSpec((seq, HEAD), lambda b, g: (b, g)),
                  pl.BlockSpec((HEAD, seq), lambda b, g: (g, b)),
                  pl.BlockSpec((seq, HEAD), lambda b, g: (b, g)),
                  pl.BlockSpec((NA_GROUP, N_PAIR_TILES, GRID_W, 2 * GRID_W), lambda b, g: (g, 0, 0, 0))],
        out_specs=pl.BlockSpec((seq, HEAD), lambda b, g: (b, g)),
        out_shape=jax.ShapeDtypeStruct((batch * seq, d_model), BF16),
        scratch_shapes=[pltpu.VMEM((N_SLAB, NA_GROUP * 2 * GRID_W, 2 * GRID_W), F32)] * 2,
        compiler_params=_cparams(2),
    )(q, kt, v, bias)


def _ffn_kernel(*refs, n_mix, alpha):
    x_ref = refs[0]
    mix_refs = refs[1:1 + 2 * n_mix]
    lmg_ref, lmb_ref, lfg_ref, lfb_ref, wg_ref, wu_ref, wd_ref, o_ref = refs[1 + 2 * n_mix:]
    mix = _dot(mix_refs[0][...], mix_refs[1][...])
    for i in range(1, n_mix):
        mix = mix + _dot(mix_refs[2 * i][...], mix_refs[2 * i + 1][...])
    h = _layer_norm(alpha * x_ref[...] + mix, lmg_ref[...], lmb_ref[...])
    hb = h.astype(BF16)
    g = _dot(hb, wg_ref[...])
    u = _dot(hb, wu_ref[...])
    y = _dot((g * _sigmoid(g) * u).astype(BF16), wd_ref[...])
    o_ref[...] = _layer_norm(alpha * h + y, lfg_ref[...], lfb_ref[...])


def _mixer_out_ffn(x, mix_pairs, ln_mix_g, ln_mix_b, ln_ffn_g, ln_ffn_b, wg, wu, wd, *, alpha, tm):
    t, d = x.shape
    assert t % tm == 0
    row = lambda i: (i, 0)
    const = lambda i: (0, 0)
    resident = functools.partial(pl.BlockSpec, index_map=const, pipeline_mode=pl.Buffered(1))
    in_specs = [pl.BlockSpec((tm, d), row)]
    args = [x]
    for o, w in mix_pairs:
        in_specs += [pl.BlockSpec((tm, o.shape[1]), row), resident(w.shape)]
        args += [o, w]
    in_specs += [resident((1, d))] * 4
    args += [ln_mix_g.reshape(1, d), ln_mix_b.reshape(1, d), ln_ffn_g.reshape(1, d), ln_ffn_b.reshape(1, d)]
    in_specs += [resident(wg.shape), resident(wu.shape), resident(wd.shape)]
    args += [wg, wu, wd]
    return pl.pallas_call(
        functools.partial(_ffn_kernel, n_mix=len(mix_pairs), alpha=alpha),
        grid=(t // tm,),
        in_specs=in_specs,
        out_specs=pl.BlockSpec((tm, d), row),
        out_shape=jax.ShapeDtypeStruct((t, d), F32),
        compiler_params=_cparams(1),
    )(*args)


def _row_tile(t):
    for tm in (1024, 512, 256, 128):
        if t % tm == 0:
            return tm
    raise ValueError(f"token count {t} is not a multiple of 128")


def kernel(x, w_in_even, gate_bias_even, lb_raw, conv_qk, gn_hgrn, gn_mlstm, w_out_even, w_qkv_odd, rpb_odd,
           w_out_odd, ln_mix_g, ln_mix_b, ln_ffn_g, ln_ffn_b, w_ffn_gate, w_ffn_up, w_ffn_down):
    batch, seq, d_model = x.shape
    depth = ln_mix_g.shape[0]
    alpha = (2.0 * depth) ** 0.25
    t = batch * seq
    tm = _row_tile(t)
    a_width = N_REC_HEADS * HEAD
    main_cols = 9 * a_width
    n_gate = 4 * N_REC_HEADS
    tm_ffn = min(tm, 512)
    tn_main = 1536 if main_cols % 1536 == 0 else 512

    h = x.reshape(t, d_model)
    for layer in range(depth):
        j = layer // 2
        if layer % 2 == 0:
            w_in = w_in_even[j]
            w_main = w_in[:, :main_cols].astype(BF16)
            w_gate = jnp.pad(w_in[:, main_cols:], ((0, 0), (0, V7X_LANES - n_gate))).astype(BF16)
            proj = _matmul(h, w_main, out_dtype=BF16, tm=tm, tn=tn_main)
            gates = _matmul(h, w_gate, out_dtype=F32, tm=tm, tn=V7X_LANES)[:, :n_gate]
            o_a = _hgrn(proj, lb_raw, gn_hgrn[j], batch=batch, seq=seq, layer_j=j)
            h_b = _mlstm(proj, gates, gate_bias_even[j], conv_qk[j], gn_mlstm[j], batch=batch, seq=seq)
            w_out = w_out_even[j].astype(BF16)
            mix_pairs = [(o_a, w_out[:a_width]), (h_b, w_out[a_width:])]
        else:
            q, kt, v = _qkv_projection(h, w_qkv_odd[j].astype(BF16), q_scale=NA_DH ** -0.5 * LOG2_E, tm=tm)
            bias = _na_bias_tables(rpb_odd[j])
            o = _neighbourhood_attention(q, kt, v, bias, batch=batch, seq=seq)
            mix_pairs = [(o, w_out_odd[j].astype(BF16))]
        h = _mixer_out_ffn(h, mix_pairs, ln_mix_g[layer], ln_mix_b[layer], ln_ffn_g[layer], ln_ffn_b[layer],
                           w_ffn_gate[layer].astype(BF16), w_ffn_up[layer].astype(BF16),
                           w_ffn_down[layer].astype(BF16), alpha=alpha, tm=tm_ffn)
    return h.reshape(batch, seq, d_model)
```

```python
import functools

import numpy as np
import jax
import jax.numpy as jnp
from jax import lax
from jax.experimental import pallas as pl
from jax.experimental.pallas import tpu as pltpu

F32 = jnp.float32
BF16 = jnp.bfloat16

GRID_W = 64
HEAD = 128
N_REC_HEADS = 4
CHUNK = 64
CONV_W = 5
GATE_GROUP = 8
NA_DH = 32
NA_GROUP = HEAD // NA_DH
WIN_R = 8
WIN_C = 16
LN_EPS = 1e-5
GN_EPS = 1e-6
NEG_BIG = -1e30
LB_FLOOR = 1e-30
LOG2_E = 1.4426950408889634

V7X_LANES = 128
V7X_VMEM_LIMIT_BYTES = 56 * 1024 * 1024

_LEVELS = (32, 16, 8, 4, 2, 1)


def _cparams(n_grid_axes):
    return pltpu.CompilerParams(
        dimension_semantics=("arbitrary",) * n_grid_axes,
        vmem_limit_bytes=V7X_VMEM_LIMIT_BYTES)


def _dot(a, b):
    return jnp.dot(a, b, preferred_element_type=F32)


def _dot_nt(a, b):
    return lax.dot_general(a, b, (((1,), (1,)), ((), ())), preferred_element_type=F32)


def _dot_tn(a, b):
    return lax.dot_general(a, b, (((0,), (0,)), ((), ())), preferred_element_type=F32)


def _split3(x):
    hi = x.astype(BF16)
    r1 = x - hi.astype(F32)
    mid = r1.astype(BF16)
    lo = (r1 - mid.astype(F32)).astype(BF16)
    return hi, mid, lo


def _log_sigmoid(z):
    return jnp.minimum(z, 0.0) - jnp.log1p(jnp.exp(-jnp.abs(z)))


def _sigmoid(z):
    return 1.0 / (1.0 + jnp.exp(-z))


def _layer_norm(t, g, b):
    mu = jnp.mean(t, axis=-1, keepdims=True)
    c = t - mu
    var = jnp.mean(c * c, axis=-1, keepdims=True)
    return c * lax.rsqrt(var + LN_EPS) * g + b


def _mm_kernel(x_ref, w_ref, o_ref):
    o_ref[...] = _dot(x_ref[...].astype(BF16), w_ref[...]).astype(o_ref.dtype)


def _matmul(x, w, *, out_dtype, tm, tn):
    t, k = x.shape
    n = w.shape[1]
    assert t % tm == 0 and n % tn == 0
    return pl.pallas_call(
        _mm_kernel,
        grid=(t // tm, n // tn),
        in_specs=[pl.BlockSpec((tm, k), lambda i, j: (i, 0)),
                  pl.BlockSpec((k, tn), lambda i, j: (0, j))],
        out_specs=pl.BlockSpec((tm, tn), lambda i, j: (i, j)),
        out_shape=jax.ShapeDtypeStruct((t, n), out_dtype),
        compiler_params=_cparams(2),
    )(x, w)


def _qkv_kernel(x_ref, w_ref, o_ref, *, q_scale):
    acc = _dot(x_ref[...].astype(BF16), w_ref[...])
    o_ref[...] = (acc * jnp.where(pl.program_id(1) == 0, q_scale, 1.0)).astype(o_ref.dtype)


def _qkv_projection(x, w_qkv, *, q_scale, tm):
    t, k = x.shape
    d = w_qkv.shape[1] // 3
    assert t % tm == 0
    return pl.pallas_call(
        functools.partial(_qkv_kernel, q_scale=q_scale),
        grid=(t // tm, 3),
        in_specs=[pl.BlockSpec((tm, k), lambda i, j: (i, 0)),
                  pl.BlockSpec((k, d), lambda i, j: (0, j))],
        out_specs=pl.BlockSpec((tm, d), lambda i, j: (i, j)),
        out_shape=jax.ShapeDtypeStruct((t, 3 * d), BF16),
        compiler_params=_cparams(2),
    )(x, w_qkv)


def _hgrn_constants():
    L = CHUNK
    t = np.arange(L)
    a_rows, rowsel, masks = [], [], []
    for c in _LEVELS:
        odd = (t // c) % 2 == 1
        rho = (t // (2 * c)) * 2 * c + c - 1
        u = t[None, :]
        a = np.where(odd[:, None], (u > rho[:, None]) & (u <= t[:, None]),
                     (u > t[:, None]) & (u <= rho[:, None]))
        a_rows.append(a.astype(np.float32))
        rowsel.append(np.broadcast_to(odd[:, None], (L, HEAD)).astype(np.float32))
        same = (t[:, None] // (2 * c)) == (t[None, :] // (2 * c))
        masks.append((odd[:, None] & ~odd[None, :] & same).astype(np.float32))
    masks.append(np.eye(L, dtype=np.float32))
    a_rows.append((t[None, :] <= t[:, None]).astype(np.float32))
    a_rows.append((t[None, :] > t[:, None]).astype(np.float32))
    a_f = np.stack(a_rows)
    rs_f = np.stack(rowsel)
    m_f = np.stack(masks)
    a = np.stack([a_f, a_f[:, ::-1, ::-1]]).reshape(2, 8 * L, L)
    a3 = np.concatenate([a, a, a, np.zeros_like(a)], axis=-1)
    rs = np.stack([rs_f, rs_f[:, ::-1]])
    m = np.stack([m_f, m_f[:, ::-1, ::-1]])
    return a3, rs, m


def _hgrn_kernel(q_ref, ff_ref, fb_ref, v_ref, g_ref, lb_ref, gn_ref, a3_ref, rs_ref, mk_ref,
                 o_ref, acc_s, st_s, y_s, sc_s, dec_s, fw_s, inc_s, dec3_s, *, layer_j, nchunk):
    L = CHUNK
    nlev = len(_LEVELS)
    f_refs = (ff_ref, fb_ref)
    Y_QPRE, Y_KSUF, Y_Q, Y_K = nlev, nlev + 1, nlev + 2, nlev + 3

    lbr = lb_ref[...]
    e = jnp.exp(lbr - jnp.max(lbr, axis=1, keepdims=True))
    soft = e / jnp.sum(e, axis=1, keepdims=True)
    cum = soft[:, 0:1, :]
    for i in range(1, layer_j + 1):
        cum = cum + soft[:, i:i + 1, :]
    lb = cum - soft[:, 0:1, :]
    lb_floor = jnp.maximum(lb, LB_FLOOR)
    one_m_lb = 1.0 - lb

    st_s[...] = jnp.zeros_like(st_s)

    def chunk_rows(d, step):
        step = jnp.minimum(step, nchunk - 1)
        c = step if d == 0 else nchunk - 1 - step
        return pl.ds(pl.multiple_of(c * L, L), L)

    def stage1(d, step0):
        qs, ks, parts = [], [], []
        for par in range(2):
            rows = chunk_rows(d, step0 + par)
            qs.append(q_ref[rows, :].astype(F32))
            f = lb_floor[d] + one_m_lb[d] * _sigmoid(f_refs[d][rows, :].astype(F32))
            ks.append(1.0 - f)
            hi, mid, lo = _split3(jnp.log(f) * LOG2_E)
            parts.append(jnp.concatenate([hi, mid, lo, jnp.zeros_like(hi)], axis=0))
        dall = _dot(a3_ref[d], jnp.concatenate(parts, axis=1))
        for slot in range(2):
            q, k = qs[slot], ks[slot]
            eall = jnp.exp2(dall[:, slot * HEAD:(slot + 1) * HEAD])
            for li, c in enumerate(_LEVELS):
                if c % 8 == 0:
                    first_q = 1 if d == 0 else 0
                    x = jnp.concatenate([(q if b % 2 == first_q else k)[b * c:(b + 1) * c] for b in range(L // c)],
                                        axis=0)
                else:
                    x = jnp.where(rs_ref[d, li] > 0.5, q, k)
                y_s[slot, d, li] = (x * eall[li * L:(li + 1) * L]).astype(BF16)
            e_pre = eall[nlev * L:(nlev + 1) * L]
            e_suf = eall[(nlev + 1) * L:(nlev + 2) * L]
            y_s[slot, d, Y_QPRE] = (q * e_pre).astype(BF16)
            y_s[slot, d, Y_KSUF] = (k * e_suf).astype(BF16)
            y_s[slot, d, Y_Q] = q.astype(BF16)
            y_s[slot, d, Y_K] = k.astype(BF16)
            last = L - 1 if d == 0 else 0
            dec_s[slot, d] = jnp.broadcast_to(e_pre[last:last + 1, :], (8, HEAD))

    def stage2(d, step, slot):
        scores = _dot_nt(y_s[slot, d, Y_Q], y_s[slot, d, Y_K]) * mk_ref[d, nlev]
        for li in range(nlev):
            y = y_s[slot, d, li]
            scores = scores + _dot_nt(y, y) * mk_ref[d, li]
        sc_s[slot, d] = scores.astype(BF16)
        fw_s[slot, d] = y_s[slot, d, Y_QPRE]
        inc_s[slot, d] = _dot_tn(v_ref[chunk_rows(d, step), :].astype(BF16), y_s[slot, d, Y_KSUF])
        dec3_s[slot, d] = dec_s[slot, d]

    def stage3(d, step, slot):
        rows = chunk_rows(d, step)
        st = st_s[d]
        o = _dot(sc_s[slot, d], v_ref[rows, :].astype(BF16)) + _dot_nt(fw_s[slot, d], st.astype(BF16))
        st_s[d] = st * dec3_s[slot, d][0:1, :] + inc_s[slot, d]
        acc_s[rows, :] = acc_s[rows, :] + o

    def body(it, carry):
        for par in range(2):
            for d in range(2):
                stage3(d, 2 * it + par, par)
        for par in range(2):
            for d in range(2):
                stage2(d, 2 * it + 2 + par, par)
        for d in range(2):
            stage1(d, 2 * it + 4)
        return carry

    acc_s[...] = jnp.zeros_like(acc_s)
    for d in range(2):
        stage1(d, 0)
    for d in range(2):
        for par in range(2):
            stage2(d, par, par)
    for d in range(2):
        stage1(d, 2)
    lax.fori_loop(0, nchunk // 2, body, 0)

    gn = gn_ref[...]
    blk = 8 * L

    def fin(i, carry):
        rows = pl.ds(pl.multiple_of(i * blk, blk), blk)
        o = acc_s[rows, :]
        g = g_ref[rows, :].astype(F32)
        o = o * lax.rsqrt(jnp.mean(o * o, axis=-1, keepdims=True) + GN_EPS)
        o_ref[rows, :] = (o * gn * (g * _sigmoid(g))).astype(o_ref.dtype)
        return carry

    lax.fori_loop(0, (nchunk * L) // blk, fin, 0)


def _hgrn(proj, lb_raw, gn, *, batch, seq, layer_j):
    nchunk = seq // CHUNK
    assert nchunk % 2 == 0 and seq % (8 * CHUNK) == 0
    a3, rs, mk = _hgrn_constants()
    n_even = lb_raw.shape[1]
    nh = N_REC_HEADS

    def col(kind):
        return pl.BlockSpec((seq, HEAD), lambda b, h: (b, kind * nh + h))

    const3 = lambda b, h: (0, 0, 0)
    const4 = lambda b, h: (0, 0, 0, 0)
    return pl.pallas_call(
        functools.partial(_hgrn_kernel, layer_j=layer_j, nchunk=nchunk),
        grid=(batch, nh),
        in_specs=[col(0), col(1), col(2), col(3), col(4),
                  pl.BlockSpec((2, n_even, HEAD), lambda b, h: (0, 0, h)),
                  pl.BlockSpec((1, HEAD), lambda b, h: (0, h)),
                  pl.BlockSpec(a3.shape, const3),
                  pl.BlockSpec(rs.shape, const4),
                  pl.BlockSpec(mk.shape, const4)],
        out_specs=pl.BlockSpec((seq, HEAD), lambda b, h: (b, h)),
        out_shape=jax.ShapeDtypeStruct((batch * seq, nh * HEAD), BF16),
        scratch_shapes=[pltpu.VMEM((seq, HEAD), F32), pltpu.VMEM((2, HEAD, HEAD), F32),
                        pltpu.VMEM((2, 2, len(_LEVELS) + 4, CHUNK, HEAD), BF16),
                        pltpu.VMEM((2, 2, CHUNK, CHUNK), BF16),
                        pltpu.VMEM((2, 2, 8, HEAD), F32),
                        pltpu.VMEM((2, 2, CHUNK, HEAD), BF16),
                        pltpu.VMEM((2, 2, HEAD, HEAD), F32),
                        pltpu.VMEM((2, 2, 8, HEAD), F32)],
        compiler_params=_cparams(2),
    )(proj, proj, proj, proj, proj, lb_raw, gn.reshape(1, -1),
      jnp.asarray(a3, BF16), jnp.asarray(rs), jnp.asarray(mk))


def _mlstm_constants():
    L = CHUNK
    t = np.arange(L)
    ut = (t[:, None] <= t[None, :]).astype(np.float32)
    cum = np.stack([ut, ut[::-1, ::-1]])
    tril = (t[None, :] <= t[:, None]).astype(np.float32)
    causal = np.stack([tril, tril[::-1, ::-1]])
    sel = np.zeros((HEAD, 4 * GATE_GROUP * HEAD), np.float32)
    for q in range(4):
        for p in range(3):
            for j in range(GATE_GROUP):
                sel[(q * 3 + p) * GATE_GROUP + j, (q * GATE_GROUP + j) * HEAD:(q * GATE_GROUP + j + 1) * HEAD] = 1.0
    return cum, causal, sel


def _mlstm_kernel(xq_ref, xk_ref, v_ref, og_ref, gates_ref, gb_ref, cwq_ref, cwk_ref, gn_ref,
                  cum_ref, cm_ref, sel_ref, o_ref, q_s, k_s, acc_s, b_s, c_s, cma_s,
                  bl_s, gm_s, mst_s, mnew_s, wold_s, qkw_s, inc_s, nd_s, wi_s, winter_s, wsc_s, floor_s,
                  *, nchunk):
    L = CHUNK
    pad = CONV_W // 2
    halo = 16

    def conv_chunk(c, carry):
        rows = pl.ds(pl.multiple_of(c * L, L), L)
        prev = pl.ds(pl.multiple_of(jnp.maximum(c * L - halo, 0), halo), halo)
        nxt = pl.ds(pl.multiple_of(jnp.minimum(c * L + L, nchunk * L - halo), halo), halo)
        has_prev = jnp.where(c > 0, 1.0, 0.0).astype(F32)
        has_next = jnp.where(c < nchunk - 1, 1.0, 0.0).astype(F32)
        for x_ref, w_ref, dst, scale in ((xq_ref, cwq_ref, q_s, 1.0), (xk_ref, cwk_ref, k_s, HEAD ** -0.5)):
            win = jnp.concatenate([x_ref[prev, :].astype(F32) * has_prev, x_ref[rows, :].astype(F32),
                                   x_ref[nxt, :].astype(F32) * has_next], axis=0)
            w = w_ref[...]
            acc = win[halo - pad:halo - pad + L] * w[0:1, :]
            for j in range(1, CONV_W):
                acc = acc + win[halo - pad + j:halo - pad + j + L] * w[j:j + 1, :]
            y = acc * _sigmoid(acc)
            dst[rows, :] = y * scale if scale != 1.0 else y
        return carry

    lax.fori_loop(0, nchunk, conv_chunk, 0)

    for d in range(2):
        lf2 = _log_sigmoid(gates_ref[2 + d] + gb_ref[2 + d])
        hi, mid, lo = _split3(lf2)
        cm = cum_ref[d]
        b2 = _dot(hi, cm) + _dot(mid, cm) + _dot(lo, cm)
        b_s[d] = b2
        last = L - 1 if d == 0 else 0
        bl_s[d] = b2[:, last:last + 1]
        li2 = gates_ref[d] + gb_ref[d]
        gm_s[d] = jnp.max(b2[:, last:last + 1] - b2 + li2, axis=1, keepdims=True)
        x = jnp.concatenate([li2 - b2, jnp.full((nchunk, HEAD - L), NEG_BIG, F32)], axis=1)
        lane_x = lax.broadcasted_iota(jnp.int32, x.shape, 1)
        sh = 1
        while sh < L:
            if d == 0:
                x = jnp.maximum(x, jnp.where(lane_x >= sh, pltpu.roll(x, sh, axis=1), NEG_BIG))
            else:
                x = jnp.maximum(x, pltpu.roll(x, HEAD - sh, axis=1))
            sh *= 2
        cma_s[d] = x[:, :L]

    c_s[...] = jnp.zeros_like(c_s)
    lane = lax.broadcasted_iota(jnp.int32, (L, HEAD), 1)
    ones_col = (lane == 0).astype(BF16)

    def stab_step(step, m):
        new = []
        for d in range(2):
            c = step if d == 0 else nchunk - 1 - step
            sl = pl.ds(c, 1)
            mst_s[d, sl, :] = m[d]
            m_new = jnp.maximum(bl_s[d, sl, :] + m[d], gm_s[d, sl, :])
            mnew_s[d, sl, :] = m_new
            new.append(m_new)
        return tuple(new)

    lax.fori_loop(0, nchunk, stab_step, (jnp.zeros((1, 1), F32), jnp.zeros((1, 1), F32)))

    G = GATE_GROUP
    causal = [cm_ref[d] > 0.5 for d in range(2)]

    def gate_weights(g, carry):
        sl = pl.ds(pl.multiple_of(g * G, G), G)
        for d in range(2):
            br = b_s[d, sl, :]
            li = gates_ref[d, sl, :] + gb_ref[d]
            a = li - br
            m_st = mst_s[d, sl, :]
            m_new = mnew_s[d, sl, :]
            last = L - 1 if d == 0 else 0
            b_last = br[:, last:last + 1]
            mx = jnp.maximum(m_st, cma_s[d, sl, :])
            quantities = (mx, jnp.exp(m_st - mx), jnp.exp(-(br + mx)), jnp.exp(b_last - br + li - m_new))
            parts = [p.astype(F32) for qty in quantities for p in _split3(qty)]
            parts.append(jnp.zeros((HEAD - len(parts) * G, L), F32))
            cols = _dot_tn(jnp.concatenate(parts, axis=0).astype(BF16), sel_ref[...])
            for j in range(G):
                rows = pl.ds(pl.multiple_of((g * G + j) * L, L), L)
                mx_col, winter, floor, wsc = (cols[:, (q * G + j) * HEAD:(q * G + j + 1) * HEAD] for q in range(4))
                am = jnp.where(causal[d], jnp.broadcast_to(a[j:j + 1], (L, L)), NEG_BIG)
                wi_s[d, rows, :] = jnp.exp(am - mx_col[:, :L])
                winter_s[d, rows, :] = winter
                floor_s[d, rows, :] = floor[:, 0:1]
                wsc_s[d, rows, :] = wsc
            wold_s[d, sl, :] = jnp.exp(b_last + m_st - m_new)
        return carry

    lax.fori_loop(0, nchunk // G, gate_weights, 0)

    def chunk_index(d, step):
        step = jnp.minimum(step, nchunk - 1)
        return step if d == 0 else nchunk - 1 - step

    def chunk_rows(d, step):
        return pl.ds(pl.multiple_of(chunk_index(d, step) * L, L), L)

    def value_aug(rows):
        return jnp.concatenate([v_ref[rows, :], ones_col], axis=1)

    def stage_a(d, step, slot):
        rows = chunk_rows(d, step)
        k = k_s[rows, :]
        qkw_s[slot, d] = (_dot_nt(q_s[rows, :].astype(BF16), k.astype(BF16)) * wi_s[d, rows, :]).astype(BF16)
        inc_s[slot, d] = _dot_tn((k * wsc_s[d, rows, :]).astype(BF16), value_aug(rows))

    def stage_b(d, step, slot):
        rows = chunk_rows(d, step)
        cst = c_s[d]
        w_inter = winter_s[d, rows, :]
        nd_s[slot, d] = (_dot(qkw_s[slot, d], value_aug(rows))
                         + jnp.concatenate([w_inter, w_inter], axis=1)
                         * _dot(q_s[rows, :].astype(BF16), cst.astype(BF16)))
        c_s[d] = wold_s[d, pl.ds(chunk_index(d, step), 1), :] * cst + inc_s[slot, d]

    def stage_c(d, step, slot):
        rows = chunk_rows(d, step)
        nd = nd_s[slot, d]
        h = nd[:, :HEAD] / jnp.maximum(jnp.abs(nd[:, HEAD:HEAD + 1]), floor_s[d, rows, :])
        acc_s[rows, :] = acc_s[rows, :] + h

    def body(it, with_c):
        for par in range(2):
            for d in range(2):
                if with_c:
                    stage_c(d, 2 * it - 2 + par, par)
        for par in range(2):
            for d in range(2):
                stage_b(d, 2 * it + par, par)
        for par in range(2):
            for d in range(2):
                stage_a(d, 2 * it + 2 + par, par)

    def loop_body(it, carry):
        body(it, True)
        return carry

    acc_s[...] = jnp.zeros_like(acc_s)
    for d in range(2):
        for par in range(2):
            stage_a(d, par, par)
    body(0, False)
    lax.fori_loop(1, nchunk // 2, loop_body, 0)
    for d in range(2):
        for par in range(2):
            stage_c(d, nchunk - 2 + par, par)

    gn = gn_ref[...]
    blk = 8 * L

    def fin(i, carry):
        rows = pl.ds(pl.multiple_of(i * blk, blk), blk)
        h = acc_s[rows, :]
        mu = jnp.mean(h, axis=-1, keepdims=True)
        cen = h - mu
        hn = cen * lax.rsqrt(jnp.mean(cen * cen, axis=-1, keepdims=True) + GN_EPS)
        o_ref[rows, :] = (hn * gn * _sigmoid(og_ref[rows, :].astype(F32))).astype(o_ref.dtype)
        return carry

    lax.fori_loop(0, (nchunk * L) // blk, fin, 0)


def _mlstm(proj, gates, gate_bias, conv_w, gn, *, batch, seq):
    nchunk = seq // CHUNK
    assert nchunk % GATE_GROUP == 0 and nchunk % 2 == 0 and seq % (8 * CHUNK) == 0
    nh = N_REC_HEADS
    width = nh * HEAD
    cum, causal, sel = _mlstm_constants()
    g5 = gates.reshape(batch, nchunk, CHUNK, 4, nh).transpose(0, 4, 3, 1, 2)
    gb = jnp.broadcast_to(gate_bias.reshape(4, nh).T[:, :, None, None], (nh, 4, 1, CHUNK))
    cw = jnp.pad(conv_w, ((0, 8 - CONV_W), (0, 0)))
    first_col = 5 * nh

    def col(kind):
        return pl.BlockSpec((seq, HEAD), lambda b, h: (b, first_col + kind * nh + h))

    const2 = lambda b, h: (0, 0)
    const3 = lambda b, h: (0, 0, 0)
    return pl.pallas_call(
        functools.partial(_mlstm_kernel, nchunk=nchunk),
        grid=(batch, nh),
        in_specs=[col(0), col(1), col(2), col(3),
                  pl.BlockSpec((None, None, 4, nchunk, CHUNK), lambda b, h: (b, h, 0, 0, 0)),
                  pl.BlockSpec((None, 4, 1, CHUNK), lambda b, h: (h, 0, 0, 0)),
                  pl.BlockSpec((8, HEAD), lambda b, h: (0, h)),
                  pl.BlockSpec((8, HEAD), lambda b, h: (0, nh + h)),
                  pl.BlockSpec((1, HEAD), lambda b, h: (0, h)),
                  pl.BlockSpec(cum.shape, const3),
                  pl.BlockSpec(causal.shape, const3),
                  pl.BlockSpec(sel.shape, const2)],
        out_specs=pl.BlockSpec((seq, HEAD), lambda b, h: (b, h)),
        out_shape=jax.ShapeDtypeStruct((batch * seq, width), BF16),
        scratch_shapes=[pltpu.VMEM((seq, HEAD), F32), pltpu.VMEM((seq, HEAD), F32),
                        pltpu.VMEM((seq, HEAD), F32), pltpu.VMEM((2, nchunk, CHUNK), F32),
                        pltpu.VMEM((2, HEAD, 2 * HEAD), F32), pltpu.VMEM((2, nchunk, CHUNK), F32)]
                       + [pltpu.VMEM((2, nchunk, 1), F32)] * 5
                       + [pltpu.VMEM((2, 2, CHUNK, CHUNK), BF16), pltpu.VMEM((2, 2, HEAD, 2 * HEAD), F32),
                          pltpu.VMEM((2, 2, CHUNK, 2 * HEAD), F32),
                          pltpu.VMEM((2, seq, CHUNK), F32), pltpu.VMEM((2, seq, HEAD), F32),
                          pltpu.VMEM((2, seq, HEAD), F32), pltpu.VMEM((2, seq, 1), F32)],
        compiler_params=_cparams(2),
    )(proj, proj, proj, proj, g5, gb, cw, cw, gn.reshape(1, -1),
      jnp.asarray(cum, BF16), jnp.asarray(causal), jnp.asarray(sel, BF16))


N_SLAB = 5
N_FULL_TILES = WIN_R - 1
TILE_LEFT, TILE_RIGHT, TILE_NONE = N_FULL_TILES, N_FULL_TILES + 1, N_FULL_TILES + 2
N_BIAS_TILES = N_FULL_TILES + 3


def _na_bias_tables(rpb):
    w = GRID_W
    qc = np.arange(w)[:, None]
    kc = np.arange(w)[None, :]
    c0 = np.clip(qc - WIN_C // 2, 0, w - WIN_C)
    valid = (kc >= c0) & (kc < c0 + WIN_C)
    cidx = np.clip(kc - qc + WIN_C - 1, 0, 2 * WIN_C - 2)
    tiles = jnp.where(jnp.asarray(valid)[None, None], rpb.astype(F32)[:, :, cidx] * LOG2_E, NEG_BIG)
    tiles = tiles.transpose(0, 1, 3, 2)
    neg = jnp.full_like(tiles[:, 0], NEG_BIG)

    def quad(a00, a10, a01, a11):
        pick = lambda a: neg if a is None else tiles[:, a]
        return jnp.concatenate([jnp.concatenate([pick(a00), pick(a10)], axis=-2),
                                jnp.concatenate([pick(a01), pick(a11)], axis=-2)], axis=-1)

    first = WIN_R - 1 - WIN_R // 2
    blocks = [quad(a, a + 1, a - 1, a) for a in range(1, 2 * N_FULL_TILES, 2)]
    blocks.append(quad(first, first + 1, None, first))
    blocks.append(quad(None, None, first + WIN_R - 1, None))
    blocks.append(quad(None, None, None, None))
    return jnp.stack(blocks, axis=1)


def _na_kernel(q_ref, k_ref, v_ref, bias_ref, o_ref, vt_s, s_s, p_s, il_s, *, n_rows):
    w = GRID_W
    pr = 2 * w
    n_pairs = n_rows // 2
    ks = N_SLAB * pr

    def to_vt(blk, carry):
        cols = pl.ds(pl.multiple_of(blk * pr, pr), pr)
        vt_s[:, cols] = v_ref[cols, :].astype(F32).T.astype(BF16)
        return carry

    lax.fori_loop(0, n_pairs, to_vt, 0, unroll=4)

    def clamp(p):
        return jnp.minimum(p, n_pairs - 1)

    def slab_start(p):
        return jnp.clip(p - 2, 0, n_pairs - N_SLAB)

    def logits(p, slot):
        p = clamp(p)
        qt = q_ref[pl.ds(pl.multiple_of(p * pr, pr), pr), :].astype(F32).T.astype(BF16)
        zero = jnp.zeros((NA_DH, pr), BF16)
        qm = jnp.concatenate(
            [jnp.concatenate([qt[hh * NA_DH:(hh + 1) * NA_DH] if r == hh else zero for r in range(NA_GROUP)], axis=0)
             for hh in range(NA_GROUP)], axis=1)
        sp0 = slab_start(p)
        st = _dot(k_ref[pl.ds(pl.multiple_of(sp0 * pr, pr), ks), :], qm)
        r0 = [jnp.clip(2 * p + qr - WIN_R // 2, 0, n_rows - WIN_R) for qr in range(2)]
        for i in range(N_SLAB):
            e_row = 2 * (sp0 + i)
            inside = [[(e_row + ko >= r0[qr]) & (e_row + ko < r0[qr] + WIN_R) for qr in range(2)] for ko in range(2)]
            all_in = inside[0][0] & inside[1][0] & inside[0][1] & inside[1][1]
            any_in = inside[0][0] | inside[1][0] | inside[0][1] | inside[1][1]
            full_idx = lax.shift_right_logical(e_row - 2 * p + WIN_R - 2, 1)
            idx = jnp.where(all_in, full_idx,
                            jnp.where(any_in, jnp.where(inside[0][0], TILE_LEFT, TILE_RIGHT), TILE_NONE))
            bias = jnp.concatenate([bias_ref[hh, idx] for hh in range(NA_GROUP)], axis=1)
            s_s[slot, i * pr:(i + 1) * pr, :] = st[i * pr:(i + 1) * pr, :] + bias

    def weights(slot):
        s = s_s[slot]
        pexp = jnp.exp2(s - jnp.max(s, axis=0, keepdims=True))
        p_s[slot] = pexp.astype(BF16)
        il_s[slot] = jnp.broadcast_to(1.0 / jnp.sum(pexp, axis=0, keepdims=True), (8, NA_GROUP * pr))

    def attend(p, slot):
        sp0 = slab_start(p)
        ot = _dot(vt_s[:, pl.ds(pl.multiple_of(sp0 * pr, pr), ks)], p_s[slot])
        inv_l = il_s[slot][0:1, :]
        out_t = jnp.concatenate(
            [ot[hh * NA_DH:(hh + 1) * NA_DH, hh * pr:(hh + 1) * pr] * inv_l[:, hh * pr:(hh + 1) * pr]
             for hh in range(NA_GROUP)], axis=0)
        o_ref[pl.ds(pl.multiple_of(p * pr, pr), pr), :] = out_t.T.astype(o_ref.dtype)

    def two_pairs(k, carry):
        for par in range(2):
            attend(2 * k + par, par)
        for par in range(2):
            weights(par)
        for par in range(2):
            logits(2 * k + 4 + par, par)
        return carry

    for par in range(2):
        logits(par, par)
    for par in range(2):
        weights(par)
    for par in range(2):
        logits(2 + par, par)
    lax.fori_loop(0, n_pairs // 2, two_pairs, 0)


def _neighbourhood_attention(qkv, bias, *, batch, seq):
    n_rows = seq // GRID_W
    assert n_rows % 4 == 0 and n_rows >= 2 * N_SLAB and n_rows >= WIN_R
    d_model = qkv.shape[1] // 3
    n_groups = d_model // HEAD
    pr = 2 * GRID_W
    return pl.pallas_call(
        functools.partial(_na_kernel, n_rows=n_rows),
        grid=(batch, n_groups),
        in_specs=[pl.BlockSpec((seq, HEAD), lambda b, g: (b, g)),
                  pl.BlockSpec((seq, HEAD), lambda b, g: (b, n_groups + g)),
                  pl.BlockSpec((seq, HEAD), lambda b, g: (b, 2 * n_groups + g)),
                  pl.BlockSpec((NA_GROUP, N_BIAS_TILES, pr, pr), lambda b, g: (g, 0, 0, 0))],
        out_specs=pl.BlockSpec((seq, HEAD), lambda b, g: (b, g)),
        out_shape=jax.ShapeDtypeStruct((batch * seq, d_model), BF16),
        scratch_shapes=[pltpu.VMEM((HEAD, seq), BF16),
                        pltpu.VMEM((2, N_SLAB * pr, NA_GROUP * pr), F32),
                        pltpu.VMEM((2, N_SLAB * pr, NA_GROUP * pr), BF16),
                        pltpu.VMEM((2, 8, NA_GROUP * pr), F32)],
        compiler_params=_cparams(2),
    )(qkv, qkv, qkv, bias)


def _ffn_kernel(*refs, n_mix, alpha):
    x_ref = refs[0]
    mix_refs = refs[1:1 + 2 * n_mix]
    lmg_ref, lmb_ref, lfg_ref, lfb_ref, wg_ref, wu_ref, wd_ref, o_ref = refs[1 + 2 * n_mix:]
    mix = _dot(mix_refs[0][...], mix_refs[1][...])
    for i in range(1, n_mix):
        mix = mix + _dot(mix_refs[2 * i][...], mix_refs[2 * i + 1][...])
    h = _layer_norm(alpha * x_ref[...] + mix, lmg_ref[...], lmb_ref[...])
    hb = h.astype(BF16)
    g = _dot(hb, wg_ref[...])
    u = _dot(hb, wu_ref[...])
    y = _dot((g * _sigmoid(g) * u).astype(BF16), wd_ref[...])
    o_ref[...] = _layer_norm(alpha * h + y, lfg_ref[...], lfb_ref[...])


def _mixer_out_ffn(x, mix_pairs, ln_mix_g, ln_mix_b, ln_ffn_g, ln_ffn_b, wg, wu, wd, *, alpha, tm):
    t, d = x.shape
    assert t % tm == 0
    row = lambda i: (i, 0)
    const = lambda i: (0, 0)
    resident = functools.partial(pl.BlockSpec, index_map=const, pipeline_mode=pl.Buffered(1))
    in_specs = [pl.BlockSpec((tm, d), row)]
    args = [x]
    for o, w in mix_pairs:
        in_specs += [pl.BlockSpec((tm, o.shape[1]), row), resident(w.shape)]
        args += [o, w]
    in_specs += [resident((1, d))] * 4
    args += [ln_mix_g.reshape(1, d), ln_mix_b.reshape(1, d), ln_ffn_g.reshape(1, d), ln_ffn_b.reshape(1, d)]
    in_specs += [resident(wg.shape), resident(wu.shape), resident(wd.shape)]
    args += [wg, wu, wd]
    return pl.pallas_call(
        functools.partial(_ffn_kernel, n_mix=len(mix_pairs), alpha=alpha),
        grid=(t // tm,),
        in_specs=in_specs,
        out_specs=pl.BlockSpec((tm, d), row),
        out_shape=jax.ShapeDtypeStruct((t, d), F32),
        compiler_params=_cparams(1),
    )(*args)


def _row_tile(t):
    for tm in (1024, 512, 256, 128):
        if t % tm == 0:
            return tm
    raise ValueError(f"token count {t} is not a multiple of 128")


def kernel(x, w_in_even, gate_bias_even, lb_raw, conv_qk, gn_hgrn, gn_mlstm, w_out_even, w_qkv_odd, rpb_odd,
           w_out_odd, ln_mix_g, ln_mix_b, ln_ffn_g, ln_ffn_b, w_ffn_gate, w_ffn_up, w_ffn_down):
    batch, seq, d_model = x.shape
    depth = ln_mix_g.shape[0]
    alpha = (2.0 * depth) ** 0.25
    t = batch * seq
    tm = _row_tile(t)
    a_width = N_REC_HEADS * HEAD
    main_cols = 9 * a_width
    n_gate = 4 * N_REC_HEADS
    tm_ffn = min(tm, 512)
    tn_main = 1536 if main_cols % 1536 == 0 else 512

    h = x.reshape(t, d_model)
    for layer in range(depth):
        j = layer // 2
        if layer % 2 == 0:
            w_in = w_in_even[j]
            w_main = w_in[:, :main_cols].astype(BF16)
            w_gate = jnp.pad(w_in[:, main_cols:], ((0, 0), (0, V7X_LANES - n_gate))).astype(BF16)
            proj = _matmul(h, w_main, out_dtype=BF16, tm=tm, tn=tn_main)
            gates = _matmul(h, w_gate, out_dtype=F32, tm=tm, tn=V7X_LANES)[:, :n_gate]
            o_a = _hgrn(proj, lb_raw, gn_hgrn[j], batch=batch, seq=seq, layer_j=j)
            h_b = _mlstm(proj, gates, gate_bias_even[j], conv_qk[j], gn_mlstm[j], batch=batch, seq=seq)
            w_out = w_out_even[j].astype(BF16)
            mix_pairs = [(o_a, w_out[:a_width]), (h_b, w_out[a_width:])]
        else:
            qkv = _qkv_projection(h, w_qkv_odd[j].astype(BF16), q_scale=NA_DH ** -0.5 * LOG2_E, tm=tm)
            o = _neighbourhood_attention(qkv, _na_bias_tables(rpb_odd[j]), batch=batch, seq=seq)
            mix_pairs = [(o, w_out_odd[j].astype(BF16))]
        h = _mixer_out_ffn(h, mix_pairs, ln_mix_g[layer], ln_mix_b[layer], ln_ffn_g[layer], ln_ffn_b[layer],
                           w_ffn_gate[layer].astype(BF16), w_ffn_up[layer].astype(BF16),
                           w_ffn_down[layer].astype(BF16), alpha=alpha, tm=tm_ffn)
    return h.reshape(batch, seq, d_model)
```

```python
import functools

import numpy as np
import jax
import jax.numpy as jnp
from jax import lax
from jax.experimental import pallas as pl
from jax.experimental.pallas import tpu as pltpu

F32 = jnp.float32
BF16 = jnp.bfloat16

GRID_W = 64
HEAD = 128
N_REC_HEADS = 4
CHUNK = 64
CONV_W = 5
GATE_GROUP = 8
NA_DH = 32
NA_GROUP = HEAD // NA_DH
WIN_R = 8
WIN_C = 16
LN_EPS = 1e-5
GN_EPS = 1e-6
NEG_BIG = -1e30
LB_FLOOR = 1e-30
LOG2_E = 1.4426950408889634

V7X_LANES = 128
V7X_VMEM_LIMIT_BYTES = 56 * 1024 * 1024

_LEVELS = (32, 16, 8, 4, 2, 1)


def _cparams(n_grid_axes):
    return pltpu.CompilerParams(
        dimension_semantics=("arbitrary",) * n_grid_axes,
        vmem_limit_bytes=V7X_VMEM_LIMIT_BYTES)


def _dot(a, b):
    return jnp.dot(a, b, preferred_element_type=F32)


def _dot_nt(a, b):
    return lax.dot_general(a, b, (((1,), (1,)), ((), ())), preferred_element_type=F32)


def _dot_tn(a, b):
    return lax.dot_general(a, b, (((0,), (0,)), ((), ())), preferred_element_type=F32)


def _split3(x):
    hi = x.astype(BF16)
    r1 = x - hi.astype(F32)
    mid = r1.astype(BF16)
    lo = (r1 - mid.astype(F32)).astype(BF16)
    return hi, mid, lo


def _log_sigmoid(z):
    return jnp.minimum(z, 0.0) - jnp.log1p(jnp.exp(-jnp.abs(z)))


def _sigmoid(z):
    return 1.0 / (1.0 + jnp.exp(-z))


def _layer_norm(t, g, b):
    mu = jnp.mean(t, axis=-1, keepdims=True)
    c = t - mu
    var = jnp.mean(c * c, axis=-1, keepdims=True)
    return c * lax.rsqrt(var + LN_EPS) * g + b


def _mm_kernel(x_ref, w_ref, o_ref):
    o_ref[...] = _dot(x_ref[...].astype(BF16), w_ref[...]).astype(o_ref.dtype)


def _matmul(x, w, *, out_dtype, tm, tn):
    t, k = x.shape
    n = w.shape[1]
    assert t % tm == 0 and n % tn == 0
    return pl.pallas_call(
        _mm_kernel,
        grid=(t // tm, n // tn),
        in_specs=[pl.BlockSpec((tm, k), lambda i, j: (i, 0)),
                  pl.BlockSpec((k, tn), lambda i, j: (0, j))],
        out_specs=pl.BlockSpec((tm, tn), lambda i, j: (i, j)),
        out_shape=jax.ShapeDtypeStruct((t, n), out_dtype),
        compiler_params=_cparams(2),
    )(x, w)


def _qkv_kernel(x_ref, w_ref, o_ref, *, q_scale):
    acc = _dot(x_ref[...].astype(BF16), w_ref[...])
    o_ref[...] = (acc * jnp.where(pl.program_id(1) == 0, q_scale, 1.0)).astype(o_ref.dtype)


def _qkv_projection(x, w_qkv, *, q_scale, tm):
    t, k = x.shape
    d = w_qkv.shape[1] // 3
    assert t % tm == 0
    return pl.pallas_call(
        functools.partial(_qkv_kernel, q_scale=q_scale),
        grid=(t // tm, 3),
        in_specs=[pl.BlockSpec((tm, k), lambda i, j: (i, 0)),
                  pl.BlockSpec((k, d), lambda i, j: (0, j))],
        out_specs=pl.BlockSpec((tm, d), lambda i, j: (i, j)),
        out_shape=jax.ShapeDtypeStruct((t, 3 * d), BF16),
        compiler_params=_cparams(2),
    )(x, w_qkv)


def _hgrn_constants():
    L = CHUNK
    t = np.arange(L)
    a_rows, rowsel, masks = [], [], []
    for c in _LEVELS:
        odd = (t // c) % 2 == 1
        rho = (t // (2 * c)) * 2 * c + c - 1
        u = t[None, :]
        a = np.where(odd[:, None], (u > rho[:, None]) & (u <= t[:, None]),
                     (u > t[:, None]) & (u <= rho[:, None]))
        a_rows.append(a.astype(np.float32))
        rowsel.append(np.broadcast_to(odd[:, None], (L, HEAD)).astype(np.float32))
        same = (t[:, None] // (2 * c)) == (t[None, :] // (2 * c))
        masks.append((odd[:, None] & ~odd[None, :] & same).astype(np.float32))
    masks.append(np.eye(L, dtype=np.float32))
    a_rows.append((t[None, :] <= t[:, None]).astype(np.float32))
    a_rows.append((t[None, :] > t[:, None]).astype(np.float32))
    a_f = np.stack(a_rows)
    rs_f = np.stack(rowsel)
    m_f = np.stack(masks)
    a = np.stack([a_f, a_f[:, ::-1, ::-1]]).reshape(2, 8 * L, L)
    a3 = np.concatenate([a, a, a, np.zeros_like(a)], axis=-1)
    rs = np.stack([rs_f, rs_f[:, ::-1]])
    m = np.stack([m_f, m_f[:, ::-1, ::-1]])
    return a3, rs, m


def _hgrn_kernel(q_ref, ff_ref, fb_ref, v_ref, g_ref, lb_ref, gn_ref, a3_ref, rs_ref, mk_ref,
                 o_ref, acc_s, st_s, y_s, sc_s, dec_s, fw_s, inc_s, dec3_s, *, layer_j, nchunk):
    L = CHUNK
    nlev = len(_LEVELS)
    f_refs = (ff_ref, fb_ref)
    Y_QPRE, Y_KSUF, Y_Q, Y_K = nlev, nlev + 1, nlev + 2, nlev + 3

    lbr = lb_ref[...]
    e = jnp.exp(lbr - jnp.max(lbr, axis=1, keepdims=True))
    soft = e / jnp.sum(e, axis=1, keepdims=True)
    cum = soft[:, 0:1, :]
    for i in range(1, layer_j + 1):
        cum = cum + soft[:, i:i + 1, :]
    lb = cum - soft[:, 0:1, :]
    lb_floor = jnp.maximum(lb, LB_FLOOR)
    one_m_lb = 1.0 - lb

    st_s[...] = jnp.zeros_like(st_s)

    def chunk_rows(d, step):
        step = jnp.minimum(step, nchunk - 1)
        c = step if d == 0 else nchunk - 1 - step
        return pl.ds(pl.multiple_of(c * L, L), L)

    def stage1(d, step0):
        qs, ks, parts = [], [], []
        for par in range(2):
            rows = chunk_rows(d, step0 + par)
            qs.append(q_ref[rows, :].astype(F32))
            f = lb_floor[d] + one_m_lb[d] * _sigmoid(f_refs[d][rows, :].astype(F32))
            ks.append(1.0 - f)
            hi, mid, lo = _split3(jnp.log(f) * LOG2_E)
            parts.append(jnp.concatenate([hi, mid, lo, jnp.zeros_like(hi)], axis=0))
        dall = _dot(a3_ref[d], jnp.concatenate(parts, axis=1))
        for slot in range(2):
            q, k = qs[slot], ks[slot]
            eall = jnp.exp2(dall[:, slot * HEAD:(slot + 1) * HEAD])
            for li, c in enumerate(_LEVELS):
                if c % 8 == 0:
                    first_q = 1 if d == 0 else 0
                    x = jnp.concatenate([(q if b % 2 == first_q else k)[b * c:(b + 1) * c] for b in range(L // c)],
                                        axis=0)
                else:
                    x = jnp.where(rs_ref[d, li] > 0.5, q, k)
                y_s[slot, d, li] = (x * eall[li * L:(li + 1) * L]).astype(BF16)
            e_pre = eall[nlev * L:(nlev + 1) * L]
            e_suf = eall[(nlev + 1) * L:(nlev + 2) * L]
            y_s[slot, d, Y_QPRE] = (q * e_pre).astype(BF16)
            y_s[slot, d, Y_KSUF] = (k * e_suf).astype(BF16)
            y_s[slot, d, Y_Q] = q.astype(BF16)
            y_s[slot, d, Y_K] = k.astype(BF16)
            last = L - 1 if d == 0 else 0
            dec_s[slot, d] = jnp.broadcast_to(e_pre[last:last + 1, :], (8, HEAD))

    def stage2(d, step, slot):
        scores = _dot_nt(y_s[slot, d, Y_Q], y_s[slot, d, Y_K]) * mk_ref[d, nlev]
        for li in range(nlev):
            y = y_s[slot, d, li]
            scores = scores + _dot_nt(y, y) * mk_ref[d, li]
        sc_s[slot, d] = scores.astype(BF16)
        fw_s[slot, d] = y_s[slot, d, Y_QPRE]
        inc_s[slot, d] = _dot_tn(v_ref[chunk_rows(d, step), :].astype(BF16), y_s[slot, d, Y_KSUF])
        dec3_s[slot, d] = dec_s[slot, d]

    def stage3(d, step, slot):
        rows = chunk_rows(d, step)
        st = st_s[d]
        o = _dot(sc_s[slot, d], v_ref[rows, :].astype(BF16)) + _dot_nt(fw_s[slot, d], st.astype(BF16))
        st_s[d] = st * dec3_s[slot, d][0:1, :] + inc_s[slot, d]
        acc_s[rows, :] = acc_s[rows, :] + o

    def body(it, carry):
        for par in range(2):
            for d in range(2):
                stage3(d, 2 * it + par, par)
        for par in range(2):
            for d in range(2):
                stage2(d, 2 * it + 2 + par, par)
        for d in range(2):
            stage1(d, 2 * it + 4)
        return carry

    acc_s[...] = jnp.zeros_like(acc_s)
    for d in range(2):
        stage1(d, 0)
    for d in range(2):
        for par in range(2):
            stage2(d, par, par)
    for d in range(2):
        stage1(d, 2)
    lax.fori_loop(0, nchunk // 2, body, 0)

    gn = gn_ref[...]
    blk = 8 * L

    def fin(i, carry):
        rows = pl.ds(pl.multiple_of(i * blk, blk), blk)
        o = acc_s[rows, :]
        g = g_ref[rows, :].astype(F32)
        o = o * lax.rsqrt(jnp.mean(o * o, axis=-1, keepdims=True) + GN_EPS)
        o_ref[rows, :] = (o * gn * (g * _sigmoid(g))).astype(o_ref.dtype)
        return carry

    lax.fori_loop(0, (nchunk * L) // blk, fin, 0)


def _hgrn(proj, lb_raw, gn, *, batch, seq, layer_j):
    nchunk = seq // CHUNK
    assert nchunk % 2 == 0 and seq % (8 * CHUNK) == 0
    a3, rs, mk = _hgrn_constants()
    n_even = lb_raw.shape[1]
    nh = N_REC_HEADS

    def col(kind):
        return pl.BlockSpec((seq, HEAD), lambda b, h: (b, kind * nh + h))

    const3 = lambda b, h: (0, 0, 0)
    const4 = lambda b, h: (0, 0, 0, 0)
    return pl.pallas_call(
        functools.partial(_hgrn_kernel, layer_j=layer_j, nchunk=nchunk),
        grid=(batch, nh),
        in_specs=[col(0), col(1), col(2), col(3), col(4),
                  pl.BlockSpec((2, n_even, HEAD), lambda b, h: (0, 0, h)),
                  pl.BlockSpec((1, HEAD), lambda b, h: (0, h)),
                  pl.BlockSpec(a3.shape, const3),
                  pl.BlockSpec(rs.shape, const4),
                  pl.BlockSpec(mk.shape, const4)],
        out_specs=pl.BlockSpec((seq, HEAD), lambda b, h: (b, h)),
        out_shape=jax.ShapeDtypeStruct((batch * seq, nh * HEAD), BF16),
        scratch_shapes=[pltpu.VMEM((seq, HEAD), F32), pltpu.VMEM((2, HEAD, HEAD), F32),
                        pltpu.VMEM((2, 2, len(_LEVELS) + 4, CHUNK, HEAD), BF16),
                        pltpu.VMEM((2, 2, CHUNK, CHUNK), BF16),
                        pltpu.VMEM((2, 2, 8, HEAD), F32),
                        pltpu.VMEM((2, 2, CHUNK, HEAD), BF16),
                        pltpu.VMEM((2, 2, HEAD, HEAD), F32),
                        pltpu.VMEM((2, 2, 8, HEAD), F32)],
        compiler_params=_cparams(2),
    )(proj, proj, proj, proj, proj, lb_raw, gn.reshape(1, -1),
      jnp.asarray(a3, BF16), jnp.asarray(rs), jnp.asarray(mk))


def _mlstm_constants():
    L = CHUNK
    t = np.arange(L)
    ut = (t[:, None] <= t[None, :]).astype(np.float32)
    cum = np.stack([ut, ut[::-1, ::-1]])
    tril = (t[None, :] <= t[:, None]).astype(np.float32)
    causal = np.stack([tril, tril[::-1, ::-1]])
    sel = np.zeros((HEAD, 4 * GATE_GROUP * HEAD), np.float32)
    for q in range(4):
        for p in range(3):
            for j in range(GATE_GROUP):
                sel[(q * 3 + p) * GATE_GROUP + j, (q * GATE_GROUP + j) * HEAD:(q * GATE_GROUP + j + 1) * HEAD] = 1.0
    return cum, causal, sel


def _mlstm_kernel(xq_ref, xk_ref, v_ref, og_ref, gates_ref, gb_ref, cwq_ref, cwk_ref, gn_ref,
                  cum_ref, cm_ref, sel_ref, o_ref, q_s, k_s, acc_s, b_s, c_s, cma_s,
                  bl_s, gm_s, mst_s, mnew_s, wold_s, qkw_s, inc_s, nd_s, wi_s, winter_s, wsc_s, floor_s,
                  *, nchunk):
    L = CHUNK
    pad = CONV_W // 2
    halo = 16

    def conv_chunk(c, carry):
        rows = pl.ds(pl.multiple_of(c * L, L), L)
        prev = pl.ds(pl.multiple_of(jnp.maximum(c * L - halo, 0), halo), halo)
        nxt = pl.ds(pl.multiple_of(jnp.minimum(c * L + L, nchunk * L - halo), halo), halo)
        has_prev = jnp.where(c > 0, 1.0, 0.0).astype(F32)
        has_next = jnp.where(c < nchunk - 1, 1.0, 0.0).astype(F32)
        for x_ref, w_ref, dst, scale in ((xq_ref, cwq_ref, q_s, 1.0), (xk_ref, cwk_ref, k_s, HEAD ** -0.5)):
            win = jnp.concatenate([x_ref[prev, :].astype(F32) * has_prev, x_ref[rows, :].astype(F32),
                                   x_ref[nxt, :].astype(F32) * has_next], axis=0)
            w = w_ref[...]
            acc = win[halo - pad:halo - pad + L] * w[0:1, :]
            for j in range(1, CONV_W):
                acc = acc + win[halo - pad + j:halo - pad + j + L] * w[j:j + 1, :]
            y = acc * _sigmoid(acc)
            dst[rows, :] = y * scale if scale != 1.0 else y
        return carry

    lax.fori_loop(0, nchunk, conv_chunk, 0)

    for d in range(2):
        lf2 = _log_sigmoid(gates_ref[2 + d] + gb_ref[2 + d])
        hi, mid, lo = _split3(lf2)
        cm = cum_ref[d]
        b2 = _dot(hi, cm) + _dot(mid, cm) + _dot(lo, cm)
        b_s[d] = b2
        last = L - 1 if d == 0 else 0
        bl_s[d] = b2[:, last:last + 1]
        li2 = gates_ref[d] + gb_ref[d]
        gm_s[d] = jnp.max(b2[:, last:last + 1] - b2 + li2, axis=1, keepdims=True)
        x = jnp.concatenate([li2 - b2, jnp.full((nchunk, HEAD - L), NEG_BIG, F32)], axis=1)
        lane_x = lax.broadcasted_iota(jnp.int32, x.shape, 1)
        sh = 1
        while sh < L:
            if d == 0:
                x = jnp.maximum(x, jnp.where(lane_x >= sh, pltpu.roll(x, sh, axis=1), NEG_BIG))
            else:
                x = jnp.maximum(x, pltpu.roll(x, HEAD - sh, axis=1))
            sh *= 2
        cma_s[d] = x[:, :L]

    c_s[...] = jnp.zeros_like(c_s)
    lane = lax.broadcasted_iota(jnp.int32, (L, HEAD), 1)
    ones_col = (lane == 0).astype(BF16)

    def stab_step(step, m):
        new = []
        for d in range(2):
            c = step if d == 0 else nchunk - 1 - step
            sl = pl.ds(c, 1)
            mst_s[d, sl, :] = m[d]
            m_new = jnp.maximum(bl_s[d, sl, :] + m[d], gm_s[d, sl, :])
            mnew_s[d, sl, :] = m_new
            new.append(m_new)
        return tuple(new)

    lax.fori_loop(0, nchunk, stab_step, (jnp.zeros((1, 1), F32), jnp.zeros((1, 1), F32)))

    G = GATE_GROUP
    causal = [cm_ref[d] > 0.5 for d in range(2)]

    def gate_weights(g, carry):
        sl = pl.ds(pl.multiple_of(g * G, G), G)
        for d in range(2):
            br = b_s[d, sl, :]
            li = gates_ref[d, sl, :] + gb_ref[d]
            a = li - br
            m_st = mst_s[d, sl, :]
            m_new = mnew_s[d, sl, :]
            last = L - 1 if d == 0 else 0
            b_last = br[:, last:last + 1]
            mx = jnp.maximum(m_st, cma_s[d, sl, :])
            quantities = (mx, jnp.exp(m_st - mx), jnp.exp(-(br + mx)), jnp.exp(b_last - br + li - m_new))
            parts = [p.astype(F32) for qty in quantities for p in _split3(qty)]
            parts.append(jnp.zeros((HEAD - len(parts) * G, L), F32))
            cols = _dot_tn(jnp.concatenate(parts, axis=0).astype(BF16), sel_ref[...])
            for j in range(G):
                rows = pl.ds(pl.multiple_of((g * G + j) * L, L), L)
                mx_col, winter, floor, wsc = (cols[:, (q * G + j) * HEAD:(q * G + j + 1) * HEAD] for q in range(4))
                am = jnp.where(causal[d], jnp.broadcast_to(a[j:j + 1], (L, L)), NEG_BIG)
                wi_s[d, rows, :] = jnp.exp(am - mx_col[:, :L])
                winter_s[d, rows, :] = winter
                floor_s[d, rows, :] = floor[:, 0:1]
                wsc_s[d, rows, :] = wsc
            wold_s[d, sl, :] = jnp.exp(b_last + m_st - m_new)
        return carry

    lax.fori_loop(0, nchunk // G, gate_weights, 0)

    def chunk_index(d, step):
        step = jnp.minimum(step, nchunk - 1)
        return step if d == 0 else nchunk - 1 - step

    def chunk_rows(d, step):
        return pl.ds(pl.multiple_of(chunk_index(d, step) * L, L), L)

    def value_aug(rows):
        return jnp.concatenate([v_ref[rows, :], ones_col], axis=1)

    def stage_a(d, step, slot):
        rows = chunk_rows(d, step)
        k = k_s[rows, :]
        qkw_s[slot, d] = (_dot_nt(q_s[rows, :].astype(BF16), k.astype(BF16)) * wi_s[d, rows, :]).astype(BF16)
        inc_s[slot, d] = _dot_tn((k * wsc_s[d, rows, :]).astype(BF16), value_aug(rows))

    def stage_b(d, step, slot):
        rows = chunk_rows(d, step)
        cst = c_s[d]
        w_inter = winter_s[d, rows, :]
        nd_s[slot, d] = (_dot(qkw_s[slot, d], value_aug(rows))
                         + jnp.concatenate([w_inter, w_inter], axis=1)
                         * _dot(q_s[rows, :].astype(BF16), cst.astype(BF16)))
        c_s[d] = wold_s[d, pl.ds(chunk_index(d, step), 1), :] * cst + inc_s[slot, d]

    def stage_c(d, step, slot):
        rows = chunk_rows(d, step)
        nd = nd_s[slot, d]
        h = nd[:, :HEAD] / jnp.maximum(jnp.abs(nd[:, HEAD:HEAD + 1]), floor_s[d, rows, :])
        acc_s[rows, :] = acc_s[rows, :] + h

    def body(it, with_c):
        for par in range(2):
            for d in range(2):
                if with_c:
                    stage_c(d, 2 * it - 2 + par, par)
        for par in range(2):
            for d in range(2):
                stage_b(d, 2 * it + par, par)
        for par in range(2):
            for d in range(2):
                stage_a(d, 2 * it + 2 + par, par)

    def loop_body(it, carry):
        body(it, True)
        return carry

    acc_s[...] = jnp.zeros_like(acc_s)
    for d in range(2):
        for par in range(2):
            stage_a(d, par, par)
    body(0, False)
    lax.fori_loop(1, nchunk // 2, loop_body, 0)
    for d in range(2):
        for par in range(2):
            stage_c(d, nchunk - 2 + par, par)

    gn = gn_ref[...]
    blk = 8 * L

    def fin(i, carry):
        rows = pl.ds(pl.multiple_of(i * blk, blk), blk)
        h = acc_s[rows, :]
        mu = jnp.mean(h, axis=-1, keepdims=True)
        cen = h - mu
        hn = cen * lax.rsqrt(jnp.mean(cen * cen, axis=-1, keepdims=True) + GN_EPS)
        o_ref[rows, :] = (hn * gn * _sigmoid(og_ref[rows, :].astype(F32))).astype(o_ref.dtype)
        return carry

    lax.fori_loop(0, (nchunk * L) // blk, fin, 0)


def _mlstm(proj, gates, gate_bias, conv_w, gn, *, batch, seq):
    nchunk = seq // CHUNK
    assert nchunk % GATE_GROUP == 0 and nchunk % 2 == 0 and seq % (8 * CHUNK) == 0
    nh = N_REC_HEADS
    width = nh * HEAD
    cum, causal, sel = _mlstm_constants()
    g5 = gates.reshape(batch, nchunk, CHUNK, 4, nh).transpose(0, 4, 3, 1, 2)
    gb = jnp.broadcast_to(gate_bias.reshape(4, nh).T[:, :, None, None], (nh, 4, 1, CHUNK))
    cw = jnp.pad(conv_w, ((0, 8 - CONV_W), (0, 0)))
    first_col = 5 * nh

    def col(kind):
        return pl.BlockSpec((seq, HEAD), lambda b, h: (b, first_col + kind * nh + h))

    const2 = lambda b, h: (0, 0)
    const3 = lambda b, h: (0, 0, 0)
    return pl.pallas_call(
        functools.partial(_mlstm_kernel, nchunk=nchunk),
        grid=(batch, nh),
        in_specs=[col(0), col(1), col(2), col(3),
                  pl.BlockSpec((None, None, 4, nchunk, CHUNK), lambda b, h: (b, h, 0, 0, 0)),
                  pl.BlockSpec((None, 4, 1, CHUNK), lambda b, h: (h, 0, 0, 0)),
                  pl.BlockSpec((8, HEAD), lambda b, h: (0, h)),
                  pl.BlockSpec((8, HEAD), lambda b, h: (0, nh + h)),
                  pl.BlockSpec((1, HEAD), lambda b, h: (0, h)),
                  pl.BlockSpec(cum.shape, const3),
                  pl.BlockSpec(causal.shape, const3),
                  pl.BlockSpec(sel.shape, const2)],
        out_specs=pl.BlockSpec((seq, HEAD), lambda b, h: (b, h)),
        out_shape=jax.ShapeDtypeStruct((batch * seq, width), BF16),
        scratch_shapes=[pltpu.VMEM((seq, HEAD), F32), pltpu.VMEM((seq, HEAD), F32),
                        pltpu.VMEM((seq, HEAD), F32), pltpu.VMEM((2, nchunk, CHUNK), F32),
                        pltpu.VMEM((2, HEAD, 2 * HEAD), F32), pltpu.VMEM((2, nchunk, CHUNK), F32)]
                       + [pltpu.VMEM((2, nchunk, 1), F32)] * 5
                       + [pltpu.VMEM((2, 2, CHUNK, CHUNK), BF16), pltpu.VMEM((2, 2, HEAD, 2 * HEAD), F32),
                          pltpu.VMEM((2, 2, CHUNK, 2 * HEAD), F32),
                          pltpu.VMEM((2, seq, CHUNK), F32), pltpu.VMEM((2, seq, HEAD), F32),
                          pltpu.VMEM((2, seq, HEAD), F32), pltpu.VMEM((2, seq, 1), F32)],
        compiler_params=_cparams(2),
    )(proj, proj, proj, proj, g5, gb, cw, cw, gn.reshape(1, -1),
      jnp.asarray(cum, BF16), jnp.asarray(causal), jnp.asarray(sel, BF16))


def _na_bias_tables(rpb):
    w = GRID_W
    qc = np.arange(w)[:, None]
    kc = np.arange(w)[None, :]
    c0 = np.clip(qc - WIN_C // 2, 0, w - WIN_C)
    valid = (kc >= c0) & (kc < c0 + WIN_C)
    cidx = np.clip(kc - qc + WIN_C - 1, 0, 2 * WIN_C - 2)
    tiles = jnp.where(jnp.asarray(valid)[None, None], rpb.astype(F32)[:, :, cidx] * LOG2_E, NEG_BIG)
    ridx = np.arange(WIN_R)[None, :] - np.arange(WIN_R)[:, None] + WIN_R - 1
    per_t = tiles[:, ridx]
    return per_t.transpose(0, 1, 3, 2, 4).reshape(tiles.shape[0], WIN_R, w, WIN_R * w)


def _na_kernel(q_ref, k_ref, v_ref, bias_ref, o_ref, s_even, s_odd, *, n_rows):
    w = GRID_W
    nk = WIN_R * w
    lane_head = lax.broadcasted_iota(jnp.int32, (w, HEAD), 1) // NA_DH
    head_masks = [lane_head == hh for hh in range(NA_GROUP)]

    def window_start(r):
        return jnp.clip(r - WIN_R // 2, 0, n_rows - WIN_R)

    def logits(r, dst):
        q = q_ref[pl.ds(pl.multiple_of(r * w, w), w), :]
        zero = jnp.zeros_like(q)
        qm = jnp.concatenate([jnp.where(head_masks[hh], q, zero) for hh in range(NA_GROUP)], axis=0)
        r0 = window_start(r)
        s = _dot_nt(qm, k_ref[pl.ds(pl.multiple_of(r0 * w, w), nk), :])
        dst[...] = s + jnp.concatenate([bias_ref[hh, r - r0] for hh in range(NA_GROUP)], axis=0)

    def attend(r, src):
        s = src[...]
        pexp = jnp.exp2(s - jnp.max(s, axis=1, keepdims=True))
        l = jnp.sum(pexp, axis=1, keepdims=True)
        r0 = window_start(r)
        o = _dot(pexp.astype(BF16), v_ref[pl.ds(pl.multiple_of(r0 * w, w), nk), :]) / l
        out = jnp.where(head_masks[0], o[0:w], 0.0)
        for hh in range(1, NA_GROUP):
            out = out + jnp.where(head_masks[hh], o[hh * w:(hh + 1) * w], 0.0)
        o_ref[pl.ds(pl.multiple_of(r * w, w), w), :] = out.astype(o_ref.dtype)

    logits(0, s_even)

    def two_rows(k, carry):
        r = 2 * k
        logits(r + 1, s_odd)
        attend(r, s_even)
        logits(jnp.minimum(r + 2, n_rows - 1), s_even)
        attend(r + 1, s_odd)
        return carry

    lax.fori_loop(0, n_rows // 2, two_rows, 0)


def _neighbourhood_attention(qkv, bias, *, batch, seq):
    n_rows = seq // GRID_W
    assert n_rows % 2 == 0 and n_rows >= WIN_R
    d_model = qkv.shape[1] // 3
    n_groups = d_model // HEAD
    return pl.pallas_call(
        functools.partial(_na_kernel, n_rows=n_rows),
        grid=(batch, n_groups),
        in_specs=[pl.BlockSpec((seq, HEAD), lambda b, g: (b, g)),
                  pl.BlockSpec((seq, HEAD), lambda b, g: (b, n_groups + g)),
                  pl.BlockSpec((seq, HEAD), lambda b, g: (b, 2 * n_groups + g)),
                  pl.BlockSpec((NA_GROUP, WIN_R, GRID_W, WIN_R * GRID_W), lambda b, g: (g, 0, 0, 0))],
        out_specs=pl.BlockSpec((seq, HEAD), lambda b, g: (b, g)),
        out_shape=jax.ShapeDtypeStruct((batch * seq, d_model), BF16),
        scratch_shapes=[pltpu.VMEM((NA_GROUP * GRID_W, WIN_R * GRID_W), F32)] * 2,
        compiler_params=_cparams(2),
    )(qkv, qkv, qkv, bias)


def _ffn_kernel(*refs, n_mix, alpha):
    x_ref = refs[0]
    mix_refs = refs[1:1 + 2 * n_mix]
    lmg_ref, lmb_ref, lfg_ref, lfb_ref, wg_ref, wu_ref, wd_ref, o_ref = refs[1 + 2 * n_mix:]
    mix = _dot(mix_refs[0][...], mix_refs[1][...])
    for i in range(1, n_mix):
        mix = mix + _dot(mix_refs[2 * i][...], mix_refs[2 * i + 1][...])
    h = _layer_norm(alpha * x_ref[...] + mix, lmg_ref[...], lmb_ref[...])
    hb = h.astype(BF16)
    g = _dot(hb, wg_ref[...])
    u = _dot(hb, wu_ref[...])
    y = _dot((g * _sigmoid(g) * u).astype(BF16), wd_ref[...])
    o_ref[...] = _layer_norm(alpha * h + y, lfg_ref[...], lfb_ref[...])


def _mixer_out_ffn(x, mix_pairs, ln_mix_g, ln_mix_b, ln_ffn_g, ln_ffn_b, wg, wu, wd, *, alpha, tm):
    t, d = x.shape
    assert t % tm == 0
    row = lambda i: (i, 0)
    const = lambda i: (0, 0)
    resident = functools.partial(pl.BlockSpec, index_map=const, pipeline_mode=pl.Buffered(1))
    in_specs = [pl.BlockSpec((tm, d), row)]
    args = [x]
    for o, w in mix_pairs:
        in_specs += [pl.BlockSpec((tm, o.shape[1]), row), resident(w.shape)]
        args += [o, w]
    in_specs += [resident((1, d))] * 4
    args += [ln_mix_g.reshape(1, d), ln_mix_b.reshape(1, d), ln_ffn_g.reshape(1, d), ln_ffn_b.reshape(1, d)]
    in_specs += [resident(wg.shape), resident(wu.shape), resident(wd.shape)]
    args += [wg, wu, wd]
    return pl.pallas_call(
        functools.partial(_ffn_kernel, n_mix=len(mix_pairs), alpha=alpha),
        grid=(t // tm,),
        in_specs=in_specs,
        out_specs=pl.BlockSpec((tm, d), row),
        out_shape=jax.ShapeDtypeStruct((t, d), F32),
        compiler_params=_cparams(1),
    )(*args)


def _row_tile(t):
    for tm in (1024, 512, 256, 128):
        if t % tm == 0:
            return tm
    raise ValueError(f"token count {t} is not a multiple of 128")


def kernel(x, w_in_even, gate_bias_even, lb_raw, conv_qk, gn_hgrn, gn_mlstm, w_out_even, w_qkv_odd, rpb_odd,
           w_out_odd, ln_mix_g, ln_mix_b, ln_ffn_g, ln_ffn_b, w_ffn_gate, w_ffn_up, w_ffn_down):
    batch, seq, d_model = x.shape
    depth = ln_mix_g.shape[0]
    alpha = (2.0 * depth) ** 0.25
    t = batch * seq
    tm = _row_tile(t)
    a_width = N_REC_HEADS * HEAD
    main_cols = 9 * a_width
    n_gate = 4 * N_REC_HEADS
    tm_ffn = min(tm, 512)
    tn_main = 1536 if main_cols % 1536 == 0 else 512

    h = x.reshape(t, d_model)
    for layer in range(depth):
        j = layer // 2
        if layer % 2 == 0:
            w_in = w_in_even[j]
            w_main = w_in[:, :main_cols].astype(BF16)
            w_gate = jnp.pad(w_in[:, main_cols:], ((0, 0), (0, V7X_LANES - n_gate))).astype(BF16)
            proj = _matmul(h, w_main, out_dtype=BF16, tm=tm, tn=tn_main)
            gates = _matmul(h, w_gate, out_dtype=F32, tm=tm, tn=V7X_LANES)[:, :n_gate]
            o_a = _hgrn(proj, lb_raw, gn_hgrn[j], batch=batch, seq=seq, layer_j=j)
            h_b = _mlstm(proj, gates, gate_bias_even[j], conv_qk[j], gn_mlstm[j], batch=batch, seq=seq)
            w_out = w_out_even[j].astype(BF16)
            mix_pairs = [(o_a, w_out[:a_width]), (h_b, w_out[a_width:])]
        else:
            qkv = _qkv_projection(h, w_qkv_odd[j].astype(BF16), q_scale=NA_DH ** -0.5 * LOG2_E, tm=tm)
            o = _neighbourhood_attention(qkv, _na_bias_tables(rpb_odd[j]), batch=batch, seq=seq)
            mix_pairs = [(o, w_out_odd[j].astype(BF16))]
        h = _mixer_out_ffn(h, mix_pairs, ln_mix_g[layer], ln_mix_b[layer], ln_ffn_g[layer], ln_ffn_b[layer],
                           w_ffn_gate[layer].astype(BF16), w_ffn_up[layer].astype(BF16),
                           w_ffn_down[layer].astype(BF16), alpha=alpha, tm=tm_ffn)
    return h.reshape(batch, seq, d_model)
```

```python
import functools

import numpy as np
import jax
import jax.numpy as jnp
from jax import lax
from jax.experimental import pallas as pl
from jax.experimental.pallas import tpu as pltpu

F32 = jnp.float32
BF16 = jnp.bfloat16

GRID_W = 64
HEAD = 128
N_REC_HEADS = 4
CHUNK = 64
CONV_W = 5
GATE_GROUP = 8
NA_DH = 32
NA_GROUP = HEAD // NA_DH
WIN_R = 8
WIN_C = 16
LN_EPS = 1e-5
GN_EPS = 1e-6
NEG_BIG = -1e30
LB_FLOOR = 1e-30
LOG2_E = 1.4426950408889634

V7X_LANES = 128
V7X_VMEM_LIMIT_BYTES = 56 * 1024 * 1024

_LEVELS = (32, 16, 8, 4, 2, 1)


def _cparams(n_grid_axes):
    return pltpu.CompilerParams(
        dimension_semantics=("arbitrary",) * n_grid_axes,
        vmem_limit_bytes=V7X_VMEM_LIMIT_BYTES)


def _dot(a, b):
    return jnp.dot(a, b, preferred_element_type=F32)


def _dot_nt(a, b):
    return lax.dot_general(a, b, (((1,), (1,)), ((), ())), preferred_element_type=F32)


def _dot_tn(a, b):
    return lax.dot_general(a, b, (((0,), (0,)), ((), ())), preferred_element_type=F32)


def _split3(x):
    hi = x.astype(BF16)
    r1 = x - hi.astype(F32)
    mid = r1.astype(BF16)
    lo = (r1 - mid.astype(F32)).astype(BF16)
    return hi, mid, lo


def _log_sigmoid(z):
    return jnp.minimum(z, 0.0) - jnp.log1p(jnp.exp(-jnp.abs(z)))


def _sigmoid(z):
    return 1.0 / (1.0 + jnp.exp(-z))


def _layer_norm(t, g, b):
    mu = jnp.mean(t, axis=-1, keepdims=True)
    c = t - mu
    var = jnp.mean(c * c, axis=-1, keepdims=True)
    return c * lax.rsqrt(var + LN_EPS) * g + b


def _mm_kernel(x_ref, w_ref, o_ref):
    o_ref[...] = _dot(x_ref[...].astype(BF16), w_ref[...]).astype(o_ref.dtype)


def _matmul(x, w, *, out_dtype, tm, tn):
    t, k = x.shape
    n = w.shape[1]
    assert t % tm == 0 and n % tn == 0
    return pl.pallas_call(
        _mm_kernel,
        grid=(t // tm, n // tn),
        in_specs=[pl.BlockSpec((tm, k), lambda i, j: (i, 0)),
                  pl.BlockSpec((k, tn), lambda i, j: (0, j))],
        out_specs=pl.BlockSpec((tm, tn), lambda i, j: (i, j)),
        out_shape=jax.ShapeDtypeStruct((t, n), out_dtype),
        compiler_params=_cparams(2),
    )(x, w)


def _qkv_kernel(x_ref, w_ref, o_ref, *, q_scale):
    acc = _dot(x_ref[...].astype(BF16), w_ref[...])
    o_ref[...] = (acc * jnp.where(pl.program_id(1) == 0, q_scale, 1.0)).astype(o_ref.dtype)


def _qkv_projection(x, w_qkv, *, q_scale, tm):
    t, k = x.shape
    d = w_qkv.shape[1] // 3
    assert t % tm == 0
    return pl.pallas_call(
        functools.partial(_qkv_kernel, q_scale=q_scale),
        grid=(t // tm, 3),
        in_specs=[pl.BlockSpec((tm, k), lambda i, j: (i, 0)),
                  pl.BlockSpec((k, d), lambda i, j: (0, j))],
        out_specs=pl.BlockSpec((tm, d), lambda i, j: (i, j)),
        out_shape=jax.ShapeDtypeStruct((t, 3 * d), BF16),
        compiler_params=_cparams(2),
    )(x, w_qkv)


def _hgrn_constants():
    L = CHUNK
    t = np.arange(L)
    a_rows, rowsel, masks = [], [], []
    for c in _LEVELS:
        odd = (t // c) % 2 == 1
        rho = (t // (2 * c)) * 2 * c + c - 1
        u = t[None, :]
        a = np.where(odd[:, None], (u > rho[:, None]) & (u <= t[:, None]),
                     (u > t[:, None]) & (u <= rho[:, None]))
        a_rows.append(a.astype(np.float32))
        rowsel.append(np.broadcast_to(odd[:, None], (L, HEAD)).astype(np.float32))
        same = (t[:, None] // (2 * c)) == (t[None, :] // (2 * c))
        masks.append((odd[:, None] & ~odd[None, :] & same).astype(np.float32))
    masks.append(np.eye(L, dtype=np.float32))
    a_rows.append((t[None, :] <= t[:, None]).astype(np.float32))
    a_rows.append((t[None, :] > t[:, None]).astype(np.float32))
    a_f = np.stack(a_rows)
    rs_f = np.stack(rowsel)
    m_f = np.stack(masks)
    a = np.stack([a_f, a_f[:, ::-1, ::-1]]).reshape(2, 8 * L, L)
    a3 = np.concatenate([a, a, a, np.zeros_like(a)], axis=-1)
    rs = np.stack([rs_f, rs_f[:, ::-1]])
    m = np.stack([m_f, m_f[:, ::-1, ::-1]])
    return a3, rs, m


def _hgrn_kernel(q_ref, ff_ref, fb_ref, v_ref, g_ref, lb_ref, gn_ref, a3_ref, rs_ref, mk_ref,
                 o_ref, acc_s, st_s, y_s, sc_s, dec_s, fw_s, inc_s, dec3_s, *, layer_j, nchunk):
    L = CHUNK
    nlev = len(_LEVELS)
    f_refs = (ff_ref, fb_ref)
    Y_QPRE, Y_KSUF, Y_Q, Y_K = nlev, nlev + 1, nlev + 2, nlev + 3

    lbr = lb_ref[...]
    e = jnp.exp(lbr - jnp.max(lbr, axis=1, keepdims=True))
    soft = e / jnp.sum(e, axis=1, keepdims=True)
    cum = soft[:, 0:1, :]
    for i in range(1, layer_j + 1):
        cum = cum + soft[:, i:i + 1, :]
    lb = cum - soft[:, 0:1, :]
    lb_floor = jnp.maximum(lb, LB_FLOOR)
    one_m_lb = 1.0 - lb

    st_s[...] = jnp.zeros_like(st_s)

    def chunk_rows(d, step):
        step = jnp.minimum(step, nchunk - 1)
        c = step if d == 0 else nchunk - 1 - step
        return pl.ds(pl.multiple_of(c * L, L), L)

    def stage1(d, step0):
        qs, ks, parts = [], [], []
        for par in range(2):
            rows = chunk_rows(d, step0 + par)
            qs.append(q_ref[rows, :].astype(F32))
            f = lb_floor[d] + one_m_lb[d] * _sigmoid(f_refs[d][rows, :].astype(F32))
            ks.append(1.0 - f)
            hi, mid, lo = _split3(jnp.log(f) * LOG2_E)
            parts.append(jnp.concatenate([hi, mid, lo, jnp.zeros_like(hi)], axis=0))
        dall = _dot(a3_ref[d], jnp.concatenate(parts, axis=1))
        for slot in range(2):
            q, k = qs[slot], ks[slot]
            eall = jnp.exp2(dall[:, slot * HEAD:(slot + 1) * HEAD])
            for li, c in enumerate(_LEVELS):
                if c % 8 == 0:
                    first_q = 1 if d == 0 else 0
                    x = jnp.concatenate([(q if b % 2 == first_q else k)[b * c:(b + 1) * c] for b in range(L // c)],
                                        axis=0)
                else:
                    x = jnp.where(rs_ref[d, li] > 0.5, q, k)
                y_s[slot, d, li] = (x * eall[li * L:(li + 1) * L]).astype(BF16)
            e_pre = eall[nlev * L:(nlev + 1) * L]
            e_suf = eall[(nlev + 1) * L:(nlev + 2) * L]
            y_s[slot, d, Y_QPRE] = (q * e_pre).astype(BF16)
            y_s[slot, d, Y_KSUF] = (k * e_suf).astype(BF16)
            y_s[slot, d, Y_Q] = q.astype(BF16)
            y_s[slot, d, Y_K] = k.astype(BF16)
            last = L - 1 if d == 0 else 0
            dec_s[slot, d] = jnp.broadcast_to(e_pre[last:last + 1, :], (8, HEAD))

    def stage2(d, step, slot):
        scores = _dot_nt(y_s[slot, d, Y_Q], y_s[slot, d, Y_K]) * mk_ref[d, nlev]
        for li in range(nlev):
            y = y_s[slot, d, li]
            scores = scores + _dot_nt(y, y) * mk_ref[d, li]
        sc_s[slot, d] = scores.astype(BF16)
        fw_s[slot, d] = y_s[slot, d, Y_QPRE]
        inc_s[slot, d] = _dot_tn(v_ref[chunk_rows(d, step), :].astype(BF16), y_s[slot, d, Y_KSUF])
        dec3_s[slot, d] = dec_s[slot, d]

    def stage3(d, step, slot):
        rows = chunk_rows(d, step)
        st = st_s[d]
        o = _dot(sc_s[slot, d], v_ref[rows, :].astype(BF16)) + _dot_nt(fw_s[slot, d], st.astype(BF16))
        st_s[d] = st * dec3_s[slot, d][0:1, :] + inc_s[slot, d]
        acc_s[rows, :] = acc_s[rows, :] + o

    def body(it, carry):
        for par in range(2):
            for d in range(2):
                stage3(d, 2 * it + par, par)
        for par in range(2):
            for d in range(2):
                stage2(d, 2 * it + 2 + par, par)
        for d in range(2):
            stage1(d, 2 * it + 4)
        return carry

    acc_s[...] = jnp.zeros_like(acc_s)
    for d in range(2):
        stage1(d, 0)
    for d in range(2):
        for par in range(2):
            stage2(d, par, par)
    for d in range(2):
        stage1(d, 2)
    lax.fori_loop(0, nchunk // 2, body, 0)

    gn = gn_ref[...]
    blk = 8 * L

    def fin(i, carry):
        rows = pl.ds(pl.multiple_of(i * blk, blk), blk)
        o = acc_s[rows, :]
        g = g_ref[rows, :].astype(F32)
        o = o * lax.rsqrt(jnp.mean(o * o, axis=-1, keepdims=True) + GN_EPS)
        o_ref[rows, :] = (o * gn * (g * _sigmoid(g))).astype(o_ref.dtype)
        return carry

    lax.fori_loop(0, (nchunk * L) // blk, fin, 0)


def _hgrn(proj, lb_raw, gn, *, batch, seq, layer_j):
    nchunk = seq // CHUNK
    assert nchunk % 2 == 0 and seq % (8 * CHUNK) == 0
    a3, rs, mk = _hgrn_constants()
    n_even = lb_raw.shape[1]
    nh = N_REC_HEADS

    def col(kind):
        return pl.BlockSpec((seq, HEAD), lambda b, h: (b, kind * nh + h))

    const3 = lambda b, h: (0, 0, 0)
    const4 = lambda b, h: (0, 0, 0, 0)
    return pl.pallas_call(
        functools.partial(_hgrn_kernel, layer_j=layer_j, nchunk=nchunk),
        grid=(batch, nh),
        in_specs=[col(0), col(1), col(2), col(3), col(4),
                  pl.BlockSpec((2, n_even, HEAD), lambda b, h: (0, 0, h)),
                  pl.BlockSpec((1, HEAD), lambda b, h: (0, h)),
                  pl.BlockSpec(a3.shape, const3),
                  pl.BlockSpec(rs.shape, const4),
                  pl.BlockSpec(mk.shape, const4)],
        out_specs=pl.BlockSpec((seq, HEAD), lambda b, h: (b, h)),
        out_shape=jax.ShapeDtypeStruct((batch * seq, nh * HEAD), BF16),
        scratch_shapes=[pltpu.VMEM((seq, HEAD), F32), pltpu.VMEM((2, HEAD, HEAD), F32),
                        pltpu.VMEM((2, 2, len(_LEVELS) + 4, CHUNK, HEAD), BF16),
                        pltpu.VMEM((2, 2, CHUNK, CHUNK), BF16),
                        pltpu.VMEM((2, 2, 8, HEAD), F32),
                        pltpu.VMEM((2, 2, CHUNK, HEAD), BF16),
                        pltpu.VMEM((2, 2, HEAD, HEAD), F32),
                        pltpu.VMEM((2, 2, 8, HEAD), F32)],
        compiler_params=_cparams(2),
    )(proj, proj, proj, proj, proj, lb_raw, gn.reshape(1, -1),
      jnp.asarray(a3, BF16), jnp.asarray(rs), jnp.asarray(mk))


def _mlstm_constants():
    L = CHUNK
    t = np.arange(L)
    ut = (t[:, None] <= t[None, :]).astype(np.float32)
    cum = np.stack([ut, ut[::-1, ::-1]])
    tril = (t[None, :] <= t[:, None]).astype(np.float32)
    causal = np.stack([tril, tril[::-1, ::-1]])
    sel = np.zeros((HEAD, 4 * GATE_GROUP * HEAD), np.float32)
    for q in range(4):
        for p in range(3):
            for j in range(GATE_GROUP):
                sel[(q * 3 + p) * GATE_GROUP + j, (q * GATE_GROUP + j) * HEAD:(q * GATE_GROUP + j + 1) * HEAD] = 1.0
    return cum, causal, sel


def _mlstm_kernel(xq_ref, xk_ref, v_ref, og_ref, gates_ref, gb_ref, cwq_ref, cwk_ref, gn_ref,
                  cum_ref, cm_ref, sel_ref, o_ref, q_s, k_s, acc_s, b_s, c_s, cma_s,
                  bl_s, gm_s, mst_s, mnew_s, wold_s, qkw_s, inc_s, nd_s, wi_s, winter_s, wsc_s, floor_s,
                  *, nchunk):
    L = CHUNK
    pad = CONV_W // 2
    halo = 16

    def conv_chunk(c, carry):
        rows = pl.ds(pl.multiple_of(c * L, L), L)
        prev = pl.ds(pl.multiple_of(jnp.maximum(c * L - halo, 0), halo), halo)
        nxt = pl.ds(pl.multiple_of(jnp.minimum(c * L + L, nchunk * L - halo), halo), halo)
        has_prev = jnp.where(c > 0, 1.0, 0.0).astype(F32)
        has_next = jnp.where(c < nchunk - 1, 1.0, 0.0).astype(F32)
        for x_ref, w_ref, dst, scale in ((xq_ref, cwq_ref, q_s, 1.0), (xk_ref, cwk_ref, k_s, HEAD ** -0.5)):
            win = jnp.concatenate([x_ref[prev, :].astype(F32) * has_prev, x_ref[rows, :].astype(F32),
                                   x_ref[nxt, :].astype(F32) * has_next], axis=0)
            w = w_ref[...]
            acc = win[halo - pad:halo - pad + L] * w[0:1, :]
            for j in range(1, CONV_W):
                acc = acc + win[halo - pad + j:halo - pad + j + L] * w[j:j + 1, :]
            y = acc * _sigmoid(acc)
            dst[rows, :] = y * scale if scale != 1.0 else y
        return carry

    lax.fori_loop(0, nchunk, conv_chunk, 0)

    for d in range(2):
        lf2 = _log_sigmoid(gates_ref[2 + d] + gb_ref[2 + d])
        hi, mid, lo = _split3(lf2)
        cm = cum_ref[d]
        b2 = _dot(hi, cm) + _dot(mid, cm) + _dot(lo, cm)
        b_s[d] = b2
        last = L - 1 if d == 0 else 0
        bl_s[d] = b2[:, last:last + 1]
        li2 = gates_ref[d] + gb_ref[d]
        gm_s[d] = jnp.max(b2[:, last:last + 1] - b2 + li2, axis=1, keepdims=True)
        x = jnp.concatenate([li2 - b2, jnp.full((nchunk, HEAD - L), NEG_BIG, F32)], axis=1)
        lane_x = lax.broadcasted_iota(jnp.int32, x.shape, 1)
        sh = 1
        while sh < L:
            if d == 0:
                x = jnp.maximum(x, jnp.where(lane_x >= sh, pltpu.roll(x, sh, axis=1), NEG_BIG))
            else:
                x = jnp.maximum(x, pltpu.roll(x, HEAD - sh, axis=1))
            sh *= 2
        cma_s[d] = x[:, :L]

    c_s[...] = jnp.zeros_like(c_s)
    lane = lax.broadcasted_iota(jnp.int32, (L, HEAD), 1)
    ones_col = (lane == 0).astype(BF16)

    def stab_step(step, m):
        new = []
        for d in range(2):
            c = step if d == 0 else nchunk - 1 - step
            sl = pl.ds(c, 1)
            mst_s[d, sl, :] = m[d]
            m_new = jnp.maximum(bl_s[d, sl, :] + m[d], gm_s[d, sl, :])
            mnew_s[d, sl, :] = m_new
            new.append(m_new)
        return tuple(new)

    lax.fori_loop(0, nchunk, stab_step, (jnp.zeros((1, 1), F32), jnp.zeros((1, 1), F32)))

    G = GATE_GROUP
    causal = [cm_ref[d] > 0.5 for d in range(2)]

    def gate_weights(g, carry):
        sl = pl.ds(pl.multiple_of(g * G, G), G)
        for d in range(2):
            br = b_s[d, sl, :]
            li = gates_ref[d, sl, :] + gb_ref[d]
            a = li - br
            m_st = mst_s[d, sl, :]
            m_new = mnew_s[d, sl, :]
            last = L - 1 if d == 0 else 0
            b_last = br[:, last:last + 1]
            mx = jnp.maximum(m_st, cma_s[d, sl, :])
            quantities = (mx, jnp.exp(m_st - mx), jnp.exp(-(br + mx)), jnp.exp(b_last - br + li - m_new))
            parts = [p.astype(F32) for qty in quantities for p in _split3(qty)]
            parts.append(jnp.zeros((HEAD - len(parts) * G, L), F32))
            cols = _dot_tn(jnp.concatenate(parts, axis=0).astype(BF16), sel_ref[...])
            for j in range(G):
                rows = pl.ds(pl.multiple_of((g * G + j) * L, L), L)
                mx_col, winter, floor, wsc = (cols[:, (q * G + j) * HEAD:(q * G + j + 1) * HEAD] for q in range(4))
                am = jnp.where(causal[d], jnp.broadcast_to(a[j:j + 1], (L, L)), NEG_BIG)
                wi_s[d, rows, :] = jnp.exp(am - mx_col[:, :L])
                winter_s[d, rows, :] = winter
                floor_s[d, rows, :] = floor[:, 0:1]
                wsc_s[d, rows, :] = wsc
            wold_s[d, sl, :] = jnp.exp(b_last + m_st - m_new)
        return carry

    lax.fori_loop(0, nchunk // G, gate_weights, 0)

    def chunk_index(d, step):
        step = jnp.minimum(step, nchunk - 1)
        return step if d == 0 else nchunk - 1 - step

    def chunk_rows(d, step):
        return pl.ds(pl.multiple_of(chunk_index(d, step) * L, L), L)

    def value_aug(rows):
        return jnp.concatenate([v_ref[rows, :], ones_col], axis=1)

    def stage_a(d, step, slot):
        rows = chunk_rows(d, step)
        k = k_s[rows, :]
        qkw_s[slot, d] = (_dot_nt(q_s[rows, :].astype(BF16), k.astype(BF16)) * wi_s[d, rows, :]).astype(BF16)
        inc_s[slot, d] = _dot_tn((k * wsc_s[d, rows, :]).astype(BF16), value_aug(rows))

    def stage_b(d, step, slot):
        rows = chunk_rows(d, step)
        cst = c_s[d]
        w_inter = winter_s[d, rows, :]
        nd_s[slot, d] = (_dot(qkw_s[slot, d], value_aug(rows))
                         + jnp.concatenate([w_inter, w_inter], axis=1)
                         * _dot(q_s[rows, :].astype(BF16), cst.astype(BF16)))
        c_s[d] = wold_s[d, pl.ds(chunk_index(d, step), 1), :] * cst + inc_s[slot, d]

    def stage_c(d, step, slot):
        rows = chunk_rows(d, step)
        nd = nd_s[slot, d]
        h = nd[:, :HEAD] / jnp.maximum(jnp.abs(nd[:, HEAD:HEAD + 1]), floor_s[d, rows, :])
        acc_s[rows, :] = acc_s[rows, :] + h

    def body(it, with_c):
        for par in range(2):
            for d in range(2):
                if with_c:
                    stage_c(d, 2 * it - 2 + par, par)
        for par in range(2):
            for d in range(2):
                stage_b(d, 2 * it + par, par)
        for par in range(2):
            for d in range(2):
                stage_a(d, 2 * it + 2 + par, par)

    def loop_body(it, carry):
        body(it, True)
        return carry

    acc_s[...] = jnp.zeros_like(acc_s)
    for d in range(2):
        for par in range(2):
            stage_a(d, par, par)
    body(0, False)
    lax.fori_loop(1, nchunk // 2, loop_body, 0)
    for d in range(2):
        for par in range(2):
            stage_c(d, nchunk - 2 + par, par)

    gn = gn_ref[...]
    blk = 8 * L

    def fin(i, carry):
        rows = pl.ds(pl.multiple_of(i * blk, blk), blk)
        h = acc_s[rows, :]
        mu = jnp.mean(h, axis=-1, keepdims=True)
        cen = h - mu
        hn = cen * lax.rsqrt(jnp.mean(cen * cen, axis=-1, keepdims=True) + GN_EPS)
        o_ref[rows, :] = (hn * gn * _sigmoid(og_ref[rows, :].astype(F32))).astype(o_ref.dtype)
        return carry

    lax.fori_loop(0, (nchunk * L) // blk, fin, 0)


def _mlstm(proj, gates, gate_bias, conv_w, gn, *, batch, seq):
    nchunk = seq // CHUNK
    assert nchunk % GATE_GROUP == 0 and nchunk % 2 == 0 and seq % (8 * CHUNK) == 0
    nh = N_REC_HEADS
    width = nh * HEAD
    cum, causal, sel = _mlstm_constants()
    g5 = gates.reshape(batch, nchunk, CHUNK, 4, nh).transpose(0, 4, 3, 1, 2)
    gb = jnp.broadcast_to(gate_bias.reshape(4, nh).T[:, :, None, None], (nh, 4, 1, CHUNK))
    cw = jnp.pad(conv_w, ((0, 8 - CONV_W), (0, 0)))
    first_col = 5 * nh

    def col(kind):
        return pl.BlockSpec((seq, HEAD), lambda b, h: (b, first_col + kind * nh + h))

    const2 = lambda b, h: (0, 0)
    const3 = lambda b, h: (0, 0, 0)
    return pl.pallas_call(
        functools.partial(_mlstm_kernel, nchunk=nchunk),
        grid=(batch, nh),
        in_specs=[col(0), col(1), col(2), col(3),
                  pl.BlockSpec((None, None, 4, nchunk, CHUNK), lambda b, h: (b, h, 0, 0, 0)),
                  pl.BlockSpec((None, 4, 1, CHUNK), lambda b, h: (h, 0, 0, 0)),
                  pl.BlockSpec((8, HEAD), lambda b, h: (0, h)),
                  pl.BlockSpec((8, HEAD), lambda b, h: (0, nh + h)),
                  pl.BlockSpec((1, HEAD), lambda b, h: (0, h)),
                  pl.BlockSpec(cum.shape, const3),
                  pl.BlockSpec(causal.shape, const3),
                  pl.BlockSpec(sel.shape, const2)],
        out_specs=pl.BlockSpec((seq, HEAD), lambda b, h: (b, h)),
        out_shape=jax.ShapeDtypeStruct((batch * seq, width), BF16),
        scratch_shapes=[pltpu.VMEM((seq, HEAD), F32), pltpu.VMEM((seq, HEAD), F32),
                        pltpu.VMEM((seq, HEAD), F32), pltpu.VMEM((2, nchunk, CHUNK), F32),
                        pltpu.VMEM((2, HEAD, 2 * HEAD), F32), pltpu.VMEM((2, nchunk, CHUNK), F32)]
                       + [pltpu.VMEM((2, nchunk, 1), F32)] * 5
                       + [pltpu.VMEM((2, 2, CHUNK, CHUNK), BF16), pltpu.VMEM((2, 2, HEAD, 2 * HEAD), F32),
                          pltpu.VMEM((2, 2, CHUNK, 2 * HEAD), F32),
                          pltpu.VMEM((2, seq, CHUNK), F32), pltpu.VMEM((2, seq, HEAD), F32),
                          pltpu.VMEM((2, seq, HEAD), F32), pltpu.VMEM((2, seq, 1), F32)],
        compiler_params=_cparams(2),
    )(proj, proj, proj, proj, g5, gb, cw, cw, gn.reshape(1, -1),
      jnp.asarray(cum, BF16), jnp.asarray(causal), jnp.asarray(sel, BF16))


def _na_bias_tables(rpb):
    w = GRID_W
    qc = np.arange(w)[:, None]
    kc = np.arange(w)[None, :]
    c0 = np.clip(qc - WIN_C // 2, 0, w - WIN_C)
    valid = (kc >= c0) & (kc < c0 + WIN_C)
    cidx = np.clip(kc - qc + WIN_C - 1, 0, 2 * WIN_C - 2)
    tiles = jnp.where(jnp.asarray(valid)[None, None], rpb.astype(F32)[:, :, cidx] * LOG2_E, NEG_BIG)
    return jnp.concatenate([tiles[:, :-1], tiles[:, 1:]], axis=-1)


def _na_kernel(q_ref, k_ref, v_ref, bias_ref, o_ref, s_even, s_odd, *, n_rows):
    w = GRID_W
    nk = WIN_R * w
    lane_head = lax.broadcasted_iota(jnp.int32, (w, HEAD), 1) // NA_DH
    head_masks = [lane_head == hh for hh in range(NA_GROUP)]

    def window_start(r):
        return jnp.clip(r - WIN_R // 2, 0, n_rows - WIN_R)

    def logits(r, dst):
        q = q_ref[pl.ds(pl.multiple_of(r * w, w), w), :]
        zero = jnp.zeros_like(q)
        qm = jnp.concatenate([jnp.where(head_masks[hh], q, zero) for hh in range(NA_GROUP)], axis=0)
        r0 = window_start(r)
        s = _dot_nt(qm, k_ref[pl.ds(pl.multiple_of(r0 * w, w), nk), :])
        first = WIN_R - 1 - (r - r0)
        dst[...] = s + jnp.concatenate(
            [jnp.concatenate([bias_ref[hh, first + 2 * m] for m in range(WIN_R // 2)], axis=1)
             for hh in range(NA_GROUP)], axis=0)

    def attend(r, src):
        s = src[...]
        pexp = jnp.exp2(s - jnp.max(s, axis=1, keepdims=True))
        l = jnp.sum(pexp, axis=1, keepdims=True)
        r0 = window_start(r)
        o = _dot(pexp.astype(BF16), v_ref[pl.ds(pl.multiple_of(r0 * w, w), nk), :]) / l
        out = jnp.where(head_masks[0], o[0:w], 0.0)
        for hh in range(1, NA_GROUP):
            out = out + jnp.where(head_masks[hh], o[hh * w:(hh + 1) * w], 0.0)
        o_ref[pl.ds(pl.multiple_of(r * w, w), w), :] = out.astype(o_ref.dtype)

    logits(0, s_even)

    def two_rows(k, carry):
        r = 2 * k
        logits(r + 1, s_odd)
        attend(r, s_even)
        logits(jnp.minimum(r + 2, n_rows - 1), s_even)
        attend(r + 1, s_odd)
        return carry

    lax.fori_loop(0, n_rows // 2, two_rows, 0)


def _neighbourhood_attention(qkv, bias, *, batch, seq):
    n_rows = seq // GRID_W
    assert n_rows % 2 == 0 and n_rows >= WIN_R
    d_model = qkv.shape[1] // 3
    n_groups = d_model // HEAD
    return pl.pallas_call(
        functools.partial(_na_kernel, n_rows=n_rows),
        grid=(batch, n_groups),
        in_specs=[pl.BlockSpec((seq, HEAD), lambda b, g: (b, g)),
                  pl.BlockSpec((seq, HEAD), lambda b, g: (b, n_groups + g)),
                  pl.BlockSpec((seq, HEAD), lambda b, g: (b, 2 * n_groups + g)),
                  pl.BlockSpec((NA_GROUP, 2 * WIN_R - 2, GRID_W, 2 * GRID_W), lambda b, g: (g, 0, 0, 0))],
        out_specs=pl.BlockSpec((seq, HEAD), lambda b, g: (b, g)),
        out_shape=jax.ShapeDtypeStruct((batch * seq, d_model), BF16),
        scratch_shapes=[pltpu.VMEM((NA_GROUP * GRID_W, WIN_R * GRID_W), F32)] * 2,
        compiler_params=_cparams(2),
    )(qkv, qkv, qkv, bias)


def _ffn_kernel(*refs, n_mix, alpha):
    x_ref = refs[0]
    mix_refs = refs[1:1 + 2 * n_mix]
    lmg_ref, lmb_ref, lfg_ref, lfb_ref, wg_ref, wu_ref, wd_ref, o_ref = refs[1 + 2 * n_mix:]
    half = x_ref.shape[0] // 2
    rows = (slice(0, half), slice(half, 2 * half))

    def pre(r):
        mix = _dot(mix_refs[0][r, :], mix_refs[1][...])
        for i in range(1, n_mix):
            mix = mix + _dot(mix_refs[2 * i][r, :], mix_refs[2 * i + 1][...])
        return _layer_norm(alpha * x_ref[r, :] + mix, lmg_ref[...], lmb_ref[...])

    def act(h):
        hb = h.astype(BF16)
        g = _dot(hb, wg_ref[...])
        u = _dot(hb, wu_ref[...])
        return (g * _sigmoid(g) * u).astype(BF16)

    h0 = pre(rows[0])
    a0 = act(h0)
    h1 = pre(rows[1])
    y0 = _dot(a0, wd_ref[...])
    a1 = act(h1)
    o_ref[rows[0], :] = _layer_norm(alpha * h0 + y0, lfg_ref[...], lfb_ref[...])
    y1 = _dot(a1, wd_ref[...])
    o_ref[rows[1], :] = _layer_norm(alpha * h1 + y1, lfg_ref[...], lfb_ref[...])


def _mixer_out_ffn(x, mix_pairs, ln_mix_g, ln_mix_b, ln_ffn_g, ln_ffn_b, wg, wu, wd, *, alpha, tm):
    t, d = x.shape
    assert t % tm == 0
    row = lambda i: (i, 0)
    const = lambda i: (0, 0)
    resident = functools.partial(pl.BlockSpec, index_map=const, pipeline_mode=pl.Buffered(1))
    in_specs = [pl.BlockSpec((tm, d), row)]
    args = [x]
    for o, w in mix_pairs:
        in_specs += [pl.BlockSpec((tm, o.shape[1]), row), resident(w.shape)]
        args += [o, w]
    in_specs += [resident((1, d))] * 4
    args += [ln_mix_g.reshape(1, d), ln_mix_b.reshape(1, d), ln_ffn_g.reshape(1, d), ln_ffn_b.reshape(1, d)]
    in_specs += [resident(wg.shape), resident(wu.shape), resident(wd.shape)]
    args += [wg, wu, wd]
    return pl.pallas_call(
        functools.partial(_ffn_kernel, n_mix=len(mix_pairs), alpha=alpha),
        grid=(t // tm,),
        in_specs=in_specs,
        out_specs=pl.BlockSpec((tm, d), row),
        out_shape=jax.ShapeDtypeStruct((t, d), F32),
        compiler_params=_cparams(1),
    )(*args)


def _row_tile(t):
    for tm in (1024, 512, 256, 128):
        if t % tm == 0:
            return tm
    raise ValueError(f"token count {t} is not a multiple of 128")


def kernel(x, w_in_even, gate_bias_even, lb_raw, conv_qk, gn_hgrn, gn_mlstm, w_out_even, w_qkv_odd, rpb_odd,
           w_out_odd, ln_mix_g, ln_mix_b, ln_ffn_g, ln_ffn_b, w_ffn_gate, w_ffn_up, w_ffn_down):
    batch, seq, d_model = x.shape
    depth = ln_mix_g.shape[0]
    alpha = (2.0 * depth) ** 0.25
    t = batch * seq
    tm = _row_tile(t)
    a_width = N_REC_HEADS * HEAD
    main_cols = 9 * a_width
    n_gate = 4 * N_REC_HEADS
    tm_ffn = min(tm, 512)
    tn_main = 1536 if main_cols % 1536 == 0 else 512

    h = x.reshape(t, d_model)
    for layer in range(depth):
        j = layer // 2
        if layer % 2 == 0:
            w_in = w_in_even[j]
            w_main = w_in[:, :main_cols].astype(BF16)
            w_gate = jnp.pad(w_in[:, main_cols:], ((0, 0), (0, V7X_LANES - n_gate))).astype(BF16)
            proj = _matmul(h, w_main, out_dtype=BF16, tm=tm, tn=tn_main)
            gates = _matmul(h, w_gate, out_dtype=F32, tm=tm, tn=V7X_LANES)[:, :n_gate]
            o_a = _hgrn(proj, lb_raw, gn_hgrn[j], batch=batch, seq=seq, layer_j=j)
            h_b = _mlstm(proj, gates, gate_bias_even[j], conv_qk[j], gn_mlstm[j], batch=batch, seq=seq)
            w_out = w_out_even[j].astype(BF16)
            mix_pairs = [(o_a, w_out[:a_width]), (h_b, w_out[a_width:])]
        else:
            qkv = _qkv_projection(h, w_qkv_odd[j].astype(BF16), q_scale=NA_DH ** -0.5 * LOG2_E, tm=tm)
            o = _neighbourhood_attention(qkv, _na_bias_tables(rpb_odd[j]), batch=batch, seq=seq)
            mix_pairs = [(o, w_out_odd[j].astype(BF16))]
        h = _mixer_out_ffn(h, mix_pairs, ln_mix_g[layer], ln_mix_b[layer], ln_ffn_g[layer], ln_ffn_b[layer],
                           w_ffn_gate[layer].astype(BF16), w_ffn_up[layer].astype(BF16),
                           w_ffn_down[layer].astype(BF16), alpha=alpha, tm=tm_ffn)
    return h.reshape(batch, seq, d_model)
```

```python
import functools

import numpy as np
import jax
import jax.numpy as jnp
from jax import lax
from jax.experimental import pallas as pl
from jax.experimental.pallas import tpu as pltpu

F32 = jnp.float32
BF16 = jnp.bfloat16

GRID_W = 64
HEAD = 128
N_REC_HEADS = 4
CHUNK = 64
CONV_W = 5
GATE_GROUP = 8
NA_DH = 32
NA_GROUP = HEAD // NA_DH
WIN_R = 8
WIN_C = 16
LN_EPS = 1e-5
GN_EPS = 1e-6
NEG_BIG = -1e30
LB_FLOOR = 1e-30
LOG2_E = 1.4426950408889634

V7X_LANES = 128
V7X_VMEM_LIMIT_BYTES = 56 * 1024 * 1024

_LEVELS = (32, 16, 8, 4, 2, 1)


def _cparams(n_grid_axes):
    return pltpu.CompilerParams(
        dimension_semantics=("arbitrary",) * n_grid_axes,
        vmem_limit_bytes=V7X_VMEM_LIMIT_BYTES)


def _dot(a, b):
    return jnp.dot(a, b, preferred_element_type=F32)


def _dot_nt(a, b):
    return lax.dot_general(a, b, (((1,), (1,)), ((), ())), preferred_element_type=F32)


def _dot_tn(a, b):
    return lax.dot_general(a, b, (((0,), (0,)), ((), ())), preferred_element_type=F32)


def _split3(x):
    hi = x.astype(BF16)
    r1 = x - hi.astype(F32)
    mid = r1.astype(BF16)
    lo = (r1 - mid.astype(F32)).astype(BF16)
    return hi, mid, lo


def _log_sigmoid(z):
    return jnp.minimum(z, 0.0) - jnp.log1p(jnp.exp(-jnp.abs(z)))


def _sigmoid(z):
    return 1.0 / (1.0 + jnp.exp(-z))


def _layer_norm(t, g, b):
    mu = jnp.mean(t, axis=-1, keepdims=True)
    c = t - mu
    var = jnp.mean(c * c, axis=-1, keepdims=True)
    return c * lax.rsqrt(var + LN_EPS) * g + b


def _resident(shape):
    return pl.BlockSpec(shape, lambda i: (0,) * len(shape), pipeline_mode=pl.Buffered(1))


def _in_proj_kernel(x_ref, w_ref, wg_ref, o_ref, g_ref):
    xb = x_ref[...].astype(BF16)
    o_ref[...] = _dot(xb, w_ref[...]).astype(o_ref.dtype)
    g_ref[...] = _dot(xb, wg_ref[...])


def _in_projection(x, w_main, w_gate, *, tm):
    t, k = x.shape
    assert t % tm == 0
    return pl.pallas_call(
        _in_proj_kernel,
        grid=(t // tm,),
        in_specs=[pl.BlockSpec((tm, k), lambda i: (i, 0)), _resident(w_main.shape), _resident(w_gate.shape)],
        out_specs=[pl.BlockSpec((tm, w_main.shape[1]), lambda i: (i, 0)),
                   pl.BlockSpec((tm, w_gate.shape[1]), lambda i: (i, 0))],
        out_shape=[jax.ShapeDtypeStruct((t, w_main.shape[1]), BF16),
                   jax.ShapeDtypeStruct((t, w_gate.shape[1]), F32)],
        compiler_params=_cparams(1),
    )(x, w_main, w_gate)


def _qkv_kernel(x_ref, w_ref, o_ref, *, q_scale):
    acc = _dot(x_ref[...].astype(BF16), w_ref[...])
    d = acc.shape[1] // 3
    o_ref[:, :d] = (acc[:, :d] * q_scale).astype(o_ref.dtype)
    o_ref[:, d:] = acc[:, d:].astype(o_ref.dtype)


def _qkv_projection(x, w_qkv, *, q_scale, tm):
    t, k = x.shape
    assert t % tm == 0
    return pl.pallas_call(
        functools.partial(_qkv_kernel, q_scale=q_scale),
        grid=(t // tm,),
        in_specs=[pl.BlockSpec((tm, k), lambda i: (i, 0)), _resident(w_qkv.shape)],
        out_specs=pl.BlockSpec((tm, w_qkv.shape[1]), lambda i: (i, 0)),
        out_shape=jax.ShapeDtypeStruct((t, w_qkv.shape[1]), BF16),
        compiler_params=_cparams(1),
    )(x, w_qkv)


def _hgrn_constants():
    L = CHUNK
    t = np.arange(L)
    a_rows, rowsel, masks = [], [], []
    for c in _LEVELS:
        odd = (t // c) % 2 == 1
        rho = (t // (2 * c)) * 2 * c + c - 1
        u = t[None, :]
        a = np.where(odd[:, None], (u > rho[:, None]) & (u <= t[:, None]),
                     (u > t[:, None]) & (u <= rho[:, None]))
        a_rows.append(a.astype(np.float32))
        rowsel.append(np.broadcast_to(odd[:, None], (L, HEAD)).astype(np.float32))
        same = (t[:, None] // (2 * c)) == (t[None, :] // (2 * c))
        masks.append((odd[:, None] & ~odd[None, :] & same).astype(np.float32))
    masks.append(np.eye(L, dtype=np.float32))
    a_rows.append((t[None, :] <= t[:, None]).astype(np.float32))
    a_rows.append((t[None, :] > t[:, None]).astype(np.float32))
    a_f = np.stack(a_rows)
    rs_f = np.stack(rowsel)
    m_f = np.stack(masks)
    a = np.stack([a_f, a_f[:, ::-1, ::-1]]).reshape(2, 8 * L, L)
    a3 = np.concatenate([a, a, a, np.zeros_like(a)], axis=-1)
    rs = np.stack([rs_f, rs_f[:, ::-1]])
    m = np.stack([m_f, m_f[:, ::-1, ::-1]])
    return a3, rs, m


def _hgrn_kernel(q_ref, ff_ref, fb_ref, v_ref, g_ref, lb_ref, gn_ref, a3_ref, rs_ref, mk_ref,
                 o_ref, acc_s, st_s, y_s, sc_s, dec_s, fw_s, inc_s, dec3_s, *, layer_j, nchunk):
    L = CHUNK
    nlev = len(_LEVELS)
    f_refs = (ff_ref, fb_ref)
    Y_QPRE, Y_KSUF, Y_Q, Y_K = nlev, nlev + 1, nlev + 2, nlev + 3

    lbr = lb_ref[...]
    e = jnp.exp(lbr - jnp.max(lbr, axis=1, keepdims=True))
    soft = e / jnp.sum(e, axis=1, keepdims=True)
    cum = soft[:, 0:1, :]
    for i in range(1, layer_j + 1):
        cum = cum + soft[:, i:i + 1, :]
    lb = cum - soft[:, 0:1, :]
    lb_floor = jnp.maximum(lb, LB_FLOOR)
    one_m_lb = 1.0 - lb

    st_s[...] = jnp.zeros_like(st_s)

    def chunk_rows(d, step):
        step = jnp.minimum(step, nchunk - 1)
        c = step if d == 0 else nchunk - 1 - step
        return pl.ds(pl.multiple_of(c * L, L), L)

    def stage1(d, step0):
        qs, ks, parts = [], [], []
        for par in range(2):
            rows = chunk_rows(d, step0 + par)
            qs.append(q_ref[rows, :].astype(F32))
            f = lb_floor[d] + one_m_lb[d] * _sigmoid(f_refs[d][rows, :].astype(F32))
            ks.append(1.0 - f)
            hi, mid, lo = _split3(jnp.log(f) * LOG2_E)
            parts.append(jnp.concatenate([hi, mid, lo, jnp.zeros_like(hi)], axis=0))
        dall = _dot(a3_ref[d], jnp.concatenate(parts, axis=1))
        for slot in range(2):
            q, k = qs[slot], ks[slot]
            eall = jnp.exp2(dall[:, slot * HEAD:(slot + 1) * HEAD])
            for li, c in enumerate(_LEVELS):
                if c % 8 == 0:
                    first_q = 1 if d == 0 else 0
                    x = jnp.concatenate([(q if b % 2 == first_q else k)[b * c:(b + 1) * c] for b in range(L // c)],
                                        axis=0)
                else:
                    x = jnp.where(rs_ref[d, li] > 0.5, q, k)
                y_s[slot, d, li] = (x * eall[li * L:(li + 1) * L]).astype(BF16)
            e_pre = eall[nlev * L:(nlev + 1) * L]
            e_suf = eall[(nlev + 1) * L:(nlev + 2) * L]
            y_s[slot, d, Y_QPRE] = (q * e_pre).astype(BF16)
            y_s[slot, d, Y_KSUF] = (k * e_suf).astype(BF16)
            y_s[slot, d, Y_Q] = q.astype(BF16)
            y_s[slot, d, Y_K] = k.astype(BF16)
            last = L - 1 if d == 0 else 0
            dec_s[slot, d] = jnp.broadcast_to(e_pre[last:last + 1, :], (8, HEAD))

    def stage2(d, step, slot):
        scores = _dot_nt(y_s[slot, d, Y_Q], y_s[slot, d, Y_K]) * mk_ref[d, nlev]
        for li in range(nlev):
            y = y_s[slot, d, li]
            scores = scores + _dot_nt(y, y) * mk_ref[d, li]
        sc_s[slot, d] = scores.astype(BF16)
        fw_s[slot, d] = y_s[slot, d, Y_QPRE]
        inc_s[slot, d] = _dot_tn(v_ref[chunk_rows(d, step), :].astype(BF16), y_s[slot, d, Y_KSUF])
        dec3_s[slot, d] = dec_s[slot, d]

    def stage3(d, step, slot):
        rows = chunk_rows(d, step)
        st = st_s[d]
        o = _dot(sc_s[slot, d], v_ref[rows, :].astype(BF16)) + _dot_nt(fw_s[slot, d], st.astype(BF16))
        st_s[d] = st * dec3_s[slot, d][0:1, :] + inc_s[slot, d]
        acc_s[rows, :] = acc_s[rows, :] + o

    def body(it, carry):
        for par in range(2):
            for d in range(2):
                stage3(d, 2 * it + par, par)
        for par in range(2):
            for d in range(2):
                stage2(d, 2 * it + 2 + par, par)
        for d in range(2):
            stage1(d, 2 * it + 4)
        return carry

    acc_s[...] = jnp.zeros_like(acc_s)
    for d in range(2):
        stage1(d, 0)
    for d in range(2):
        for par in range(2):
            stage2(d, par, par)
    for d in range(2):
        stage1(d, 2)
    lax.fori_loop(0, nchunk // 2, body, 0)

    gn = gn_ref[...]
    blk = 8 * L

    def fin(i, carry):
        rows = pl.ds(pl.multiple_of(i * blk, blk), blk)
        o = acc_s[rows, :]
        g = g_ref[rows, :].astype(F32)
        o = o * lax.rsqrt(jnp.mean(o * o, axis=-1, keepdims=True) + GN_EPS)
        o_ref[rows, :] = (o * gn * (g * _sigmoid(g))).astype(o_ref.dtype)
        return carry

    lax.fori_loop(0, (nchunk * L) // blk, fin, 0)


def _hgrn(proj, lb_raw, gn, *, batch, seq, layer_j):
    nchunk = seq // CHUNK
    assert nchunk % 2 == 0 and seq % (8 * CHUNK) == 0
    a3, rs, mk = _hgrn_constants()
    n_even = lb_raw.shape[1]
    nh = N_REC_HEADS

    def col(kind):
        return pl.BlockSpec((seq, HEAD), lambda b, h: (b, kind * nh + h))

    const3 = lambda b, h: (0, 0, 0)
    const4 = lambda b, h: (0, 0, 0, 0)
    return pl.pallas_call(
        functools.partial(_hgrn_kernel, layer_j=layer_j, nchunk=nchunk),
        grid=(batch, nh),
        in_specs=[col(0), col(1), col(2), col(3), col(4),
                  pl.BlockSpec((2, n_even, HEAD), lambda b, h: (0, 0, h)),
                  pl.BlockSpec((1, HEAD), lambda b, h: (0, h)),
                  pl.BlockSpec(a3.shape, const3),
                  pl.BlockSpec(rs.shape, const4),
                  pl.BlockSpec(mk.shape, const4)],
        out_specs=pl.BlockSpec((seq, HEAD), lambda b, h: (b, h)),
        out_shape=jax.ShapeDtypeStruct((batch * seq, nh * HEAD), BF16),
        scratch_shapes=[pltpu.VMEM((seq, HEAD), F32), pltpu.VMEM((2, HEAD, HEAD), F32),
                        pltpu.VMEM((2, 2, len(_LEVELS) + 4, CHUNK, HEAD), BF16),
                        pltpu.VMEM((2, 2, CHUNK, CHUNK), BF16),
                        pltpu.VMEM((2, 2, 8, HEAD), F32),
                        pltpu.VMEM((2, 2, CHUNK, HEAD), BF16),
                        pltpu.VMEM((2, 2, HEAD, HEAD), F32),
                        pltpu.VMEM((2, 2, 8, HEAD), F32)],
        compiler_params=_cparams(2),
    )(proj, proj, proj, proj, proj, lb_raw, gn.reshape(1, -1),
      jnp.asarray(a3, BF16), jnp.asarray(rs), jnp.asarray(mk))


def _mlstm_constants():
    L = CHUNK
    t = np.arange(L)
    ut = (t[:, None] <= t[None, :]).astype(np.float32)
    cum = np.stack([ut, ut[::-1, ::-1]])
    tril = (t[None, :] <= t[:, None]).astype(np.float32)
    causal = np.stack([tril, tril[::-1, ::-1]])
    sel = np.zeros((HEAD, 4 * GATE_GROUP * HEAD), np.float32)
    for q in range(4):
        for p in range(3):
            for j in range(GATE_GROUP):
                sel[(q * 3 + p) * GATE_GROUP + j, (q * GATE_GROUP + j) * HEAD:(q * GATE_GROUP + j + 1) * HEAD] = 1.0
    return cum, causal, sel


def _mlstm_kernel(xq_ref, xk_ref, v_ref, og_ref, gates_ref, gb_ref, cwq_ref, cwk_ref, gn_ref,
                  cum_ref, cm_ref, sel_ref, o_ref, q_s, k_s, acc_s, b_s, c_s, cma_s,
                  bl_s, gm_s, mst_s, mnew_s, wold_s, qkw_s, inc_s, nd_s, wi_s, winter_s, wsc_s, floor_s,
                  *, nchunk):
    L = CHUNK
    pad = CONV_W // 2
    halo = 16

    def conv_chunk(c, carry):
        rows = pl.ds(pl.multiple_of(c * L, L), L)
        prev = pl.ds(pl.multiple_of(jnp.maximum(c * L - halo, 0), halo), halo)
        nxt = pl.ds(pl.multiple_of(jnp.minimum(c * L + L, nchunk * L - halo), halo), halo)
        has_prev = jnp.where(c > 0, 1.0, 0.0).astype(F32)
        has_next = jnp.where(c < nchunk - 1, 1.0, 0.0).astype(F32)
        for x_ref, w_ref, dst, scale in ((xq_ref, cwq_ref, q_s, 1.0), (xk_ref, cwk_ref, k_s, HEAD ** -0.5)):
            win = jnp.concatenate([x_ref[prev, :].astype(F32) * has_prev, x_ref[rows, :].astype(F32),
                                   x_ref[nxt, :].astype(F32) * has_next], axis=0)
            w = w_ref[...]
            acc = win[halo - pad:halo - pad + L] * w[0:1, :]
            for j in range(1, CONV_W):
                acc = acc + win[halo - pad + j:halo - pad + j + L] * w[j:j + 1, :]
            y = acc * _sigmoid(acc)
            dst[rows, :] = y * scale if scale != 1.0 else y
        return carry

    lax.fori_loop(0, nchunk, conv_chunk, 0)

    for d in range(2):
        lf2 = _log_sigmoid(gates_ref[2 + d] + gb_ref[2 + d])
        hi, mid, lo = _split3(lf2)
        cm = cum_ref[d]
        b2 = _dot(hi, cm) + _dot(mid, cm) + _dot(lo, cm)
        b_s[d] = b2
        last = L - 1 if d == 0 else 0
        bl_s[d] = b2[:, last:last + 1]
        li2 = gates_ref[d] + gb_ref[d]
        gm_s[d] = jnp.max(b2[:, last:last + 1] - b2 + li2, axis=1, keepdims=True)
        x = jnp.concatenate([li2 - b2, jnp.full((nchunk, HEAD - L), NEG_BIG, F32)], axis=1)
        lane_x = lax.broadcasted_iota(jnp.int32, x.shape, 1)
        sh = 1
        while sh < L:
            if d == 0:
                x = jnp.maximum(x, jnp.where(lane_x >= sh, pltpu.roll(x, sh, axis=1), NEG_BIG))
            else:
                x = jnp.maximum(x, pltpu.roll(x, HEAD - sh, axis=1))
            sh *= 2
        cma_s[d] = x[:, :L]

    c_s[...] = jnp.zeros_like(c_s)
    lane = lax.broadcasted_iota(jnp.int32, (L, HEAD), 1)
    ones_col = (lane == 0).astype(BF16)

    def stab_step(step, m):
        new = []
        for d in range(2):
            c = step if d == 0 else nchunk - 1 - step
            sl = pl.ds(c, 1)
            mst_s[d, sl, :] = m[d]
            m_new = jnp.maximum(bl_s[d, sl, :] + m[d], gm_s[d, sl, :])
            mnew_s[d, sl, :] = m_new
            new.append(m_new)
        return tuple(new)

    lax.fori_loop(0, nchunk, stab_step, (jnp.zeros((1, 1), F32), jnp.zeros((1, 1), F32)))

    G = GATE_GROUP
    causal = [cm_ref[d] > 0.5 for d in range(2)]

    def gate_weights(g, carry):
        sl = pl.ds(pl.multiple_of(g * G, G), G)
        for d in range(2):
            br = b_s[d, sl, :]
            li = gates_ref[d, sl, :] + gb_ref[d]
            a = li - br
            m_st = mst_s[d, sl, :]
            m_new = mnew_s[d, sl, :]
            last = L - 1 if d == 0 else 0
            b_last = br[:, last:last + 1]
            mx = jnp.maximum(m_st, cma_s[d, sl, :])
            quantities = (mx, jnp.exp(m_st - mx), jnp.exp(-(br + mx)), jnp.exp(b_last - br + li - m_new))
            parts = [p.astype(F32) for qty in quantities for p in _split3(qty)]
            parts.append(jnp.zeros((HEAD - len(parts) * G, L), F32))
            cols = _dot_tn(jnp.concatenate(parts, axis=0).astype(BF16), sel_ref[...])
            for j in range(G):
                rows = pl.ds(pl.multiple_of((g * G + j) * L, L), L)
                mx_col, winter, floor, wsc = (cols[:, (q * G + j) * HEAD:(q * G + j + 1) * HEAD] for q in range(4))
                am = jnp.where(causal[d], jnp.broadcast_to(a[j:j + 1], (L, L)), NEG_BIG)
                wi_s[d, rows, :] = jnp.exp(am - mx_col[:, :L])
                winter_s[d, rows, :] = winter
                floor_s[d, rows, :] = floor[:, 0:1]
                wsc_s[d, rows, :] = wsc
            wold_s[d, sl, :] = jnp.exp(b_last + m_st - m_new)
        return carry

    lax.fori_loop(0, nchunk // G, gate_weights, 0)

    def chunk_index(d, step):
        step = jnp.minimum(step, nchunk - 1)
        return step if d == 0 else nchunk - 1 - step

    def chunk_rows(d, step):
        return pl.ds(pl.multiple_of(chunk_index(d, step) * L, L), L)

    def value_aug(rows):
        return jnp.concatenate([v_ref[rows, :], ones_col], axis=1)

    def stage_a(d, step, slot):
        rows = chunk_rows(d, step)
        k = k_s[rows, :]
        qkw_s[slot, d] = (_dot_nt(q_s[rows, :].astype(BF16), k.astype(BF16)) * wi_s[d, rows, :]).astype(BF16)
        inc_s[slot, d] = _dot_tn((k * wsc_s[d, rows, :]).astype(BF16), value_aug(rows))

    def stage_b(d, step, slot):
        rows = chunk_rows(d, step)
        cst = c_s[d]
        w_inter = winter_s[d, rows, :]
        nd_s[slot, d] = (_dot(qkw_s[slot, d], value_aug(rows))
                         + jnp.concatenate([w_inter, w_inter], axis=1)
                         * _dot(q_s[rows, :].astype(BF16), cst.astype(BF16)))
        c_s[d] = wold_s[d, pl.ds(chunk_index(d, step), 1), :] * cst + inc_s[slot, d]

    def stage_c(d, step, slot):
        rows = chunk_rows(d, step)
        nd = nd_s[slot, d]
        h = nd[:, :HEAD] / jnp.maximum(jnp.abs(nd[:, HEAD:HEAD + 1]), floor_s[d, rows, :])
        acc_s[rows, :] = acc_s[rows, :] + h

    def body(it, with_c):
        for par in range(2):
            for d in range(2):
                if with_c:
                    stage_c(d, 2 * it - 2 + par, par)
        for par in range(2):
            for d in range(2):
                stage_b(d, 2 * it + par, par)
        for par in range(2):
            for d in range(2):
                stage_a(d, 2 * it + 2 + par, par)

    def loop_body(it, carry):
        body(it, True)
        return carry

    acc_s[...] = jnp.zeros_like(acc_s)
    for d in range(2):
        for par in range(2):
            stage_a(d, par, par)
    body(0, False)
    lax.fori_loop(1, nchunk // 2, loop_body, 0)
    for d in range(2):
        for par in range(2):
            stage_c(d, nchunk - 2 + par, par)

    gn = gn_ref[...]
    blk = 8 * L

    def fin(i, carry):
        rows = pl.ds(pl.multiple_of(i * blk, blk), blk)
        h = acc_s[rows, :]
        mu = jnp.mean(h, axis=-1, keepdims=True)
        cen = h - mu
        hn = cen * lax.rsqrt(jnp.mean(cen * cen, axis=-1, keepdims=True) + GN_EPS)
        o_ref[rows, :] = (hn * gn * _sigmoid(og_ref[rows, :].astype(F32))).astype(o_ref.dtype)
        return carry

    lax.fori_loop(0, (nchunk * L) // blk, fin, 0)


def _mlstm(proj, gates, gate_bias, conv_w, gn, *, batch, seq):
    nchunk = seq // CHUNK
    assert nchunk % GATE_GROUP == 0 and nchunk % 2 == 0 and seq % (8 * CHUNK) == 0
    nh = N_REC_HEADS
    width = nh * HEAD
    cum, causal, sel = _mlstm_constants()
    g5 = gates.reshape(batch, nchunk, CHUNK, 4, nh).transpose(0, 4, 3, 1, 2)
    gb = jnp.broadcast_to(gate_bias.reshape(4, nh).T[:, :, None, None], (nh, 4, 1, CHUNK))
    cw = jnp.pad(conv_w, ((0, 8 - CONV_W), (0, 0)))
    first_col = 5 * nh

    def col(kind):
        return pl.BlockSpec((seq, HEAD), lambda b, h: (b, first_col + kind * nh + h))

    const2 = lambda b, h: (0, 0)
    const3 = lambda b, h: (0, 0, 0)
    return pl.pallas_call(
        functools.partial(_mlstm_kernel, nchunk=nchunk),
        grid=(batch, nh),
        in_specs=[col(0), col(1), col(2), col(3),
                  pl.BlockSpec((None, None, 4, nchunk, CHUNK), lambda b, h: (b, h, 0, 0, 0)),
                  pl.BlockSpec((None, 4, 1, CHUNK), lambda b, h: (h, 0, 0, 0)),
                  pl.BlockSpec((8, HEAD), lambda b, h: (0, h)),
                  pl.BlockSpec((8, HEAD), lambda b, h: (0, nh + h)),
                  pl.BlockSpec((1, HEAD), lambda b, h: (0, h)),
                  pl.BlockSpec(cum.shape, const3),
                  pl.BlockSpec(causal.shape, const3),
                  pl.BlockSpec(sel.shape, const2)],
        out_specs=pl.BlockSpec((seq, HEAD), lambda b, h: (b, h)),
        out_shape=jax.ShapeDtypeStruct((batch * seq, width), BF16),
        scratch_shapes=[pltpu.VMEM((seq, HEAD), F32), pltpu.VMEM((seq, HEAD), F32),
                        pltpu.VMEM((seq, HEAD), F32), pltpu.VMEM((2, nchunk, CHUNK), F32),
                        pltpu.VMEM((2, HEAD, 2 * HEAD), F32), pltpu.VMEM((2, nchunk, CHUNK), F32)]
                       + [pltpu.VMEM((2, nchunk, 1), F32)] * 5
                       + [pltpu.VMEM((2, 2, CHUNK, CHUNK), BF16), pltpu.VMEM((2, 2, HEAD, 2 * HEAD), F32),
                          pltpu.VMEM((2, 2, CHUNK, 2 * HEAD), F32),
                          pltpu.VMEM((2, seq, CHUNK), F32), pltpu.VMEM((2, seq, HEAD), F32),
                          pltpu.VMEM((2, seq, HEAD), F32), pltpu.VMEM((2, seq, 1), F32)],
        compiler_params=_cparams(2),
    )(proj, proj, proj, proj, g5, gb, cw, cw, gn.reshape(1, -1),
      jnp.asarray(cum, BF16), jnp.asarray(causal), jnp.asarray(sel, BF16))


def _na_bias_tables(rpb):
    w = GRID_W
    qc = np.arange(w)[:, None]
    kc = np.arange(w)[None, :]
    c0 = np.clip(qc - WIN_C // 2, 0, w - WIN_C)
    valid = (kc >= c0) & (kc < c0 + WIN_C)
    cidx = np.clip(kc - qc + WIN_C - 1, 0, 2 * WIN_C - 2)
    tiles = jnp.where(jnp.asarray(valid)[None, None], rpb.astype(F32)[:, :, cidx] * LOG2_E, NEG_BIG)
    return jnp.concatenate([tiles[:, :-1], tiles[:, 1:]], axis=-1)


def _na_kernel(q_ref, k_ref, v_ref, bias_ref, o_ref, s_even, s_odd, *, n_rows):
    w = GRID_W
    nk = WIN_R * w
    lane_head = lax.broadcasted_iota(jnp.int32, (w, HEAD), 1) // NA_DH
    head_masks = [lane_head == hh for hh in range(NA_GROUP)]

    def window_start(r):
        return jnp.clip(r - WIN_R // 2, 0, n_rows - WIN_R)

    def logits(r, dst):
        q = q_ref[pl.ds(pl.multiple_of(r * w, w), w), :]
        zero = jnp.zeros_like(q)
        qm = jnp.concatenate([jnp.where(head_masks[hh], q, zero) for hh in range(NA_GROUP)], axis=0)
        r0 = window_start(r)
        s = _dot_nt(qm, k_ref[pl.ds(pl.multiple_of(r0 * w, w), nk), :])
        first = WIN_R - 1 - (r - r0)
        dst[...] = s + jnp.concatenate(
            [jnp.concatenate([bias_ref[hh, first + 2 * m] for m in range(WIN_R // 2)], axis=1)
             for hh in range(NA_GROUP)], axis=0)

    def attend(r, src):
        s = src[...]
        pexp = jnp.exp2(s - jnp.max(s, axis=1, keepdims=True))
        l = jnp.sum(pexp, axis=1, keepdims=True)
        r0 = window_start(r)
        o = _dot(pexp.astype(BF16), v_ref[pl.ds(pl.multiple_of(r0 * w, w), nk), :]) / l
        out = jnp.where(head_masks[0], o[0:w], 0.0)
        for hh in range(1, NA_GROUP):
            out = out + jnp.where(head_masks[hh], o[hh * w:(hh + 1) * w], 0.0)
        o_ref[pl.ds(pl.multiple_of(r * w, w), w), :] = out.astype(o_ref.dtype)

    logits(0, s_even)

    def two_rows(k, carry):
        r = 2 * k
        logits(r + 1, s_odd)
        attend(r, s_even)
        logits(jnp.minimum(r + 2, n_rows - 1), s_even)
        attend(r + 1, s_odd)
        return carry

    lax.fori_loop(0, n_rows // 2, two_rows, 0)


def _neighbourhood_attention(qkv, bias, *, batch, seq):
    n_rows = seq // GRID_W
    assert n_rows % 2 == 0 and n_rows >= WIN_R
    d_model = qkv.shape[1] // 3
    n_groups = d_model // HEAD
    return pl.pallas_call(
        functools.partial(_na_kernel, n_rows=n_rows),
        grid=(batch, n_groups),
        in_specs=[pl.BlockSpec((seq, HEAD), lambda b, g: (b, g)),
                  pl.BlockSpec((seq, HEAD), lambda b, g: (b, n_groups + g)),
                  pl.BlockSpec((seq, HEAD), lambda b, g: (b, 2 * n_groups + g)),
                  pl.BlockSpec((NA_GROUP, 2 * WIN_R - 2, GRID_W, 2 * GRID_W), lambda b, g: (g, 0, 0, 0))],
        out_specs=pl.BlockSpec((seq, HEAD), lambda b, g: (b, g)),
        out_shape=jax.ShapeDtypeStruct((batch * seq, d_model), BF16),
        scratch_shapes=[pltpu.VMEM((NA_GROUP * GRID_W, WIN_R * GRID_W), F32)] * 2,
        compiler_params=_cparams(2),
    )(qkv, qkv, qkv, bias)


def _ffn_kernel(*refs, n_mix, alpha):
    x_ref = refs[0]
    mix_refs = refs[1:1 + 2 * n_mix]
    lmg_ref, lmb_ref, lfg_ref, lfb_ref, wg_ref, wu_ref, wd_ref, o_ref = refs[1 + 2 * n_mix:]
    half = x_ref.shape[0] // 2
    rows = (slice(0, half), slice(half, 2 * half))

    def pre(r):
        mix = _dot(mix_refs[0][r, :], mix_refs[1][...])
        for i in range(1, n_mix):
            mix = mix + _dot(mix_refs[2 * i][r, :], mix_refs[2 * i + 1][...])
        return _layer_norm(alpha * x_ref[r, :] + mix, lmg_ref[...], lmb_ref[...])

    def act(h):
        hb = h.astype(BF16)
        g = _dot(hb, wg_ref[...])
        u = _dot(hb, wu_ref[...])
        return (g * _sigmoid(g) * u).astype(BF16)

    h0 = pre(rows[0])
    a0 = act(h0)
    h1 = pre(rows[1])
    y0 = _dot(a0, wd_ref[...])
    a1 = act(h1)
    o_ref[rows[0], :] = _layer_norm(alpha * h0 + y0, lfg_ref[...], lfb_ref[...])
    y1 = _dot(a1, wd_ref[...])
    o_ref[rows[1], :] = _layer_norm(alpha * h1 + y1, lfg_ref[...], lfb_ref[...])


def _mixer_out_ffn(x, mix_pairs, ln_mix_g, ln_mix_b, ln_ffn_g, ln_ffn_b, wg, wu, wd, *, alpha, tm):
    t, d = x.shape
    assert t % tm == 0
    row = lambda i: (i, 0)
    in_specs = [pl.BlockSpec((tm, d), row)]
    args = [x]
    for o, w in mix_pairs:
        in_specs += [pl.BlockSpec((tm, o.shape[1]), row), _resident(w.shape)]
        args += [o, w]
    in_specs += [_resident((1, d))] * 4
    args += [ln_mix_g.reshape(1, d), ln_mix_b.reshape(1, d), ln_ffn_g.reshape(1, d), ln_ffn_b.reshape(1, d)]
    in_specs += [_resident(wg.shape), _resident(wu.shape), _resident(wd.shape)]
    args += [wg, wu, wd]
    return pl.pallas_call(
        functools.partial(_ffn_kernel, n_mix=len(mix_pairs), alpha=alpha),
        grid=(t // tm,),
        in_specs=in_specs,
        out_specs=pl.BlockSpec((tm, d), row),
        out_shape=jax.ShapeDtypeStruct((t, d), F32),
        compiler_params=_cparams(1),
    )(*args)


def _row_tile(t):
    for tm in (1024, 512, 256, 128):
        if t % tm == 0:
            return tm
    raise ValueError(f"token count {t} is not a multiple of 128")


def kernel(x, w_in_even, gate_bias_even, lb_raw, conv_qk, gn_hgrn, gn_mlstm, w_out_even, w_qkv_odd, rpb_odd,
           w_out_odd, ln_mix_g, ln_mix_b, ln_ffn_g, ln_ffn_b, w_ffn_gate, w_ffn_up, w_ffn_down):
    batch, seq, d_model = x.shape
    depth = ln_mix_g.shape[0]
    alpha = (2.0 * depth) ** 0.25
    t = batch * seq
    tm = min(_row_tile(t), 512)
    a_width = N_REC_HEADS * HEAD
    main_cols = 9 * a_width
    n_gate = 4 * N_REC_HEADS

    h = x.reshape(t, d_model)
    for layer in range(depth):
        j = layer // 2
        if layer % 2 == 0:
            w_in = w_in_even[j]
            w_main = w_in[:, :main_cols].astype(BF16)
            w_gate = jnp.pad(w_in[:, main_cols:], ((0, 0), (0, V7X_LANES - n_gate))).astype(BF16)
            proj, gates = _in_projection(h, w_main, w_gate, tm=tm)
            gates = gates[:, :n_gate]
            o_a = _hgrn(proj, lb_raw, gn_hgrn[j], batch=batch, seq=seq, layer_j=j)
            h_b = _mlstm(proj, gates, gate_bias_even[j], conv_qk[j], gn_mlstm[j], batch=batch, seq=seq)
            w_out = w_out_even[j].astype(BF16)
            mix_pairs = [(o_a, w_out[:a_width]), (h_b, w_out[a_width:])]
        else:
            qkv = _qkv_projection(h, w_qkv_odd[j].astype(BF16), q_scale=NA_DH ** -0.5 * LOG2_E, tm=tm)
            o = _neighbourhood_attention(qkv, _na_bias_tables(rpb_odd[j]), batch=batch, seq=seq)
            mix_pairs = [(o, w_out_odd[j].astype(BF16))]
        h = _mixer_out_ffn(h, mix_pairs, ln_mix_g[layer], ln_mix_b[layer], ln_ffn_g[layer], ln_ffn_b[layer],
                           w_ffn_gate[layer].astype(BF16), w_ffn_up[layer].astype(BF16),
                           w_ffn_down[layer].astype(BF16), alpha=alpha, tm=tm)
    return h.reshape(batch, seq, d_model)
```

```python
import functools

import numpy as np
import jax
import jax.numpy as jnp
from jax import lax
from jax.experimental import pallas as pl
from jax.experimental.pallas import tpu as pltpu

F32 = jnp.float32
BF16 = jnp.bfloat16

GRID_W = 64
HEAD = 128
N_REC_HEADS = 4
CHUNK = 64
CONV_W = 5
HGRN_STEPS_PER_TRIP = 4
MLSTM_STEPS_PER_TRIP = 4
GATE_GROUP = 8
NA_DH = 32
NA_GROUP = HEAD // NA_DH
WIN_R = 8
WIN_C = 16
LN_EPS = 1e-5
GN_EPS = 1e-6
NEG_BIG = -1e30
LB_FLOOR = 1e-30
LOG2_E = 1.4426950408889634

V7X_LANES = 128
V7X_VMEM_LIMIT_BYTES = 56 * 1024 * 1024

_LEVELS = (32, 16, 8, 4, 2, 1)


def _cparams(n_grid_axes):
    return pltpu.CompilerParams(
        dimension_semantics=("arbitrary",) * n_grid_axes,
        vmem_limit_bytes=V7X_VMEM_LIMIT_BYTES)


def _dot(a, b):
    return jnp.dot(a, b, preferred_element_type=F32)


def _dot_nt(a, b):
    return lax.dot_general(a, b, (((1,), (1,)), ((), ())), preferred_element_type=F32)


def _dot_tn(a, b):
    return lax.dot_general(a, b, (((0,), (0,)), ((), ())), preferred_element_type=F32)


def _split3(x):
    hi = x.astype(BF16)
    r1 = x - hi.astype(F32)
    mid = r1.astype(BF16)
    lo = (r1 - mid.astype(F32)).astype(BF16)
    return hi, mid, lo


def _log_sigmoid(z):
    return jnp.minimum(z, 0.0) - jnp.log1p(jnp.exp(-jnp.abs(z)))


def _sigmoid(z):
    return 1.0 / (1.0 + jnp.exp(-z))


def _layer_norm(t, g, b):
    mu = jnp.mean(t, axis=-1, keepdims=True)
    c = t - mu
    var = jnp.mean(c * c, axis=-1, keepdims=True)
    return c * lax.rsqrt(var + LN_EPS) * g + b


def _resident(shape):
    return pl.BlockSpec(shape, lambda i: (0,) * len(shape), pipeline_mode=pl.Buffered(1))


def _in_proj_kernel(x_ref, w_ref, wg_ref, o_ref, g_ref):
    xb = x_ref[...].astype(BF16)
    o_ref[...] = _dot(xb, w_ref[...]).astype(o_ref.dtype)
    g_ref[...] = _dot(xb, wg_ref[...])


def _in_projection(x, w_main, w_gate, *, tm):
    t, k = x.shape
    assert t % tm == 0
    return pl.pallas_call(
        _in_proj_kernel,
        grid=(t // tm,),
        in_specs=[pl.BlockSpec((tm, k), lambda i: (i, 0)), _resident(w_main.shape), _resident(w_gate.shape)],
        out_specs=[pl.BlockSpec((tm, w_main.shape[1]), lambda i: (i, 0)),
                   pl.BlockSpec((tm, w_gate.shape[1]), lambda i: (i, 0))],
        out_shape=[jax.ShapeDtypeStruct((t, w_main.shape[1]), BF16),
                   jax.ShapeDtypeStruct((t, w_gate.shape[1]), F32)],
        compiler_params=_cparams(1),
    )(x, w_main, w_gate)


def _qkv_kernel(x_ref, w_ref, o_ref, *, q_scale):
    acc = _dot(x_ref[...].astype(BF16), w_ref[...])
    d = acc.shape[1] // 3
    o_ref[:, :d] = (acc[:, :d] * q_scale).astype(o_ref.dtype)
    o_ref[:, d:] = acc[:, d:].astype(o_ref.dtype)


def _qkv_projection(x, w_qkv, *, q_scale, tm):
    t, k = x.shape
    assert t % tm == 0
    return pl.pallas_call(
        functools.partial(_qkv_kernel, q_scale=q_scale),
        grid=(t // tm,),
        in_specs=[pl.BlockSpec((tm, k), lambda i: (i, 0)), _resident(w_qkv.shape)],
        out_specs=pl.BlockSpec((tm, w_qkv.shape[1]), lambda i: (i, 0)),
        out_shape=jax.ShapeDtypeStruct((t, w_qkv.shape[1]), BF16),
        compiler_params=_cparams(1),
    )(x, w_qkv)


def _hgrn_constants():
    L = CHUNK
    t = np.arange(L)
    a_rows, rowsel, masks = [], [], []
    for c in _LEVELS:
        odd = (t // c) % 2 == 1
        rho = (t // (2 * c)) * 2 * c + c - 1
        u = t[None, :]
        a = np.where(odd[:, None], (u > rho[:, None]) & (u <= t[:, None]),
                     (u > t[:, None]) & (u <= rho[:, None]))
        a_rows.append(a.astype(np.float32))
        rowsel.append(np.broadcast_to(odd[:, None], (L, HEAD)).astype(np.float32))
        same = (t[:, None] // (2 * c)) == (t[None, :] // (2 * c))
        masks.append((odd[:, None] & ~odd[None, :] & same).astype(np.float32))
    masks.append(np.eye(L, dtype=np.float32))
    a_rows.append((t[None, :] <= t[:, None]).astype(np.float32))
    a_rows.append((t[None, :] > t[:, None]).astype(np.float32))
    a_f = np.stack(a_rows)
    rs_f = np.stack(rowsel)
    m_f = np.stack(masks)
    a = np.stack([a_f, a_f[:, ::-1, ::-1]]).reshape(2, 8 * L, L)
    a3 = np.concatenate([a, a, a, np.zeros_like(a)], axis=-1)
    rs = np.stack([rs_f, rs_f[:, ::-1]])
    m = np.stack([m_f, m_f[:, ::-1, ::-1]])
    return a3, rs, m


def _hgrn_kernel(q_ref, ff_ref, fb_ref, v_ref, g_ref, lb_ref, gn_ref, a3_ref, rs_ref, mk_ref,
                 o_ref, acc_s, st_s, y_s, sc_s, dec_s, fw_s, inc_s, dec3_s, *, layer_j, nchunk):
    L = CHUNK
    nlev = len(_LEVELS)
    f_refs = (ff_ref, fb_ref)
    Y_QPRE, Y_KSUF, Y_Q, Y_K = nlev, nlev + 1, nlev + 2, nlev + 3
    ns = HGRN_STEPS_PER_TRIP

    lbr = lb_ref[...]
    e = jnp.exp(lbr - jnp.max(lbr, axis=1, keepdims=True))
    soft = e / jnp.sum(e, axis=1, keepdims=True)
    cum = soft[:, 0:1, :]
    for i in range(1, layer_j + 1):
        cum = cum + soft[:, i:i + 1, :]
    lb = cum - soft[:, 0:1, :]
    lb_floor = jnp.maximum(lb, LB_FLOOR)
    one_m_lb = 1.0 - lb

    st_s[...] = jnp.zeros_like(st_s)

    def chunk_rows(d, step):
        step = jnp.minimum(step, nchunk - 1)
        c = step if d == 0 else nchunk - 1 - step
        return pl.ds(pl.multiple_of(c * L, L), L)

    def stage1(d, step0, slot0):
        qs, ks, parts = [], [], []
        for par in range(2):
            rows = chunk_rows(d, step0 + par)
            qs.append(q_ref[rows, :].astype(F32))
            f = lb_floor[d] + one_m_lb[d] * _sigmoid(f_refs[d][rows, :].astype(F32))
            ks.append(1.0 - f)
            hi, mid, lo = _split3(jnp.log(f) * LOG2_E)
            parts.append(jnp.concatenate([hi, mid, lo, jnp.zeros_like(hi)], axis=0))
        dall = _dot(a3_ref[d], jnp.concatenate(parts, axis=1))
        for half in range(2):
            slot = slot0 + half
            q, k = qs[half], ks[half]
            eall = jnp.exp2(dall[:, half * HEAD:(half + 1) * HEAD])
            for li, c in enumerate(_LEVELS):
                if c % 8 == 0:
                    first_q = 1 if d == 0 else 0
                    x = jnp.concatenate([(q if b % 2 == first_q else k)[b * c:(b + 1) * c] for b in range(L // c)],
                                        axis=0)
                else:
                    x = jnp.where(rs_ref[d, li] > 0.5, q, k)
                y_s[slot, d, li] = (x * eall[li * L:(li + 1) * L]).astype(BF16)
            e_pre = eall[nlev * L:(nlev + 1) * L]
            e_suf = eall[(nlev + 1) * L:(nlev + 2) * L]
            y_s[slot, d, Y_QPRE] = (q * e_pre).astype(BF16)
            y_s[slot, d, Y_KSUF] = (k * e_suf).astype(BF16)
            y_s[slot, d, Y_Q] = q.astype(BF16)
            y_s[slot, d, Y_K] = k.astype(BF16)
            last = L - 1 if d == 0 else 0
            dec_s[slot, d] = jnp.broadcast_to(e_pre[last:last + 1, :], (8, HEAD))

    def stage2(d, step, slot):
        scores = _dot_nt(y_s[slot, d, Y_Q], y_s[slot, d, Y_K]) * mk_ref[d, nlev]
        for li in range(nlev):
            y = y_s[slot, d, li]
            scores = scores + _dot_nt(y, y) * mk_ref[d, li]
        sc_s[slot, d] = scores.astype(BF16)
        fw_s[slot, d] = y_s[slot, d, Y_QPRE]
        inc_s[slot, d] = _dot_tn(v_ref[chunk_rows(d, step), :].astype(BF16), y_s[slot, d, Y_KSUF])
        dec3_s[slot, d] = dec_s[slot, d]

    def stage3(d, step, slot):
        rows = chunk_rows(d, step)
        st = st_s[d]
        o = _dot(sc_s[slot, d], v_ref[rows, :].astype(BF16)) + _dot_nt(fw_s[slot, d], st.astype(BF16))
        st_s[d] = st * dec3_s[slot, d][0:1, :] + inc_s[slot, d]
        acc_s[rows, :] = acc_s[rows, :] + o

    def body(it, carry):
        for par in range(ns):
            for d in range(2):
                stage3(d, ns * it + par, par)
        for par in range(ns):
            for d in range(2):
                stage2(d, ns * it + ns + par, par)
        for par in range(0, ns, 2):
            for d in range(2):
                stage1(d, ns * it + 2 * ns + par, par)
        return carry

    acc_s[...] = jnp.zeros_like(acc_s)
    for par in range(0, ns, 2):
        for d in range(2):
            stage1(d, par, par)
    for par in range(ns):
        for d in range(2):
            stage2(d, par, par)
    for par in range(0, ns, 2):
        for d in range(2):
            stage1(d, ns + par, par)
    lax.fori_loop(0, nchunk // ns, body, 0)

    gn = gn_ref[...]
    blk = 8 * L

    def fin(i, carry):
        rows = pl.ds(pl.multiple_of(i * blk, blk), blk)
        o = acc_s[rows, :]
        g = g_ref[rows, :].astype(F32)
        o = o * lax.rsqrt(jnp.mean(o * o, axis=-1, keepdims=True) + GN_EPS)
        o_ref[rows, :] = (o * gn * (g * _sigmoid(g))).astype(o_ref.dtype)
        return carry

    lax.fori_loop(0, (nchunk * L) // blk, fin, 0)


def _hgrn(proj, lb_raw, gn, *, batch, seq, layer_j):
    nchunk = seq // CHUNK
    ns = HGRN_STEPS_PER_TRIP
    assert nchunk % ns == 0 and seq % (8 * CHUNK) == 0
    a3, rs, mk = _hgrn_constants()
    n_even = lb_raw.shape[1]
    nh = N_REC_HEADS

    def col(kind):
        return pl.BlockSpec((seq, HEAD), lambda b, h: (b, kind * nh + h))

    const3 = lambda b, h: (0, 0, 0)
    const4 = lambda b, h: (0, 0, 0, 0)
    return pl.pallas_call(
        functools.partial(_hgrn_kernel, layer_j=layer_j, nchunk=nchunk),
        grid=(batch, nh),
        in_specs=[col(0), col(1), col(2), col(3), col(4),
                  pl.BlockSpec((2, n_even, HEAD), lambda b, h: (0, 0, h)),
                  pl.BlockSpec((1, HEAD), lambda b, h: (0, h)),
                  pl.BlockSpec(a3.shape, const3),
                  pl.BlockSpec(rs.shape, const4),
                  pl.BlockSpec(mk.shape, const4)],
        out_specs=pl.BlockSpec((seq, HEAD), lambda b, h: (b, h)),
        out_shape=jax.ShapeDtypeStruct((batch * seq, nh * HEAD), BF16),
        scratch_shapes=[pltpu.VMEM((seq, HEAD), F32), pltpu.VMEM((2, HEAD, HEAD), F32),
                        pltpu.VMEM((ns, 2, len(_LEVELS) + 4, CHUNK, HEAD), BF16),
                        pltpu.VMEM((ns, 2, CHUNK, CHUNK), BF16),
                        pltpu.VMEM((ns, 2, 8, HEAD), F32),
                        pltpu.VMEM((ns, 2, CHUNK, HEAD), BF16),
                        pltpu.VMEM((ns, 2, HEAD, HEAD), F32),
                        pltpu.VMEM((ns, 2, 8, HEAD), F32)],
        compiler_params=_cparams(2),
    )(proj, proj, proj, proj, proj, lb_raw, gn.reshape(1, -1),
      jnp.asarray(a3, BF16), jnp.asarray(rs), jnp.asarray(mk))


def _mlstm_constants():
    L = CHUNK
    t = np.arange(L)
    ut = (t[:, None] <= t[None, :]).astype(np.float32)
    cum = np.stack([ut, ut[::-1, ::-1]])
    tril = (t[None, :] <= t[:, None]).astype(np.float32)
    causal = np.stack([tril, tril[::-1, ::-1]])
    sel = np.zeros((HEAD, 4 * GATE_GROUP * HEAD), np.float32)
    for q in range(4):
        for p in range(3):
            for j in range(GATE_GROUP):
                sel[(q * 3 + p) * GATE_GROUP + j, (q * GATE_GROUP + j) * HEAD:(q * GATE_GROUP + j + 1) * HEAD] = 1.0
    return cum, causal, sel


def _mlstm_kernel(xq_ref, xk_ref, v_ref, og_ref, gates_ref, gb_ref, cwq_ref, cwk_ref, gn_ref,
                  cum_ref, cm_ref, sel_ref, o_ref, q_s, k_s, acc_s, b_s, c_s, cma_s,
                  bl_s, gm_s, mst_s, mnew_s, wold_s, qkw_s, inc_s, nd_s, wi_s, winter_s, wsc_s, floor_s,
                  *, nchunk):
    L = CHUNK
    pad = CONV_W // 2
    halo = 16

    def conv_chunk(c, carry):
        rows = pl.ds(pl.multiple_of(c * L, L), L)
        prev = pl.ds(pl.multiple_of(jnp.maximum(c * L - halo, 0), halo), halo)
        nxt = pl.ds(pl.multiple_of(jnp.minimum(c * L + L, nchunk * L - halo), halo), halo)
        has_prev = jnp.where(c > 0, 1.0, 0.0).astype(F32)
        has_next = jnp.where(c < nchunk - 1, 1.0, 0.0).astype(F32)
        for x_ref, w_ref, dst, scale in ((xq_ref, cwq_ref, q_s, 1.0), (xk_ref, cwk_ref, k_s, HEAD ** -0.5)):
            win = jnp.concatenate([x_ref[prev, :].astype(F32) * has_prev, x_ref[rows, :].astype(F32),
                                   x_ref[nxt, :].astype(F32) * has_next], axis=0)
            w = w_ref[...]
            acc = win[halo - pad:halo - pad + L] * w[0:1, :]
            for j in range(1, CONV_W):
                acc = acc + win[halo - pad + j:halo - pad + j + L] * w[j:j + 1, :]
            y = acc * _sigmoid(acc)
            dst[rows, :] = y * scale if scale != 1.0 else y
        return carry

    lax.fori_loop(0, nchunk, conv_chunk, 0)

    for d in range(2):
        lf2 = _log_sigmoid(gates_ref[2 + d] + gb_ref[2 + d])
        hi, mid, lo = _split3(lf2)
        cm = cum_ref[d]
        b2 = _dot(hi, cm) + _dot(mid, cm) + _dot(lo, cm)
        b_s[d] = b2
        last = L - 1 if d == 0 else 0
        bl_s[d] = b2[:, last:last + 1]
        li2 = gates_ref[d] + gb_ref[d]
        gm_s[d] = jnp.max(b2[:, last:last + 1] - b2 + li2, axis=1, keepdims=True)
        x = jnp.concatenate([li2 - b2, jnp.full((nchunk, HEAD - L), NEG_BIG, F32)], axis=1)
        lane_x = lax.broadcasted_iota(jnp.int32, x.shape, 1)
        sh = 1
        while sh < L:
            if d == 0:
                x = jnp.maximum(x, jnp.where(lane_x >= sh, pltpu.roll(x, sh, axis=1), NEG_BIG))
            else:
                x = jnp.maximum(x, pltpu.roll(x, HEAD - sh, axis=1))
            sh *= 2
        cma_s[d] = x[:, :L]

    c_s[...] = jnp.zeros_like(c_s)
    lane = lax.broadcasted_iota(jnp.int32, (L, HEAD), 1)
    ones_col = (lane == 0).astype(BF16)

    def stab_step(step, m):
        new = []
        for d in range(2):
            c = step if d == 0 else nchunk - 1 - step
            sl = pl.ds(c, 1)
            mst_s[d, sl, :] = m[d]
            m_new = jnp.maximum(bl_s[d, sl, :] + m[d], gm_s[d, sl, :])
            mnew_s[d, sl, :] = m_new
            new.append(m_new)
        return tuple(new)

    lax.fori_loop(0, nchunk, stab_step, (jnp.zeros((1, 1), F32), jnp.zeros((1, 1), F32)))

    G = GATE_GROUP
    causal = [cm_ref[d] > 0.5 for d in range(2)]

    def gate_weights(g, carry):
        sl = pl.ds(pl.multiple_of(g * G, G), G)
        for d in range(2):
            br = b_s[d, sl, :]
            li = gates_ref[d, sl, :] + gb_ref[d]
            a = li - br
            m_st = mst_s[d, sl, :]
            m_new = mnew_s[d, sl, :]
            last = L - 1 if d == 0 else 0
            b_last = br[:, last:last + 1]
            mx = jnp.maximum(m_st, cma_s[d, sl, :])
            quantities = (mx, jnp.exp(m_st - mx), jnp.exp(-(br + mx)), jnp.exp(b_last - br + li - m_new))
            parts = [p.astype(F32) for qty in quantities for p in _split3(qty)]
            parts.append(jnp.zeros((HEAD - len(parts) * G, L), F32))
            cols = _dot_tn(jnp.concatenate(parts, axis=0).astype(BF16), sel_ref[...])
            for j in range(G):
                rows = pl.ds(pl.multiple_of((g * G + j) * L, L), L)
                mx_col, winter, floor, wsc = (cols[:, (q * G + j) * HEAD:(q * G + j + 1) * HEAD] for q in range(4))
                am = jnp.where(causal[d], jnp.broadcast_to(a[j:j + 1], (L, L)), NEG_BIG)
                wi_s[d, rows, :] = jnp.exp(am - mx_col[:, :L])
                winter_s[d, rows, :] = winter
                floor_s[d, rows, :] = floor[:, 0:1]
                wsc_s[d, rows, :] = wsc
            wold_s[d, sl, :] = jnp.exp(b_last + m_st - m_new)
        return carry

    lax.fori_loop(0, nchunk // G, gate_weights, 0)

    def chunk_index(d, step):
        step = jnp.minimum(step, nchunk - 1)
        return step if d == 0 else nchunk - 1 - step

    def chunk_rows(d, step):
        return pl.ds(pl.multiple_of(chunk_index(d, step) * L, L), L)

    def value_aug(rows):
        return jnp.concatenate([v_ref[rows, :], ones_col], axis=1)

    def stage_a(d, step, slot):
        rows = chunk_rows(d, step)
        k = k_s[rows, :]
        qkw_s[slot, d] = (_dot_nt(q_s[rows, :].astype(BF16), k.astype(BF16)) * wi_s[d, rows, :]).astype(BF16)
        inc_s[slot, d] = _dot_tn((k * wsc_s[d, rows, :]).astype(BF16), value_aug(rows))

    def stage_b(d, step, slot):
        rows = chunk_rows(d, step)
        cst = c_s[d]
        w_inter = winter_s[d, rows, :]
        nd_s[slot, d] = (_dot(qkw_s[slot, d], value_aug(rows))
                         + jnp.concatenate([w_inter, w_inter], axis=1)
                         * _dot(q_s[rows, :].astype(BF16), cst.astype(BF16)))
        c_s[d] = wold_s[d, pl.ds(chunk_index(d, step), 1), :] * cst + inc_s[slot, d]

    def stage_c(d, step, slot):
        rows = chunk_rows(d, step)
        nd = nd_s[slot, d]
        h = nd[:, :HEAD] / jnp.maximum(jnp.abs(nd[:, HEAD:HEAD + 1]), floor_s[d, rows, :])
        acc_s[rows, :] = acc_s[rows, :] + h

    ns = MLSTM_STEPS_PER_TRIP

    def body(it, with_c):
        for par in range(ns):
            for d in range(2):
                if with_c:
                    stage_c(d, ns * it - ns + par, par)
        for par in range(ns):
            for d in range(2):
                stage_b(d, ns * it + par, par)
        for par in range(ns):
            for d in range(2):
                stage_a(d, ns * it + ns + par, par)

    def loop_body(it, carry):
        body(it, True)
        return carry

    acc_s[...] = jnp.zeros_like(acc_s)
    for d in range(2):
        for par in range(ns):
            stage_a(d, par, par)
    body(0, False)
    lax.fori_loop(1, nchunk // ns, loop_body, 0)
    for d in range(2):
        for par in range(ns):
            stage_c(d, nchunk - ns + par, par)

    gn = gn_ref[...]
    blk = 8 * L

    def fin(i, carry):
        rows = pl.ds(pl.multiple_of(i * blk, blk), blk)
        h = acc_s[rows, :]
        mu = jnp.mean(h, axis=-1, keepdims=True)
        cen = h - mu
        hn = cen * lax.rsqrt(jnp.mean(cen * cen, axis=-1, keepdims=True) + GN_EPS)
        o_ref[rows, :] = (hn * gn * _sigmoid(og_ref[rows, :].astype(F32))).astype(o_ref.dtype)
        return carry

    lax.fori_loop(0, (nchunk * L) // blk, fin, 0)


def _mlstm(proj, gates, gate_bias, conv_w, gn, *, batch, seq):
    nchunk = seq // CHUNK
    ns = MLSTM_STEPS_PER_TRIP
    assert nchunk % GATE_GROUP == 0 and nchunk % ns == 0 and seq % (8 * CHUNK) == 0
    nh = N_REC_HEADS
    width = nh * HEAD
    cum, causal, sel = _mlstm_constants()
    g5 = gates.reshape(batch, nchunk, CHUNK, 4, nh).transpose(0, 4, 3, 1, 2)
    gb = jnp.broadcast_to(gate_bias.reshape(4, nh).T[:, :, None, None], (nh, 4, 1, CHUNK))
    cw = jnp.pad(conv_w, ((0, 8 - CONV_W), (0, 0)))
    first_col = 5 * nh

    def col(kind):
        return pl.BlockSpec((seq, HEAD), lambda b, h: (b, first_col + kind * nh + h))

    const2 = lambda b, h: (0, 0)
    const3 = lambda b, h: (0, 0, 0)
    return pl.pallas_call(
        functools.partial(_mlstm_kernel, nchunk=nchunk),
        grid=(batch, nh),
        in_specs=[col(0), col(1), col(2), col(3),
                  pl.BlockSpec((None, None, 4, nchunk, CHUNK), lambda b, h: (b, h, 0, 0, 0)),
                  pl.BlockSpec((None, 4, 1, CHUNK), lambda b, h: (h, 0, 0, 0)),
                  pl.BlockSpec((8, HEAD), lambda b, h: (0, h)),
                  pl.BlockSpec((8, HEAD), lambda b, h: (0, nh + h)),
                  pl.BlockSpec((1, HEAD), lambda b, h: (0, h)),
                  pl.BlockSpec(cum.shape, const3),
                  pl.BlockSpec(causal.shape, const3),
                  pl.BlockSpec(sel.shape, const2)],
        out_specs=pl.BlockSpec((seq, HEAD), lambda b, h: (b, h)),
        out_shape=jax.ShapeDtypeStruct((batch * seq, width), BF16),
        scratch_shapes=[pltpu.VMEM((seq, HEAD), F32), pltpu.VMEM((seq, HEAD), F32),
                        pltpu.VMEM((seq, HEAD), F32), pltpu.VMEM((2, nchunk, CHUNK), F32),
                        pltpu.VMEM((2, HEAD, 2 * HEAD), F32), pltpu.VMEM((2, nchunk, CHUNK), F32)]
                       + [pltpu.VMEM((2, nchunk, 1), F32)] * 5
                       + [pltpu.VMEM((ns, 2, CHUNK, CHUNK), BF16), pltpu.VMEM((ns, 2, HEAD, 2 * HEAD), F32),
                          pltpu.VMEM((ns, 2, CHUNK, 2 * HEAD), F32),
                          pltpu.VMEM((2, seq, CHUNK), F32), pltpu.VMEM((2, seq, HEAD), F32),
                          pltpu.VMEM((2, seq, HEAD), F32), pltpu.VMEM((2, seq, 1), F32)],
        compiler_params=_cparams(2),
    )(proj, proj, proj, proj, g5, gb, cw, cw, gn.reshape(1, -1),
      jnp.asarray(cum, BF16), jnp.asarray(causal), jnp.asarray(sel, BF16))


def _na_bias_tables(rpb):
    w = GRID_W
    qc = np.arange(w)[:, None]
    kc = np.arange(w)[None, :]
    c0 = np.clip(qc - WIN_C // 2, 0, w - WIN_C)
    valid = (kc >= c0) & (kc < c0 + WIN_C)
    cidx = np.clip(kc - qc + WIN_C - 1, 0, 2 * WIN_C - 2)
    tiles = jnp.where(jnp.asarray(valid)[None, None], rpb.astype(F32)[:, :, cidx] * LOG2_E, NEG_BIG)
    return jnp.concatenate([tiles[:, :-1], tiles[:, 1:]], axis=-1)


def _na_kernel(q_ref, k_ref, v_ref, bias_ref, o_ref, s_even, s_odd, *, n_rows):
    w = GRID_W
    nk = WIN_R * w
    lane_head = lax.broadcasted_iota(jnp.int32, (w, HEAD), 1) // NA_DH
    head_masks = [lane_head == hh for hh in range(NA_GROUP)]

    def window_start(r):
        return jnp.clip(r - WIN_R // 2, 0, n_rows - WIN_R)

    def logits(r, dst):
        q = q_ref[pl.ds(pl.multiple_of(r * w, w), w), :]
        zero = jnp.zeros_like(q)
        qm = jnp.concatenate([jnp.where(head_masks[hh], q, zero) for hh in range(NA_GROUP)], axis=0)
        r0 = window_start(r)
        s = _dot_nt(qm, k_ref[pl.ds(pl.multiple_of(r0 * w, w), nk), :])
        first = WIN_R - 1 - (r - r0)
        dst[...] = s + jnp.concatenate(
            [jnp.concatenate([bias_ref[hh, first + 2 * m] for m in range(WIN_R // 2)], axis=1)
             for hh in range(NA_GROUP)], axis=0)

    def attend(r, src):
        s = src[...]
        pexp = jnp.exp2(s - jnp.max(s, axis=1, keepdims=True))
        l = jnp.sum(pexp, axis=1, keepdims=True)
        r0 = window_start(r)
        o = _dot(pexp.astype(BF16), v_ref[pl.ds(pl.multiple_of(r0 * w, w), nk), :]) / l
        out = jnp.where(head_masks[0], o[0:w], 0.0)
        for hh in range(1, NA_GROUP):
            out = out + jnp.where(head_masks[hh], o[hh * w:(hh + 1) * w], 0.0)
        o_ref[pl.ds(pl.multiple_of(r * w, w), w), :] = out.astype(o_ref.dtype)

    logits(0, s_even)

    def two_rows(k, carry):
        r = 2 * k
        logits(r + 1, s_odd)
        attend(r, s_even)
        logits(jnp.minimum(r + 2, n_rows - 1), s_even)
        attend(r + 1, s_odd)
        return carry

    lax.fori_loop(0, n_rows // 2, two_rows, 0)


def _neighbourhood_attention(qkv, bias, *, batch, seq):
    n_rows = seq // GRID_W
    assert n_rows % 2 == 0 and n_rows >= WIN_R
    d_model = qkv.shape[1] // 3
    n_groups = d_model // HEAD
    return pl.pallas_call(
        functools.partial(_na_kernel, n_rows=n_rows),
        grid=(batch, n_groups),
        in_specs=[pl.BlockSpec((seq, HEAD), lambda b, g: (b, g)),
                  pl.BlockSpec((seq, HEAD), lambda b, g: (b, n_groups + g)),
                  pl.BlockSpec((seq, HEAD), lambda b, g: (b, 2 * n_groups + g)),
                  pl.BlockSpec((NA_GROUP, 2 * WIN_R - 2, GRID_W, 2 * GRID_W), lambda b, g: (g, 0, 0, 0))],
        out_specs=pl.BlockSpec((seq, HEAD), lambda b, g: (b, g)),
        out_shape=jax.ShapeDtypeStruct((batch * seq, d_model), BF16),
        scratch_shapes=[pltpu.VMEM((NA_GROUP * GRID_W, WIN_R * GRID_W), F32)] * 2,
        compiler_params=_cparams(2),
    )(qkv, qkv, qkv, bias)


def _ffn_kernel(*refs, n_mix, alpha):
    x_ref = refs[0]
    mix_refs = refs[1:1 + 2 * n_mix]
    lmg_ref, lmb_ref, lfg_ref, lfb_ref, wg_ref, wu_ref, wd_ref, o_ref = refs[1 + 2 * n_mix:]
    half = x_ref.shape[0] // 2
    rows = (slice(0, half), slice(half, 2 * half))

    def pre(r):
        mix = _dot(mix_refs[0][r, :], mix_refs[1][...])
        for i in range(1, n_mix):
            mix = mix + _dot(mix_refs[2 * i][r, :], mix_refs[2 * i + 1][...])
        return _layer_norm(alpha * x_ref[r, :] + mix, lmg_ref[...], lmb_ref[...])

    def act(h):
        hb = h.astype(BF16)
        g = _dot(hb, wg_ref[...])
        u = _dot(hb, wu_ref[...])
        return (g * _sigmoid(g) * u).astype(BF16)

    h0 = pre(rows[0])
    a0 = act(h0)
    h1 = pre(rows[1])
    y0 = _dot(a0, wd_ref[...])
    a1 = act(h1)
    o_ref[rows[0], :] = _layer_norm(alpha * h0 + y0, lfg_ref[...], lfb_ref[...])
    y1 = _dot(a1, wd_ref[...])
    o_ref[rows[1], :] = _layer_norm(alpha * h1 + y1, lfg_ref[...], lfb_ref[...])


def _mixer_out_ffn(x, mix_pairs, ln_mix_g, ln_mix_b, ln_ffn_g, ln_ffn_b, wg, wu, wd, *, alpha, tm):
    t, d = x.shape
    assert t % tm == 0
    row = lambda i: (i, 0)
    in_specs = [pl.BlockSpec((tm, d), row)]
    args = [x]
    for o, w in mix_pairs:
        in_specs += [pl.BlockSpec((tm, o.shape[1]), row), _resident(w.shape)]
        args += [o, w]
    in_specs += [_resident((1, d))] * 4
    args += [ln_mix_g.reshape(1, d), ln_mix_b.reshape(1, d), ln_ffn_g.reshape(1, d), ln_ffn_b.reshape(1, d)]
    in_specs += [_resident(wg.shape), _resident(wu.shape), _resident(wd.shape)]
    args += [wg, wu, wd]
    return pl.pallas_call(
        functools.partial(_ffn_kernel, n_mix=len(mix_pairs), alpha=alpha),
        grid=(t // tm,),
        in_specs=in_specs,
        out_specs=pl.BlockSpec((tm, d), row),
        out_shape=jax.ShapeDtypeStruct((t, d), F32),
        compiler_params=_cparams(1),
    )(*args)


def _row_tile(t):
    for tm in (1024, 512, 256, 128):
        if t % tm == 0:
            return tm
    raise ValueError(f"token count {t} is not a multiple of 128")


def kernel(x, w_in_even, gate_bias_even, lb_raw, conv_qk, gn_hgrn, gn_mlstm, w_out_even, w_qkv_odd, rpb_odd,
           w_out_odd, ln_mix_g, ln_mix_b, ln_ffn_g, ln_ffn_b, w_ffn_gate, w_ffn_up, w_ffn_down):
    batch, seq, d_model = x.shape
    depth = ln_mix_g.shape[0]
    alpha = (2.0 * depth) ** 0.25
    t = batch * seq
    tm = min(_row_tile(t), 512)
    a_width = N_REC_HEADS * HEAD
    main_cols = 9 * a_width
    n_gate = 4 * N_REC_HEADS

    h = x.reshape(t, d_model)
    for layer in range(depth):
        j = layer // 2
        if layer % 2 == 0:
            w_in = w_in_even[j]
            w_main = w_in[:, :main_cols].astype(BF16)
            w_gate = jnp.pad(w_in[:, main_cols:], ((0, 0), (0, V7X_LANES - n_gate))).astype(BF16)
            proj, gates = _in_projection(h, w_main, w_gate, tm=tm)
            gates = gates[:, :n_gate]
            o_a = _hgrn(proj, lb_raw, gn_hgrn[j], batch=batch, seq=seq, layer_j=j)
            h_b = _mlstm(proj, gates, gate_bias_even[j], conv_qk[j], gn_mlstm[j], batch=batch, seq=seq)
            w_out = w_out_even[j].astype(BF16)
            mix_pairs = [(o_a, w_out[:a_width]), (h_b, w_out[a_width:])]
        else:
            qkv = _qkv_projection(h, w_qkv_odd[j].astype(BF16), q_scale=NA_DH ** -0.5 * LOG2_E, tm=tm)
            o = _neighbourhood_attention(qkv, _na_bias_tables(rpb_odd[j]), batch=batch, seq=seq)
            mix_pairs = [(o, w_out_odd[j].astype(BF16))]
        h = _mixer_out_ffn(h, mix_pairs, ln_mix_g[layer], ln_mix_b[layer], ln_ffn_g[layer], ln_ffn_b[layer],
                           w_ffn_gate[layer].astype(BF16), w_ffn_up[layer].astype(BF16),
                           w_ffn_down[layer].astype(BF16), alpha=alpha, tm=tm)
    return h.reshape(batch, seq, d_model)
```

```python
import functools

import numpy as np
import jax
import jax.numpy as jnp
from jax import lax
from jax.experimental import pallas as pl
from jax.experimental.pallas import tpu as pltpu

F32 = jnp.float32
BF16 = jnp.bfloat16

GRID_W = 64
HEAD = 128
N_REC_HEADS = 4
CHUNK = 64
CONV_W = 5
HGRN_STEPS_PER_TRIP = 4
MLSTM_STEPS_PER_TRIP = 4
GATE_GROUP = 8
NA_DH = 32
NA_GROUP = HEAD // NA_DH
WIN_R = 8
WIN_C = 16
LN_EPS = 1e-5
GN_EPS = 1e-6
NEG_BIG = -1e30
LB_FLOOR = 1e-30
LOG2_E = 1.4426950408889634

V7X_LANES = 128
V7X_VMEM_LIMIT_BYTES = 56 * 1024 * 1024

_LEVELS = (32, 16, 8, 4, 2, 1)


def _cparams(n_grid_axes):
    return pltpu.CompilerParams(
        dimension_semantics=("arbitrary",) * n_grid_axes,
        vmem_limit_bytes=V7X_VMEM_LIMIT_BYTES)


def _dot(a, b):
    return jnp.dot(a, b, preferred_element_type=F32)


def _dot_nt(a, b):
    return lax.dot_general(a, b, (((1,), (1,)), ((), ())), preferred_element_type=F32)


def _dot_tn(a, b):
    return lax.dot_general(a, b, (((0,), (0,)), ((), ())), preferred_element_type=F32)


def _split3(x):
    hi = x.astype(BF16)
    r1 = x - hi.astype(F32)
    mid = r1.astype(BF16)
    lo = (r1 - mid.astype(F32)).astype(BF16)
    return hi, mid, lo


def _log_sigmoid(z):
    return jnp.minimum(z, 0.0) - jnp.log1p(jnp.exp(-jnp.abs(z)))


def _sigmoid(z):
    return 1.0 / (1.0 + jnp.exp(-z))


def _layer_norm(t, g, b):
    mu = jnp.mean(t, axis=-1, keepdims=True)
    c = t - mu
    var = jnp.mean(c * c, axis=-1, keepdims=True)
    return c * lax.rsqrt(var + LN_EPS) * g + b


def _resident(shape):
    return pl.BlockSpec(shape, lambda i: (0,) * len(shape), pipeline_mode=pl.Buffered(1))


def _in_proj_kernel(x_ref, w_ref, wg_ref, o_ref, g_ref):
    xb = x_ref[...].astype(BF16)
    o_ref[...] = _dot(xb, w_ref[...]).astype(o_ref.dtype)
    g_ref[...] = _dot(xb, wg_ref[...])


def _in_projection(x, w_main, w_gate, *, tm):
    t, k = x.shape
    assert t % tm == 0
    return pl.pallas_call(
        _in_proj_kernel,
        grid=(t // tm,),
        in_specs=[pl.BlockSpec((tm, k), lambda i: (i, 0)), _resident(w_main.shape), _resident(w_gate.shape)],
        out_specs=[pl.BlockSpec((tm, w_main.shape[1]), lambda i: (i, 0)),
                   pl.BlockSpec((tm, w_gate.shape[1]), lambda i: (i, 0))],
        out_shape=[jax.ShapeDtypeStruct((t, w_main.shape[1]), BF16),
                   jax.ShapeDtypeStruct((t, w_gate.shape[1]), F32)],
        compiler_params=_cparams(1),
    )(x, w_main, w_gate)


def _qkv_kernel(x_ref, w_ref, o_ref, *, q_scale):
    acc = _dot(x_ref[...].astype(BF16), w_ref[...])
    d = acc.shape[1] // 3
    o_ref[:, :d] = (acc[:, :d] * q_scale).astype(o_ref.dtype)
    o_ref[:, d:] = acc[:, d:].astype(o_ref.dtype)


def _qkv_projection(x, w_qkv, *, q_scale, tm):
    t, k = x.shape
    assert t % tm == 0
    return pl.pallas_call(
        functools.partial(_qkv_kernel, q_scale=q_scale),
        grid=(t // tm,),
        in_specs=[pl.BlockSpec((tm, k), lambda i: (i, 0)), _resident(w_qkv.shape)],
        out_specs=pl.BlockSpec((tm, w_qkv.shape[1]), lambda i: (i, 0)),
        out_shape=jax.ShapeDtypeStruct((t, w_qkv.shape[1]), BF16),
        compiler_params=_cparams(1),
    )(x, w_qkv)


def _hgrn_constants():
    L = CHUNK
    t = np.arange(L)
    a_rows, rowsel, masks = [], [], []
    for c in _LEVELS:
        odd = (t // c) % 2 == 1
        rho = (t // (2 * c)) * 2 * c + c - 1
        u = t[None, :]
        a = np.where(odd[:, None], (u > rho[:, None]) & (u <= t[:, None]),
                     (u > t[:, None]) & (u <= rho[:, None]))
        a_rows.append(a.astype(np.float32))
        rowsel.append(np.broadcast_to(odd[:, None], (L, HEAD)).astype(np.float32))
        same = (t[:, None] // (2 * c)) == (t[None, :] // (2 * c))
        masks.append((odd[:, None] & ~odd[None, :] & same).astype(np.float32))
    masks.append(np.eye(L, dtype=np.float32))
    a_rows.append((t[None, :] <= t[:, None]).astype(np.float32))
    a_rows.append((t[None, :] > t[:, None]).astype(np.float32))
    a_f = np.stack(a_rows)
    rs_f = np.stack(rowsel)
    m_f = np.stack(masks)
    a = np.stack([a_f, a_f[:, ::-1, ::-1]]).reshape(2, 8 * L, L)
    a3 = np.concatenate([a, a, a, np.zeros_like(a)], axis=-1)
    rs = np.stack([rs_f, rs_f[:, ::-1]])
    m = np.stack([m_f, m_f[:, ::-1, ::-1]])
    return a3, rs, m


def _hgrn_kernel(q_ref, ff_ref, fb_ref, v_ref, g_ref, lb_ref, gn_ref, a3_ref, rs_ref, mk_ref,
                 o_ref, acc_s, st_s, y_s, sc_s, dec_s, fw_s, inc_s, dec3_s, *, layer_j, nchunk):
    L = CHUNK
    nlev = len(_LEVELS)
    f_refs = (ff_ref, fb_ref)
    Y_QPRE, Y_KSUF, Y_Q, Y_K = nlev, nlev + 1, nlev + 2, nlev + 3
    ns = HGRN_STEPS_PER_TRIP

    lbr = lb_ref[...]
    e = jnp.exp(lbr - jnp.max(lbr, axis=1, keepdims=True))
    soft = e / jnp.sum(e, axis=1, keepdims=True)
    cum = soft[:, 0:1, :]
    for i in range(1, layer_j + 1):
        cum = cum + soft[:, i:i + 1, :]
    lb = cum - soft[:, 0:1, :]
    lb_floor = jnp.maximum(lb, LB_FLOOR)
    one_m_lb = 1.0 - lb

    st_s[...] = jnp.zeros_like(st_s)

    def chunk_rows(d, step):
        step = jnp.minimum(step, nchunk - 1)
        c = step if d == 0 else nchunk - 1 - step
        return pl.ds(pl.multiple_of(c * L, L), L)

    def stage1(d, step0, slot0):
        qs, ks, parts = [], [], []
        for par in range(2):
            rows = chunk_rows(d, step0 + par)
            qs.append(q_ref[rows, :].astype(F32))
            f = lb_floor[d] + one_m_lb[d] * _sigmoid(f_refs[d][rows, :].astype(F32))
            ks.append(1.0 - f)
            hi, mid, lo = _split3(jnp.log(f) * LOG2_E)
            parts.append(jnp.concatenate([hi, mid, lo, jnp.zeros_like(hi)], axis=0))
        dall = _dot(a3_ref[d], jnp.concatenate(parts, axis=1))
        for half in range(2):
            slot = slot0 + half
            q, k = qs[half], ks[half]
            eall = jnp.exp2(dall[:, half * HEAD:(half + 1) * HEAD])
            for li, c in enumerate(_LEVELS):
                if c % 8 == 0:
                    first_q = 1 if d == 0 else 0
                    x = jnp.concatenate([(q if b % 2 == first_q else k)[b * c:(b + 1) * c] for b in range(L // c)],
                                        axis=0)
                else:
                    x = jnp.where(rs_ref[d, li] > 0.5, q, k)
                y_s[slot, d, li] = (x * eall[li * L:(li + 1) * L]).astype(BF16)
            e_pre = eall[nlev * L:(nlev + 1) * L]
            e_suf = eall[(nlev + 1) * L:(nlev + 2) * L]
            y_s[slot, d, Y_QPRE] = (q * e_pre).astype(BF16)
            y_s[slot, d, Y_KSUF] = (k * e_suf).astype(BF16)
            y_s[slot, d, Y_Q] = q.astype(BF16)
            y_s[slot, d, Y_K] = k.astype(BF16)
            last = L - 1 if d == 0 else 0
            dec_s[slot, d] = jnp.broadcast_to(e_pre[last:last + 1, :], (8, HEAD))

    def stage2(d, step, slot):
        scores = _dot_nt(y_s[slot, d, Y_Q], y_s[slot, d, Y_K]) * mk_ref[d, nlev]
        for li in range(nlev):
            y = y_s[slot, d, li]
            scores = scores + _dot_nt(y, y) * mk_ref[d, li]
        sc_s[slot, d] = scores.astype(BF16)
        fw_s[slot, d] = y_s[slot, d, Y_QPRE]
        inc_s[slot, d] = _dot_tn(v_ref[chunk_rows(d, step), :].astype(BF16), y_s[slot, d, Y_KSUF])
        dec3_s[slot, d] = dec_s[slot, d]

    def stage3(d, step, slot):
        rows = chunk_rows(d, step)
        st = st_s[d]
        o = _dot(sc_s[slot, d], v_ref[rows, :].astype(BF16)) + _dot_nt(fw_s[slot, d], st.astype(BF16))
        st_s[d] = st * dec3_s[slot, d][0:1, :] + inc_s[slot, d]
        acc_s[rows, :] = acc_s[rows, :] + o

    def body(it, carry):
        for par in range(ns):
            for d in range(2):
                stage3(d, ns * it + par, par)
        for par in range(ns):
            for d in range(2):
                stage2(d, ns * it + ns + par, par)
        for par in range(0, ns, 2):
            for d in range(2):
                stage1(d, ns * it + 2 * ns + par, par)
        return carry

    acc_s[...] = jnp.zeros_like(acc_s)
    for par in range(0, ns, 2):
        for d in range(2):
            stage1(d, par, par)
    for par in range(ns):
        for d in range(2):
            stage2(d, par, par)
    for par in range(0, ns, 2):
        for d in range(2):
            stage1(d, ns + par, par)
    lax.fori_loop(0, nchunk // ns, body, 0, unroll=2)

    gn = gn_ref[...]
    blk = 8 * L

    def fin(i, carry):
        rows = pl.ds(pl.multiple_of(i * blk, blk), blk)
        o = acc_s[rows, :]
        g = g_ref[rows, :].astype(F32)
        o = o * lax.rsqrt(jnp.mean(o * o, axis=-1, keepdims=True) + GN_EPS)
        o_ref[rows, :] = (o * gn * (g * _sigmoid(g))).astype(o_ref.dtype)
        return carry

    lax.fori_loop(0, (nchunk * L) // blk, fin, 0)


def _hgrn(proj, lb_raw, gn, *, batch, seq, layer_j):
    nchunk = seq // CHUNK
    ns = HGRN_STEPS_PER_TRIP
    assert nchunk % ns == 0 and seq % (8 * CHUNK) == 0
    a3, rs, mk = _hgrn_constants()
    n_even = lb_raw.shape[1]
    nh = N_REC_HEADS

    def col(kind):
        return pl.BlockSpec((seq, HEAD), lambda b, h: (b, kind * nh + h))

    const3 = lambda b, h: (0, 0, 0)
    const4 = lambda b, h: (0, 0, 0, 0)
    return pl.pallas_call(
        functools.partial(_hgrn_kernel, layer_j=layer_j, nchunk=nchunk),
        grid=(batch, nh),
        in_specs=[col(0), col(1), col(2), col(3), col(4),
                  pl.BlockSpec((2, n_even, HEAD), lambda b, h: (0, 0, h)),
                  pl.BlockSpec((1, HEAD), lambda b, h: (0, h)),
                  pl.BlockSpec(a3.shape, const3),
                  pl.BlockSpec(rs.shape, const4),
                  pl.BlockSpec(mk.shape, const4)],
        out_specs=pl.BlockSpec((seq, HEAD), lambda b, h: (b, h)),
        out_shape=jax.ShapeDtypeStruct((batch * seq, nh * HEAD), BF16),
        scratch_shapes=[pltpu.VMEM((seq, HEAD), F32), pltpu.VMEM((2, HEAD, HEAD), F32),
                        pltpu.VMEM((ns, 2, len(_LEVELS) + 4, CHUNK, HEAD), BF16),
                        pltpu.VMEM((ns, 2, CHUNK, CHUNK), BF16),
                        pltpu.VMEM((ns, 2, 8, HEAD), F32),
                        pltpu.VMEM((ns, 2, CHUNK, HEAD), BF16),
                        pltpu.VMEM((ns, 2, HEAD, HEAD), F32),
                        pltpu.VMEM((ns, 2, 8, HEAD), F32)],
        compiler_params=_cparams(2),
    )(proj, proj, proj, proj, proj, lb_raw, gn.reshape(1, -1),
      jnp.asarray(a3, BF16), jnp.asarray(rs), jnp.asarray(mk))


def _mlstm_constants():
    L = CHUNK
    t = np.arange(L)
    ut = (t[:, None] <= t[None, :]).astype(np.float32)
    cum = np.stack([ut, ut[::-1, ::-1]])
    tril = (t[None, :] <= t[:, None]).astype(np.float32)
    causal = np.stack([tril, tril[::-1, ::-1]])
    sel = np.zeros((HEAD, 4 * GATE_GROUP * HEAD), np.float32)
    for q in range(4):
        for p in range(3):
            for j in range(GATE_GROUP):
                sel[(q * 3 + p) * GATE_GROUP + j, (q * GATE_GROUP + j) * HEAD:(q * GATE_GROUP + j + 1) * HEAD] = 1.0
    return cum, causal, sel


def _mlstm_kernel(xq_ref, xk_ref, v_ref, og_ref, gates_ref, gb_ref, cwq_ref, cwk_ref, gn_ref,
                  cum_ref, cm_ref, sel_ref, o_ref, q_s, k_s, acc_s, b_s, c_s, cma_s,
                  bl_s, gm_s, mst_s, mnew_s, wold_s, qkw_s, inc_s, nd_s, wi_s, winter_s, wsc_s, floor_s,
                  *, nchunk):
    L = CHUNK
    pad = CONV_W // 2
    halo = 16

    def conv_chunk(c, carry):
        rows = pl.ds(pl.multiple_of(c * L, L), L)
        prev = pl.ds(pl.multiple_of(jnp.maximum(c * L - halo, 0), halo), halo)
        nxt = pl.ds(pl.multiple_of(jnp.minimum(c * L + L, nchunk * L - halo), halo), halo)
        has_prev = jnp.where(c > 0, 1.0, 0.0).astype(F32)
        has_next = jnp.where(c < nchunk - 1, 1.0, 0.0).astype(F32)
        for x_ref, w_ref, dst, scale in ((xq_ref, cwq_ref, q_s, 1.0), (xk_ref, cwk_ref, k_s, HEAD ** -0.5)):
            win = jnp.concatenate([x_ref[prev, :].astype(F32) * has_prev, x_ref[rows, :].astype(F32),
                                   x_ref[nxt, :].astype(F32) * has_next], axis=0)
            w = w_ref[...]
            acc = win[halo - pad:halo - pad + L] * w[0:1, :]
            for j in range(1, CONV_W):
                acc = acc + win[halo - pad + j:halo - pad + j + L] * w[j:j + 1, :]
            y = acc * _sigmoid(acc)
            dst[rows, :] = y * scale if scale != 1.0 else y
        return carry

    lax.fori_loop(0, nchunk, conv_chunk, 0)

    for d in range(2):
        lf2 = _log_sigmoid(gates_ref[2 + d] + gb_ref[2 + d])
        hi, mid, lo = _split3(lf2)
        cm = cum_ref[d]
        b2 = _dot(hi, cm) + _dot(mid, cm) + _dot(lo, cm)
        b_s[d] = b2
        last = L - 1 if d == 0 else 0
        bl_s[d] = b2[:, last:last + 1]
        li2 = gates_ref[d] + gb_ref[d]
        gm_s[d] = jnp.max(b2[:, last:last + 1] - b2 + li2, axis=1, keepdims=True)
        x = jnp.concatenate([li2 - b2, jnp.full((nchunk, HEAD - L), NEG_BIG, F32)], axis=1)
        lane_x = lax.broadcasted_iota(jnp.int32, x.shape, 1)
        sh = 1
        while sh < L:
            if d == 0:
                x = jnp.maximum(x, jnp.where(lane_x >= sh, pltpu.roll(x, sh, axis=1), NEG_BIG))
            else:
                x = jnp.maximum(x, pltpu.roll(x, HEAD - sh, axis=1))
            sh *= 2
        cma_s[d] = x[:, :L]

    c_s[...] = jnp.zeros_like(c_s)
    lane = lax.broadcasted_iota(jnp.int32, (L, HEAD), 1)
    ones_col = (lane == 0).astype(BF16)

    def stab_step(step, m):
        new = []
        for d in range(2):
            c = step if d == 0 else nchunk - 1 - step
            sl = pl.ds(c, 1)
            mst_s[d, sl, :] = m[d]
            m_new = jnp.maximum(bl_s[d, sl, :] + m[d], gm_s[d, sl, :])
            mnew_s[d, sl, :] = m_new
            new.append(m_new)
        return tuple(new)

    lax.fori_loop(0, nchunk, stab_step, (jnp.zeros((1, 1), F32), jnp.zeros((1, 1), F32)))

    G = GATE_GROUP
    causal = [cm_ref[d] > 0.5 for d in range(2)]

    def gate_weights(g, carry):
        sl = pl.ds(pl.multiple_of(g * G, G), G)
        for d in range(2):
            br = b_s[d, sl, :]
            li = gates_ref[d, sl, :] + gb_ref[d]
            a = li - br
            m_st = mst_s[d, sl, :]
            m_new = mnew_s[d, sl, :]
            last = L - 1 if d == 0 else 0
            b_last = br[:, last:last + 1]
            mx = jnp.maximum(m_st, cma_s[d, sl, :])
            quantities = (mx, jnp.exp(m_st - mx), jnp.exp(-(br + mx)), jnp.exp(b_last - br + li - m_new))
            parts = [p.astype(F32) for qty in quantities for p in _split3(qty)]
            parts.append(jnp.zeros((HEAD - len(parts) * G, L), F32))
            cols = _dot_tn(jnp.concatenate(parts, axis=0).astype(BF16), sel_ref[...])
            for j in range(G):
                rows = pl.ds(pl.multiple_of((g * G + j) * L, L), L)
                mx_col, winter, floor, wsc = (cols[:, (q * G + j) * HEAD:(q * G + j + 1) * HEAD] for q in range(4))
                am = jnp.where(causal[d], jnp.broadcast_to(a[j:j + 1], (L, L)), NEG_BIG)
                wi_s[d, rows, :] = jnp.exp(am - mx_col[:, :L])
                winter_s[d, rows, :] = winter
                floor_s[d, rows, :] = floor[:, 0:1]
                wsc_s[d, rows, :] = wsc
            wold_s[d, sl, :] = jnp.exp(b_last + m_st - m_new)
        return carry

    lax.fori_loop(0, nchunk // G, gate_weights, 0, unroll=2)

    def chunk_index(d, step):
        step = jnp.minimum(step, nchunk - 1)
        return step if d == 0 else nchunk - 1 - step

    def chunk_rows(d, step):
        return pl.ds(pl.multiple_of(chunk_index(d, step) * L, L), L)

    def value_aug(rows):
        return jnp.concatenate([v_ref[rows, :], ones_col], axis=1)

    def stage_a(d, step, slot):
        rows = chunk_rows(d, step)
        k = k_s[rows, :]
        qkw_s[slot, d] = (_dot_nt(q_s[rows, :].astype(BF16), k.astype(BF16)) * wi_s[d, rows, :]).astype(BF16)
        inc_s[slot, d] = _dot_tn((k * wsc_s[d, rows, :]).astype(BF16), value_aug(rows))

    def stage_b(d, step, slot):
        rows = chunk_rows(d, step)
        cst = c_s[d]
        w_inter = winter_s[d, rows, :]
        nd_s[slot, d] = (_dot(qkw_s[slot, d], value_aug(rows))
                         + jnp.concatenate([w_inter, w_inter], axis=1)
                         * _dot(q_s[rows, :].astype(BF16), cst.astype(BF16)))
        c_s[d] = wold_s[d, pl.ds(chunk_index(d, step), 1), :] * cst + inc_s[slot, d]

    def stage_c(d, step, slot):
        rows = chunk_rows(d, step)
        nd = nd_s[slot, d]
        h = nd[:, :HEAD] / jnp.maximum(jnp.abs(nd[:, HEAD:HEAD + 1]), floor_s[d, rows, :])
        acc_s[rows, :] = acc_s[rows, :] + h

    ns = MLSTM_STEPS_PER_TRIP

    def body(it, with_c):
        for par in range(ns):
            for d in range(2):
                if with_c:
                    stage_c(d, ns * it - ns + par, par)
        for par in range(ns):
            for d in range(2):
                stage_b(d, ns * it + par, par)
        for par in range(ns):
            for d in range(2):
                stage_a(d, ns * it + ns + par, par)

    def loop_body(it, carry):
        body(it, True)
        return carry

    acc_s[...] = jnp.zeros_like(acc_s)
    for d in range(2):
        for par in range(ns):
            stage_a(d, par, par)
    body(0, False)
    lax.fori_loop(1, nchunk // ns, loop_body, 0, unroll=3)
    for d in range(2):
        for par in range(ns):
            stage_c(d, nchunk - ns + par, par)

    gn = gn_ref[...]
    blk = 8 * L

    def fin(i, carry):
        rows = pl.ds(pl.multiple_of(i * blk, blk), blk)
        h = acc_s[rows, :]
        mu = jnp.mean(h, axis=-1, keepdims=True)
        cen = h - mu
        hn = cen * lax.rsqrt(jnp.mean(cen * cen, axis=-1, keepdims=True) + GN_EPS)
        o_ref[rows, :] = (hn * gn * _sigmoid(og_ref[rows, :].astype(F32))).astype(o_ref.dtype)
        return carry

    lax.fori_loop(0, (nchunk * L) // blk, fin, 0)


def _mlstm(proj, gates, gate_bias, conv_w, gn, *, batch, seq):
    nchunk = seq // CHUNK
    ns = MLSTM_STEPS_PER_TRIP
    assert nchunk % GATE_GROUP == 0 and nchunk % ns == 0 and seq % (8 * CHUNK) == 0
    nh = N_REC_HEADS
    width = nh * HEAD
    cum, causal, sel = _mlstm_constants()
    g5 = gates.reshape(batch, nchunk, CHUNK, 4, nh).transpose(0, 4, 3, 1, 2)
    gb = jnp.broadcast_to(gate_bias.reshape(4, nh).T[:, :, None, None], (nh, 4, 1, CHUNK))
    cw = jnp.pad(conv_w, ((0, 8 - CONV_W), (0, 0)))
    first_col = 5 * nh

    def col(kind):
        return pl.BlockSpec((seq, HEAD), lambda b, h: (b, first_col + kind * nh + h))

    const2 = lambda b, h: (0, 0)
    const3 = lambda b, h: (0, 0, 0)
    return pl.pallas_call(
        functools.partial(_mlstm_kernel, nchunk=nchunk),
        grid=(batch, nh),
        in_specs=[col(0), col(1), col(2), col(3),
                  pl.BlockSpec((None, None, 4, nchunk, CHUNK), lambda b, h: (b, h, 0, 0, 0)),
                  pl.BlockSpec((None, 4, 1, CHUNK), lambda b, h: (h, 0, 0, 0)),
                  pl.BlockSpec((8, HEAD), lambda b, h: (0, h)),
                  pl.BlockSpec((8, HEAD), lambda b, h: (0, nh + h)),
                  pl.BlockSpec((1, HEAD), lambda b, h: (0, h)),
                  pl.BlockSpec(cum.shape, const3),
                  pl.BlockSpec(causal.shape, const3),
                  pl.BlockSpec(sel.shape, const2)],
        out_specs=pl.BlockSpec((seq, HEAD), lambda b, h: (b, h)),
        out_shape=jax.ShapeDtypeStruct((batch * seq, width), BF16),
        scratch_shapes=[pltpu.VMEM((seq, HEAD), F32), pltpu.VMEM((seq, HEAD), F32),
                        pltpu.VMEM((seq, HEAD), F32), pltpu.VMEM((2, nchunk, CHUNK), F32),
                        pltpu.VMEM((2, HEAD, 2 * HEAD), F32), pltpu.VMEM((2, nchunk, CHUNK), F32)]
                       + [pltpu.VMEM((2, nchunk, 1), F32)] * 5
                       + [pltpu.VMEM((ns, 2, CHUNK, CHUNK), BF16), pltpu.VMEM((ns, 2, HEAD, 2 * HEAD), F32),
                          pltpu.VMEM((ns, 2, CHUNK, 2 * HEAD), F32),
                          pltpu.VMEM((2, seq, CHUNK), F32), pltpu.VMEM((2, seq, HEAD), F32),
                          pltpu.VMEM((2, seq, HEAD), F32), pltpu.VMEM((2, seq, 1), F32)],
        compiler_params=_cparams(2),
    )(proj, proj, proj, proj, g5, gb, cw, cw, gn.reshape(1, -1),
      jnp.asarray(cum, BF16), jnp.asarray(causal), jnp.asarray(sel, BF16))


def _na_bias_tables(rpb):
    w = GRID_W
    qc = np.arange(w)[:, None]
    kc = np.arange(w)[None, :]
    c0 = np.clip(qc - WIN_C // 2, 0, w - WIN_C)
    valid = (kc >= c0) & (kc < c0 + WIN_C)
    cidx = np.clip(kc - qc + WIN_C - 1, 0, 2 * WIN_C - 2)
    tiles = jnp.where(jnp.asarray(valid)[None, None], rpb.astype(F32)[:, :, cidx] * LOG2_E, NEG_BIG)
    return jnp.concatenate([tiles[:, :-1], tiles[:, 1:]], axis=-1)


def _na_kernel(q_ref, k_ref, v_ref, bias_ref, o_ref, s_even, s_odd, *, n_rows):
    w = GRID_W
    nk = WIN_R * w
    lane_head = lax.broadcasted_iota(jnp.int32, (w, HEAD), 1) // NA_DH
    head_masks = [lane_head == hh for hh in range(NA_GROUP)]

    def window_start(r):
        return jnp.clip(r - WIN_R // 2, 0, n_rows - WIN_R)

    def logits(r, dst):
        q = q_ref[pl.ds(pl.multiple_of(r * w, w), w), :]
        zero = jnp.zeros_like(q)
        qm = jnp.concatenate([jnp.where(head_masks[hh], q, zero) for hh in range(NA_GROUP)], axis=0)
        r0 = window_start(r)
        s = _dot_nt(qm, k_ref[pl.ds(pl.multiple_of(r0 * w, w), nk), :])
        first = WIN_R - 1 - (r - r0)
        dst[...] = s + jnp.concatenate(
            [jnp.concatenate([bias_ref[hh, first + 2 * m] for m in range(WIN_R // 2)], axis=1)
             for hh in range(NA_GROUP)], axis=0)

    def attend(r, src):
        s = src[...]
        pexp = jnp.exp2(s - jnp.max(s, axis=1, keepdims=True))
        l = jnp.sum(pexp, axis=1, keepdims=True)
        r0 = window_start(r)
        o = _dot(pexp.astype(BF16), v_ref[pl.ds(pl.multiple_of(r0 * w, w), nk), :]) / l
        out = jnp.where(head_masks[0], o[0:w], 0.0)
        for hh in range(1, NA_GROUP):
            out = out + jnp.where(head_masks[hh], o[hh * w:(hh + 1) * w], 0.0)
        o_ref[pl.ds(pl.multiple_of(r * w, w), w), :] = out.astype(o_ref.dtype)

    logits(0, s_even)

    def two_rows(k, carry):
        r = 2 * k
        logits(r + 1, s_odd)
        attend(r, s_even)
        logits(jnp.minimum(r + 2, n_rows - 1), s_even)
        attend(r + 1, s_odd)
        return carry

    lax.fori_loop(0, n_rows // 2, two_rows, 0, unroll=4)


def _neighbourhood_attention(qkv, bias, *, batch, seq):
    n_rows = seq // GRID_W
    assert n_rows % 2 == 0 and n_rows >= WIN_R
    d_model = qkv.shape[1] // 3
    n_groups = d_model // HEAD
    return pl.pallas_call(
        functools.partial(_na_kernel, n_rows=n_rows),
        grid=(batch, n_groups),
        in_specs=[pl.BlockSpec((seq, HEAD), lambda b, g: (b, g)),
                  pl.BlockSpec((seq, HEAD), lambda b, g: (b, n_groups + g)),
                  pl.BlockSpec((seq, HEAD), lambda b, g: (b, 2 * n_groups + g)),
                  pl.BlockSpec((NA_GROUP, 2 * WIN_R - 2, GRID_W, 2 * GRID_W), lambda b, g: (g, 0, 0, 0))],
        out_specs=pl.BlockSpec((seq, HEAD), lambda b, g: (b, g)),
        out_shape=jax.ShapeDtypeStruct((batch * seq, d_model), BF16),
        scratch_shapes=[pltpu.VMEM((NA_GROUP * GRID_W, WIN_R * GRID_W), F32)] * 2,
        compiler_params=_cparams(2),
    )(qkv, qkv, qkv, bias)


def _ffn_kernel(*refs, n_mix, alpha):
    x_ref = refs[0]
    mix_refs = refs[1:1 + 2 * n_mix]
    lmg_ref, lmb_ref, lfg_ref, lfb_ref, wg_ref, wu_ref, wd_ref, o_ref = refs[1 + 2 * n_mix:]
    half = x_ref.shape[0] // 2
    rows = (slice(0, half), slice(half, 2 * half))

    def pre(r):
        mix = _dot(mix_refs[0][r, :], mix_refs[1][...])
        for i in range(1, n_mix):
            mix = mix + _dot(mix_refs[2 * i][r, :], mix_refs[2 * i + 1][...])
        return _layer_norm(alpha * x_ref[r, :] + mix, lmg_ref[...], lmb_ref[...])

    def act(h):
        hb = h.astype(BF16)
        g = _dot(hb, wg_ref[...])
        u = _dot(hb, wu_ref[...])
        return (g * _sigmoid(g) * u).astype(BF16)

    h0 = pre(rows[0])
    a0 = act(h0)
    h1 = pre(rows[1])
    y0 = _dot(a0, wd_ref[...])
    a1 = act(h1)
    o_ref[rows[0], :] = _layer_norm(alpha * h0 + y0, lfg_ref[...], lfb_ref[...])
    y1 = _dot(a1, wd_ref[...])
    o_ref[rows[1], :] = _layer_norm(alpha * h1 + y1, lfg_ref[...], lfb_ref[...])


def _mixer_out_ffn(x, mix_pairs, ln_mix_g, ln_mix_b, ln_ffn_g, ln_ffn_b, wg, wu, wd, *, alpha, tm):
    t, d = x.shape
    assert t % tm == 0
    row = lambda i: (i, 0)
    in_specs = [pl.BlockSpec((tm, d), row)]
    args = [x]
    for o, w in mix_pairs:
        in_specs += [pl.BlockSpec((tm, o.shape[1]), row), _resident(w.shape)]
        args += [o, w]
    in_specs += [_resident((1, d))] * 4
    args += [ln_mix_g.reshape(1, d), ln_mix_b.reshape(1, d), ln_ffn_g.reshape(1, d), ln_ffn_b.reshape(1, d)]
    in_specs += [_resident(wg.shape), _resident(wu.shape), _resident(wd.shape)]
    args += [wg, wu, wd]
    return pl.pallas_call(
        functools.partial(_ffn_kernel, n_mix=len(mix_pairs), alpha=alpha),
        grid=(t // tm,),
        in_specs=in_specs,
        out_specs=pl.BlockSpec((tm, d), row),
        out_shape=jax.ShapeDtypeStruct((t, d), F32),
        compiler_params=_cparams(1),
    )(*args)


def _row_tile(t):
    for tm in (1024, 512, 256, 128):
        if t % tm == 0:
            return tm
    raise ValueError(f"token count {t} is not a multiple of 128")


def kernel(x, w_in_even, gate_bias_even, lb_raw, conv_qk, gn_hgrn, gn_mlstm, w_out_even, w_qkv_odd, rpb_odd,
           w_out_odd, ln_mix_g, ln_mix_b, ln_ffn_g, ln_ffn_b, w_ffn_gate, w_ffn_up, w_ffn_down):
    batch, seq, d_model = x.shape
    depth = ln_mix_g.shape[0]
    alpha = (2.0 * depth) ** 0.25
    t = batch * seq
    tm = min(_row_tile(t), 512)
    a_width = N_REC_HEADS * HEAD
    main_cols = 9 * a_width
    n_gate = 4 * N_REC_HEADS

    h = x.reshape(t, d_model)
    for layer in range(depth):
        j = layer // 2
        if layer % 2 == 0:
            w_in = w_in_even[j]
            w_main = w_in[:, :main_cols].astype(BF16)
            w_gate = jnp.pad(w_in[:, main_cols:], ((0, 0), (0, V7X_LANES - n_gate))).astype(BF16)
            proj, gates = _in_projection(h, w_main, w_gate, tm=tm)
            gates = gates[:, :n_gate]
            o_a = _hgrn(proj, lb_raw, gn_hgrn[j], batch=batch, seq=seq, layer_j=j)
            h_b = _mlstm(proj, gates, gate_bias_even[j], conv_qk[j], gn_mlstm[j], batch=batch, seq=seq)
            w_out = w_out_even[j].astype(BF16)
            mix_pairs = [(o_a, w_out[:a_width]), (h_b, w_out[a_width:])]
        else:
            qkv = _qkv_projection(h, w_qkv_odd[j].astype(BF16), q_scale=NA_DH ** -0.5 * LOG2_E, tm=tm)
            o = _neighbourhood_attention(qkv, _na_bias_tables(rpb_odd[j]), batch=batch, seq=seq)
            mix_pairs = [(o, w_out_odd[j].astype(BF16))]
        h = _mixer_out_ffn(h, mix_pairs, ln_mix_g[layer], ln_mix_b[layer], ln_ffn_g[layer], ln_ffn_b[layer],
                           w_ffn_gate[layer].astype(BF16), w_ffn_up[layer].astype(BF16),
                           w_ffn_down[layer].astype(BF16), alpha=alpha, tm=tm)
    return h.reshape(batch, seq, d_model)
```

```python
import functools

import numpy as np
import jax
import jax.numpy as jnp
from jax import lax
from jax.experimental import pallas as pl
from jax.experimental.pallas import tpu as pltpu

F32 = jnp.float32
BF16 = jnp.bfloat16

GRID_W = 64
HEAD = 128
N_REC_HEADS = 4
CHUNK = 64
CONV_W = 5
HGRN_STEPS_PER_TRIP = 4
MLSTM_STEPS_PER_TRIP = 4
GATE_GROUP = 8
NA_DH = 32
NA_GROUP = HEAD // NA_DH
WIN_R = 8
WIN_C = 16
LN_EPS = 1e-5
GN_EPS = 1e-6
NEG_BIG = -1e30
LB_FLOOR = 1e-30
LOG2_E = 1.4426950408889634

V7X_LANES = 128
V7X_VMEM_LIMIT_BYTES = 56 * 1024 * 1024

_LEVELS = (32, 16, 8, 4, 2, 1)


def _cparams(n_grid_axes):
    return pltpu.CompilerParams(
        dimension_semantics=("arbitrary",) * n_grid_axes,
        vmem_limit_bytes=V7X_VMEM_LIMIT_BYTES)


def _dot(a, b):
    return jnp.dot(a, b, preferred_element_type=F32)


def _dot_nt(a, b):
    return lax.dot_general(a, b, (((1,), (1,)), ((), ())), preferred_element_type=F32)


def _dot_tn(a, b):
    return lax.dot_general(a, b, (((0,), (0,)), ((), ())), preferred_element_type=F32)


def _split3(x):
    hi = x.astype(BF16)
    r1 = x - hi.astype(F32)
    mid = r1.astype(BF16)
    lo = (r1 - mid.astype(F32)).astype(BF16)
    return hi, mid, lo


def _log_sigmoid(z):
    return jnp.minimum(z, 0.0) - jnp.log1p(jnp.exp(-jnp.abs(z)))


def _sigmoid(z):
    return 1.0 / (1.0 + jnp.exp(-z))


def _layer_norm(t, g, b):
    mu = jnp.mean(t, axis=-1, keepdims=True)
    c = t - mu
    var = jnp.mean(c * c, axis=-1, keepdims=True)
    return c * lax.rsqrt(var + LN_EPS) * g + b


def _resident(shape):
    return pl.BlockSpec(shape, lambda i: (0,) * len(shape), pipeline_mode=pl.Buffered(1))


def _in_proj_kernel(x_ref, w_ref, wg_ref, o_ref, g_ref):
    xb = x_ref[...].astype(BF16)
    o_ref[...] = _dot(xb, w_ref[...]).astype(o_ref.dtype)
    g_ref[...] = _dot(xb, wg_ref[...])


def _in_projection(x, w_main, w_gate, *, tm):
    t, k = x.shape
    assert t % tm == 0
    return pl.pallas_call(
        _in_proj_kernel,
        grid=(t // tm,),
        in_specs=[pl.BlockSpec((tm, k), lambda i: (i, 0)), _resident(w_main.shape), _resident(w_gate.shape)],
        out_specs=[pl.BlockSpec((tm, w_main.shape[1]), lambda i: (i, 0)),
                   pl.BlockSpec((tm, w_gate.shape[1]), lambda i: (i, 0))],
        out_shape=[jax.ShapeDtypeStruct((t, w_main.shape[1]), BF16),
                   jax.ShapeDtypeStruct((t, w_gate.shape[1]), F32)],
        compiler_params=_cparams(1),
    )(x, w_main, w_gate)


def _qkv_kernel(x_ref, w_ref, o_ref, *, q_scale):
    acc = _dot(x_ref[...].astype(BF16), w_ref[...])
    d = acc.shape[1] // 3
    o_ref[:, :d] = (acc[:, :d] * q_scale).astype(o_ref.dtype)
    o_ref[:, d:] = acc[:, d:].astype(o_ref.dtype)


def _qkv_projection(x, w_qkv, *, q_scale, tm):
    t, k = x.shape
    assert t % tm == 0
    return pl.pallas_call(
        functools.partial(_qkv_kernel, q_scale=q_scale),
        grid=(t // tm,),
        in_specs=[pl.BlockSpec((tm, k), lambda i: (i, 0)), _resident(w_qkv.shape)],
        out_specs=pl.BlockSpec((tm, w_qkv.shape[1]), lambda i: (i, 0)),
        out_shape=jax.ShapeDtypeStruct((t, w_qkv.shape[1]), BF16),
        compiler_params=_cparams(1),
    )(x, w_qkv)


def _hgrn_constants():
    L = CHUNK
    t = np.arange(L)
    a_rows, rowsel, masks = [], [], []
    for c in _LEVELS:
        odd = (t // c) % 2 == 1
        rho = (t // (2 * c)) * 2 * c + c - 1
        u = t[None, :]
        a = np.where(odd[:, None], (u > rho[:, None]) & (u <= t[:, None]),
                     (u > t[:, None]) & (u <= rho[:, None]))
        a_rows.append(a.astype(np.float32))
        rowsel.append(np.broadcast_to(odd[:, None], (L, HEAD)).astype(np.float32))
        same = (t[:, None] // (2 * c)) == (t[None, :] // (2 * c))
        masks.append((odd[:, None] & ~odd[None, :] & same).astype(np.float32))
    masks.append(np.eye(L, dtype=np.float32))
    a_rows.append((t[None, :] <= t[:, None]).astype(np.float32))
    a_rows.append((t[None, :] > t[:, None]).astype(np.float32))
    a_f = np.stack(a_rows)
    rs_f = np.stack(rowsel)
    m_f = np.stack(masks)
    a = np.stack([a_f, a_f[:, ::-1, ::-1]]).reshape(2, 8 * L, L)
    a3 = np.concatenate([a, a, a, np.zeros_like(a)], axis=-1)
    rs = np.stack([rs_f, rs_f[:, ::-1]])
    m = np.stack([m_f, m_f[:, ::-1, ::-1]])
    return a3, rs, m


def _hgrn_kernel(q_ref, ff_ref, fb_ref, v_ref, g_ref, lb_ref, gn_ref, a3_ref, rs_ref, mk_ref,
                 o_ref, acc_s, st_s, y_s, sc_s, dec_s, fw_s, inc_s, dec3_s, *, layer_j, nchunk):
    L = CHUNK
    nlev = len(_LEVELS)
    f_refs = (ff_ref, fb_ref)
    Y_QPRE, Y_KSUF, Y_Q, Y_K = nlev, nlev + 1, nlev + 2, nlev + 3
    ns = HGRN_STEPS_PER_TRIP

    lbr = lb_ref[...]
    e = jnp.exp(lbr - jnp.max(lbr, axis=1, keepdims=True))
    soft = e / jnp.sum(e, axis=1, keepdims=True)
    cum = soft[:, 0:1, :]
    for i in range(1, layer_j + 1):
        cum = cum + soft[:, i:i + 1, :]
    lb = cum - soft[:, 0:1, :]
    lb_floor = jnp.maximum(lb, LB_FLOOR)
    one_m_lb = 1.0 - lb

    st_s[...] = jnp.zeros_like(st_s)

    def chunk_rows(d, step):
        step = jnp.minimum(step, nchunk - 1)
        c = step if d == 0 else nchunk - 1 - step
        return pl.ds(pl.multiple_of(c * L, L), L)

    def stage1(d, step0, slot0):
        qs, ks, parts = [], [], []
        for par in range(2):
            rows = chunk_rows(d, step0 + par)
            qs.append(q_ref[rows, :].astype(F32))
            f = lb_floor[d] + one_m_lb[d] * _sigmoid(f_refs[d][rows, :].astype(F32))
            ks.append(1.0 - f)
            hi, mid, lo = _split3(jnp.log(f) * LOG2_E)
            parts.append(jnp.concatenate([hi, mid, lo, jnp.zeros_like(hi)], axis=0))
        dall = _dot(a3_ref[d], jnp.concatenate(parts, axis=1))
        for half in range(2):
            slot = slot0 + half
            q, k = qs[half], ks[half]
            eall = jnp.exp2(dall[:, half * HEAD:(half + 1) * HEAD])
            for li, c in enumerate(_LEVELS):
                if c % 8 == 0:
                    first_q = 1 if d == 0 else 0
                    x = jnp.concatenate([(q if b % 2 == first_q else k)[b * c:(b + 1) * c] for b in range(L // c)],
                                        axis=0)
                else:
                    x = jnp.where(rs_ref[d, li] > 0.5, q, k)
                y_s[slot, d, li] = (x * eall[li * L:(li + 1) * L]).astype(BF16)
            e_pre = eall[nlev * L:(nlev + 1) * L]
            e_suf = eall[(nlev + 1) * L:(nlev + 2) * L]
            y_s[slot, d, Y_QPRE] = (q * e_pre).astype(BF16)
            y_s[slot, d, Y_KSUF] = (k * e_suf).astype(BF16)
            y_s[slot, d, Y_Q] = q.astype(BF16)
            y_s[slot, d, Y_K] = k.astype(BF16)
            last = L - 1 if d == 0 else 0
            dec_s[slot, d] = jnp.broadcast_to(e_pre[last:last + 1, :], (8, HEAD))

    def stage2(d, step, slot):
        scores = _dot_nt(y_s[slot, d, Y_Q], y_s[slot, d, Y_K]) * mk_ref[d, nlev]
        for li in range(nlev):
            y = y_s[slot, d, li]
            scores = scores + _dot_nt(y, y) * mk_ref[d, li]
        sc_s[slot, d] = scores.astype(BF16)
        fw_s[slot, d] = y_s[slot, d, Y_QPRE]
        inc_s[slot, d] = _dot_tn(v_ref[chunk_rows(d, step), :].astype(BF16), y_s[slot, d, Y_KSUF])
        dec3_s[slot, d] = dec_s[slot, d]

    def stage3(d, step, slot):
        rows = chunk_rows(d, step)
        st = st_s[d]
        o = _dot(sc_s[slot, d], v_ref[rows, :].astype(BF16)) + _dot_nt(fw_s[slot, d], st.astype(BF16))
        st_s[d] = st * dec3_s[slot, d][0:1, :] + inc_s[slot, d]
        acc_s[rows, :] = acc_s[rows, :] + o

    def body(it, carry):
        for par in range(ns):
            for d in range(2):
                stage3(d, ns * it + par, par)
        for par in range(ns):
            for d in range(2):
                stage2(d, ns * it + ns + par, par)
        for par in range(0, ns, 2):
            for d in range(2):
                stage1(d, ns * it + 2 * ns + par, par)
        return carry

    acc_s[...] = jnp.zeros_like(acc_s)
    for par in range(0, ns, 2):
        for d in range(2):
            stage1(d, par, par)
    for par in range(ns):
        for d in range(2):
            stage2(d, par, par)
    for par in range(0, ns, 2):
        for d in range(2):
            stage1(d, ns + par, par)
    lax.fori_loop(0, nchunk // ns, body, 0, unroll=4)

    gn = gn_ref[...]
    blk = 8 * L

    def fin(i, carry):
        rows = pl.ds(pl.multiple_of(i * blk, blk), blk)
        o = acc_s[rows, :]
        g = g_ref[rows, :].astype(F32)
        o = o * lax.rsqrt(jnp.mean(o * o, axis=-1, keepdims=True) + GN_EPS)
        o_ref[rows, :] = (o * gn * (g * _sigmoid(g))).astype(o_ref.dtype)
        return carry

    lax.fori_loop(0, (nchunk * L) // blk, fin, 0, unroll=2)


def _hgrn(proj, lb_raw, gn, *, batch, seq, layer_j):
    nchunk = seq // CHUNK
    ns = HGRN_STEPS_PER_TRIP
    assert nchunk % ns == 0 and seq % (8 * CHUNK) == 0
    a3, rs, mk = _hgrn_constants()
    n_even = lb_raw.shape[1]
    nh = N_REC_HEADS

    def col(kind):
        return pl.BlockSpec((seq, HEAD), lambda b, h: (b, kind * nh + h))

    const3 = lambda b, h: (0, 0, 0)
    const4 = lambda b, h: (0, 0, 0, 0)
    return pl.pallas_call(
        functools.partial(_hgrn_kernel, layer_j=layer_j, nchunk=nchunk),
        grid=(batch, nh),
        in_specs=[col(0), col(1), col(2), col(3), col(4),
                  pl.BlockSpec((2, n_even, HEAD), lambda b, h: (0, 0, h)),
                  pl.BlockSpec((1, HEAD), lambda b, h: (0, h)),
                  pl.BlockSpec(a3.shape, const3),
                  pl.BlockSpec(rs.shape, const4),
                  pl.BlockSpec(mk.shape, const4)],
        out_specs=pl.BlockSpec((seq, HEAD), lambda b, h: (b, h)),
        out_shape=jax.ShapeDtypeStruct((batch * seq, nh * HEAD), BF16),
        scratch_shapes=[pltpu.VMEM((seq, HEAD), F32), pltpu.VMEM((2, HEAD, HEAD), F32),
                        pltpu.VMEM((ns, 2, len(_LEVELS) + 4, CHUNK, HEAD), BF16),
                        pltpu.VMEM((ns, 2, CHUNK, CHUNK), BF16),
                        pltpu.VMEM((ns, 2, 8, HEAD), F32),
                        pltpu.VMEM((ns, 2, CHUNK, HEAD), BF16),
                        pltpu.VMEM((ns, 2, HEAD, HEAD), F32),
                        pltpu.VMEM((ns, 2, 8, HEAD), F32)],
        compiler_params=_cparams(2),
    )(proj, proj, proj, proj, proj, lb_raw, gn.reshape(1, -1),
      jnp.asarray(a3, BF16), jnp.asarray(rs), jnp.asarray(mk))


def _mlstm_constants():
    L = CHUNK
    t = np.arange(L)
    ut = (t[:, None] <= t[None, :]).astype(np.float32)
    cum = np.stack([ut, ut[::-1, ::-1]])
    tril = (t[None, :] <= t[:, None]).astype(np.float32)
    causal = np.stack([tril, tril[::-1, ::-1]])
    sel = np.zeros((HEAD, 4 * GATE_GROUP * HEAD), np.float32)
    for q in range(4):
        for p in range(3):
            for j in range(GATE_GROUP):
                sel[(q * 3 + p) * GATE_GROUP + j, (q * GATE_GROUP + j) * HEAD:(q * GATE_GROUP + j + 1) * HEAD] = 1.0
    return cum, causal, sel


def _mlstm_kernel(xq_ref, xk_ref, v_ref, og_ref, gates_ref, gb_ref, cwq_ref, cwk_ref, gn_ref,
                  cum_ref, cm_ref, sel_ref, o_ref, q_s, k_s, acc_s, b_s, c_s, cma_s,
                  bl_s, gm_s, mst_s, mnew_s, wold_s, qkw_s, inc_s, nd_s, wi_s, winter_s, wsc_s, floor_s,
                  *, nchunk):
    L = CHUNK
    pad = CONV_W // 2
    halo = 16

    def conv_chunk(c, carry):
        rows = pl.ds(pl.multiple_of(c * L, L), L)
        prev = pl.ds(pl.multiple_of(jnp.maximum(c * L - halo, 0), halo), halo)
        nxt = pl.ds(pl.multiple_of(jnp.minimum(c * L + L, nchunk * L - halo), halo), halo)
        has_prev = jnp.where(c > 0, 1.0, 0.0).astype(F32)
        has_next = jnp.where(c < nchunk - 1, 1.0, 0.0).astype(F32)
        for x_ref, w_ref, dst, scale in ((xq_ref, cwq_ref, q_s, 1.0), (xk_ref, cwk_ref, k_s, HEAD ** -0.5)):
            win = jnp.concatenate([x_ref[prev, :].astype(F32) * has_prev, x_ref[rows, :].astype(F32),
                                   x_ref[nxt, :].astype(F32) * has_next], axis=0)
            w = w_ref[...]
            acc = win[halo - pad:halo - pad + L] * w[0:1, :]
            for j in range(1, CONV_W):
                acc = acc + win[halo - pad + j:halo - pad + j + L] * w[j:j + 1, :]
            y = acc * _sigmoid(acc)
            dst[rows, :] = y * scale if scale != 1.0 else y
        return carry

    lax.fori_loop(0, nchunk, conv_chunk, 0, unroll=2)

    for d in range(2):
        lf2 = _log_sigmoid(gates_ref[2 + d] + gb_ref[2 + d])
        hi, mid, lo = _split3(lf2)
        cm = cum_ref[d]
        b2 = _dot(hi, cm) + _dot(mid, cm) + _dot(lo, cm)
        b_s[d] = b2
        last = L - 1 if d == 0 else 0
        bl_s[d] = b2[:, last:last + 1]
        li2 = gates_ref[d] + gb_ref[d]
        gm_s[d] = jnp.max(b2[:, last:last + 1] - b2 + li2, axis=1, keepdims=True)
        x = jnp.concatenate([li2 - b2, jnp.full((nchunk, HEAD - L), NEG_BIG, F32)], axis=1)
        lane_x = lax.broadcasted_iota(jnp.int32, x.shape, 1)
        sh = 1
        while sh < L:
            if d == 0:
                x = jnp.maximum(x, jnp.where(lane_x >= sh, pltpu.roll(x, sh, axis=1), NEG_BIG))
            else:
                x = jnp.maximum(x, pltpu.roll(x, HEAD - sh, axis=1))
            sh *= 2
        cma_s[d] = x[:, :L]

    c_s[...] = jnp.zeros_like(c_s)
    lane = lax.broadcasted_iota(jnp.int32, (L, HEAD), 1)
    ones_col = (lane == 0).astype(BF16)

    def stab_step(step, m):
        new = []
        for d in range(2):
            c = step if d == 0 else nchunk - 1 - step
            sl = pl.ds(c, 1)
            mst_s[d, sl, :] = m[d]
            m_new = jnp.maximum(bl_s[d, sl, :] + m[d], gm_s[d, sl, :])
            mnew_s[d, sl, :] = m_new
            new.append(m_new)
        return tuple(new)

    lax.fori_loop(0, nchunk, stab_step, (jnp.zeros((1, 1), F32), jnp.zeros((1, 1), F32)))

    G = GATE_GROUP
    causal = [cm_ref[d] > 0.5 for d in range(2)]

    def gate_weights(g, carry):
        sl = pl.ds(pl.multiple_of(g * G, G), G)
        for d in range(2):
            br = b_s[d, sl, :]
            li = gates_ref[d, sl, :] + gb_ref[d]
            a = li - br
            m_st = mst_s[d, sl, :]
            m_new = mnew_s[d, sl, :]
            last = L - 1 if d == 0 else 0
            b_last = br[:, last:last + 1]
            mx = jnp.maximum(m_st, cma_s[d, sl, :])
            quantities = (mx, jnp.exp(m_st - mx), jnp.exp(-(br + mx)), jnp.exp(b_last - br + li - m_new))
            parts = [p.astype(F32) for qty in quantities for p in _split3(qty)]
            parts.append(jnp.zeros((HEAD - len(parts) * G, L), F32))
            cols = _dot_tn(jnp.concatenate(parts, axis=0).astype(BF16), sel_ref[...])
            for j in range(G):
                rows = pl.ds(pl.multiple_of((g * G + j) * L, L), L)
                mx_col, winter, floor, wsc = (cols[:, (q * G + j) * HEAD:(q * G + j + 1) * HEAD] for q in range(4))
                am = jnp.where(causal[d], jnp.broadcast_to(a[j:j + 1], (L, L)), NEG_BIG)
                wi_s[d, rows, :] = jnp.exp(am - mx_col[:, :L])
                winter_s[d, rows, :] = winter
                floor_s[d, rows, :] = floor[:, 0:1]
                wsc_s[d, rows, :] = wsc
            wold_s[d, sl, :] = jnp.exp(b_last + m_st - m_new)
        return carry

    lax.fori_loop(0, nchunk // G, gate_weights, 0, unroll=4)

    def chunk_index(d, step):
        step = jnp.minimum(step, nchunk - 1)
        return step if d == 0 else nchunk - 1 - step

    def chunk_rows(d, step):
        return pl.ds(pl.multiple_of(chunk_index(d, step) * L, L), L)

    def value_aug(rows):
        return jnp.concatenate([v_ref[rows, :], ones_col], axis=1)

    def stage_a(d, step, slot):
        rows = chunk_rows(d, step)
        k = k_s[rows, :]
        qkw_s[slot, d] = (_dot_nt(q_s[rows, :].astype(BF16), k.astype(BF16)) * wi_s[d, rows, :]).astype(BF16)
        inc_s[slot, d] = _dot_tn((k * wsc_s[d, rows, :]).astype(BF16), value_aug(rows))

    def stage_b(d, step, slot):
        rows = chunk_rows(d, step)
        cst = c_s[d]
        w_inter = winter_s[d, rows, :]
        nd_s[slot, d] = (_dot(qkw_s[slot, d], value_aug(rows))
                         + jnp.concatenate([w_inter, w_inter], axis=1)
                         * _dot(q_s[rows, :].astype(BF16), cst.astype(BF16)))
        c_s[d] = wold_s[d, pl.ds(chunk_index(d, step), 1), :] * cst + inc_s[slot, d]

    def stage_c(d, step, slot):
        rows = chunk_rows(d, step)
        nd = nd_s[slot, d]
        h = nd[:, :HEAD] / jnp.maximum(jnp.abs(nd[:, HEAD:HEAD + 1]), floor_s[d, rows, :])
        acc_s[rows, :] = acc_s[rows, :] + h

    ns = MLSTM_STEPS_PER_TRIP

    def body(it, with_c):
        for par in range(ns):
            for d in range(2):
                if with_c:
                    stage_c(d, ns * it - ns + par, par)
        for par in range(ns):
            for d in range(2):
                stage_b(d, ns * it + par, par)
        for par in range(ns):
            for d in range(2):
                stage_a(d, ns * it + ns + par, par)

    def loop_body(it, carry):
        body(it, True)
        return carry

    acc_s[...] = jnp.zeros_like(acc_s)
    for d in range(2):
        for par in range(ns):
            stage_a(d, par, par)
    body(0, False)
    lax.fori_loop(1, nchunk // ns, loop_body, 0, unroll=5)
    for d in range(2):
        for par in range(ns):
            stage_c(d, nchunk - ns + par, par)

    gn = gn_ref[...]
    blk = 8 * L

    def fin(i, carry):
        rows = pl.ds(pl.multiple_of(i * blk, blk), blk)
        h = acc_s[rows, :]
        mu = jnp.mean(h, axis=-1, keepdims=True)
        cen = h - mu
        hn = cen * lax.rsqrt(jnp.mean(cen * cen, axis=-1, keepdims=True) + GN_EPS)
        o_ref[rows, :] = (hn * gn * _sigmoid(og_ref[rows, :].astype(F32))).astype(o_ref.dtype)
        return carry

    lax.fori_loop(0, (nchunk * L) // blk, fin, 0, unroll=2)


def _mlstm(proj, gates, gate_bias, conv_w, gn, *, batch, seq):
    nchunk = seq // CHUNK
    ns = MLSTM_STEPS_PER_TRIP
    assert nchunk % GATE_GROUP == 0 and nchunk % ns == 0 and seq % (8 * CHUNK) == 0
    nh = N_REC_HEADS
    width = nh * HEAD
    cum, causal, sel = _mlstm_constants()
    g5 = gates.reshape(batch, nchunk, CHUNK, 4, nh).transpose(0, 4, 3, 1, 2)
    gb = jnp.broadcast_to(gate_bias.reshape(4, nh).T[:, :, None, None], (nh, 4, 1, CHUNK))
    cw = jnp.pad(conv_w, ((0, 8 - CONV_W), (0, 0)))
    first_col = 5 * nh

    def col(kind):
        return pl.BlockSpec((seq, HEAD), lambda b, h: (b, first_col + kind * nh + h))

    const2 = lambda b, h: (0, 0)
    const3 = lambda b, h: (0, 0, 0)
    return pl.pallas_call(
        functools.partial(_mlstm_kernel, nchunk=nchunk),
        grid=(batch, nh),
        in_specs=[col(0), col(1), col(2), col(3),
                  pl.BlockSpec((None, None, 4, nchunk, CHUNK), lambda b, h: (b, h, 0, 0, 0)),
                  pl.BlockSpec((None, 4, 1, CHUNK), lambda b, h: (h, 0, 0, 0)),
                  pl.BlockSpec((8, HEAD), lambda b, h: (0, h)),
                  pl.BlockSpec((8, HEAD), lambda b, h: (0, nh + h)),
                  pl.BlockSpec((1, HEAD), lambda b, h: (0, h)),
                  pl.BlockSpec(cum.shape, const3),
                  pl.BlockSpec(causal.shape, const3),
                  pl.BlockSpec(sel.shape, const2)],
        out_specs=pl.BlockSpec((seq, HEAD), lambda b, h: (b, h)),
        out_shape=jax.ShapeDtypeStruct((batch * seq, width), BF16),
        scratch_shapes=[pltpu.VMEM((seq, HEAD), F32), pltpu.VMEM((seq, HEAD), F32),
                        pltpu.VMEM((seq, HEAD), F32), pltpu.VMEM((2, nchunk, CHUNK), F32),
                        pltpu.VMEM((2, HEAD, 2 * HEAD), F32), pltpu.VMEM((2, nchunk, CHUNK), F32)]
                       + [pltpu.VMEM((2, nchunk, 1), F32)] * 5
                       + [pltpu.VMEM((ns, 2, CHUNK, CHUNK), BF16), pltpu.VMEM((ns, 2, HEAD, 2 * HEAD), F32),
                          pltpu.VMEM((ns, 2, CHUNK, 2 * HEAD), F32),
                          pltpu.VMEM((2, seq, CHUNK), F32), pltpu.VMEM((2, seq, HEAD), F32),
                          pltpu.VMEM((2, seq, HEAD), F32), pltpu.VMEM((2, seq, 1), F32)],
        compiler_params=_cparams(2),
    )(proj, proj, proj, proj, g5, gb, cw, cw, gn.reshape(1, -1),
      jnp.asarray(cum, BF16), jnp.asarray(causal), jnp.asarray(sel, BF16))


def _na_bias_tables(rpb):
    w = GRID_W
    qc = np.arange(w)[:, None]
    kc = np.arange(w)[None, :]
    c0 = np.clip(qc - WIN_C // 2, 0, w - WIN_C)
    valid = (kc >= c0) & (kc < c0 + WIN_C)
    cidx = np.clip(kc - qc + WIN_C - 1, 0, 2 * WIN_C - 2)
    tiles = jnp.where(jnp.asarray(valid)[None, None], rpb.astype(F32)[:, :, cidx] * LOG2_E, NEG_BIG)
    return jnp.concatenate([tiles[:, :-1], tiles[:, 1:]], axis=-1)


def _na_kernel(q_ref, k_ref, v_ref, bias_ref, o_ref, s_even, s_odd, *, n_rows):
    w = GRID_W
    nk = WIN_R * w
    lane_head = lax.broadcasted_iota(jnp.int32, (w, HEAD), 1) // NA_DH
    head_masks = [lane_head == hh for hh in range(NA_GROUP)]

    def window_start(r):
        return jnp.clip(r - WIN_R // 2, 0, n_rows - WIN_R)

    def logits(r, dst):
        q = q_ref[pl.ds(pl.multiple_of(r * w, w), w), :]
        zero = jnp.zeros_like(q)
        qm = jnp.concatenate([jnp.where(head_masks[hh], q, zero) for hh in range(NA_GROUP)], axis=0)
        r0 = window_start(r)
        s = _dot_nt(qm, k_ref[pl.ds(pl.multiple_of(r0 * w, w), nk), :])
        first = WIN_R - 1 - (r - r0)
        dst[...] = s + jnp.concatenate(
            [jnp.concatenate([bias_ref[hh, first + 2 * m] for m in range(WIN_R // 2)], axis=1)
             for hh in range(NA_GROUP)], axis=0)

    def attend(r, src):
        s = src[...]
        pexp = jnp.exp2(s - jnp.max(s, axis=1, keepdims=True))
        l = jnp.sum(pexp, axis=1, keepdims=True)
        r0 = window_start(r)
        o = _dot(pexp.astype(BF16), v_ref[pl.ds(pl.multiple_of(r0 * w, w), nk), :]) / l
        out = jnp.where(head_masks[0], o[0:w], 0.0)
        for hh in range(1, NA_GROUP):
            out = out + jnp.where(head_masks[hh], o[hh * w:(hh + 1) * w], 0.0)
        o_ref[pl.ds(pl.multiple_of(r * w, w), w), :] = out.astype(o_ref.dtype)

    logits(0, s_even)

    def two_rows(k, carry):
        r = 2 * k
        logits(r + 1, s_odd)
        attend(r, s_even)
        logits(jnp.minimum(r + 2, n_rows - 1), s_even)
        attend(r + 1, s_odd)
        return carry

    lax.fori_loop(0, n_rows // 2, two_rows, 0, unroll=8)


def _neighbourhood_attention(qkv, bias, *, batch, seq):
    n_rows = seq // GRID_W
    assert n_rows % 2 == 0 and n_rows >= WIN_R
    d_model = qkv.shape[1] // 3
    n_groups = d_model // HEAD
    return pl.pallas_call(
        functools.partial(_na_kernel, n_rows=n_rows),
        grid=(batch, n_groups),
        in_specs=[pl.BlockSpec((seq, HEAD), lambda b, g: (b, g)),
                  pl.BlockSpec((seq, HEAD), lambda b, g: (b, n_groups + g)),
                  pl.BlockSpec((seq, HEAD), lambda b, g: (b, 2 * n_groups + g)),
                  pl.BlockSpec((NA_GROUP, 2 * WIN_R - 2, GRID_W, 2 * GRID_W), lambda b, g: (g, 0, 0, 0))],
        out_specs=pl.BlockSpec((seq, HEAD), lambda b, g: (b, g)),
        out_shape=jax.ShapeDtypeStruct((batch * seq, d_model), BF16),
        scratch_shapes=[pltpu.VMEM((NA_GROUP * GRID_W, WIN_R * GRID_W), F32)] * 2,
        compiler_params=_cparams(2),
    )(qkv, qkv, qkv, bias)


def _ffn_kernel(*refs, n_mix, alpha):
    x_ref = refs[0]
    mix_refs = refs[1:1 + 2 * n_mix]
    lmg_ref, lmb_ref, lfg_ref, lfb_ref, wg_ref, wu_ref, wd_ref, o_ref = refs[1 + 2 * n_mix:]
    half = x_ref.shape[0] // 2
    rows = (slice(0, half), slice(half, 2 * half))

    def pre(r):
        mix = _dot(mix_refs[0][r, :], mix_refs[1][...])
        for i in range(1, n_mix):
            mix = mix + _dot(mix_refs[2 * i][r, :], mix_refs[2 * i + 1][...])
        return _layer_norm(alpha * x_ref[r, :] + mix, lmg_ref[...], lmb_ref[...])

    def act(h):
        hb = h.astype(BF16)
        g = _dot(hb, wg_ref[...])
        u = _dot(hb, wu_ref[...])
        return (g * _sigmoid(g) * u).astype(BF16)

    h0 = pre(rows[0])
    a0 = act(h0)
    h1 = pre(rows[1])
    y0 = _dot(a0, wd_ref[...])
    a1 = act(h1)
    o_ref[rows[0], :] = _layer_norm(alpha * h0 + y0, lfg_ref[...], lfb_ref[...])
    y1 = _dot(a1, wd_ref[...])
    o_ref[rows[1], :] = _layer_norm(alpha * h1 + y1, lfg_ref[...], lfb_ref[...])


def _mixer_out_ffn(x, mix_pairs, ln_mix_g, ln_mix_b, ln_ffn_g, ln_ffn_b, wg, wu, wd, *, alpha, tm):
    t, d = x.shape
    assert t % tm == 0
    row = lambda i: (i, 0)
    in_specs = [pl.BlockSpec((tm, d), row)]
    args = [x]
    for o, w in mix_pairs:
        in_specs += [pl.BlockSpec((tm, o.shape[1]), row), _resident(w.shape)]
        args += [o, w]
    in_specs += [_resident((1, d))] * 4
    args += [ln_mix_g.reshape(1, d), ln_mix_b.reshape(1, d), ln_ffn_g.reshape(1, d), ln_ffn_b.reshape(1, d)]
    in_specs += [_resident(wg.shape), _resident(wu.shape), _resident(wd.shape)]
    args += [wg, wu, wd]
    return pl.pallas_call(
        functools.partial(_ffn_kernel, n_mix=len(mix_pairs), alpha=alpha),
        grid=(t // tm,),
        in_specs=in_specs,
        out_specs=pl.BlockSpec((tm, d), row),
        out_shape=jax.ShapeDtypeStruct((t, d), F32),
        compiler_params=_cparams(1),
    )(*args)


def _row_tile(t):
    for tm in (1024, 512, 256, 128):
        if t % tm == 0:
            return tm
    raise ValueError(f"token count {t} is not a multiple of 128")


def kernel(x, w_in_even, gate_bias_even, lb_raw, conv_qk, gn_hgrn, gn_mlstm, w_out_even, w_qkv_odd, rpb_odd,
           w_out_odd, ln_mix_g, ln_mix_b, ln_ffn_g, ln_ffn_b, w_ffn_gate, w_ffn_up, w_ffn_down):
    batch, seq, d_model = x.shape
    depth = ln_mix_g.shape[0]
    alpha = (2.0 * depth) ** 0.25
    t = batch * seq
    tm = min(_row_tile(t), 512)
    a_width = N_REC_HEADS * HEAD
    main_cols = 9 * a_width
    n_gate = 4 * N_REC_HEADS

    h = x.reshape(t, d_model)
    for layer in range(depth):
        j = layer // 2
        if layer % 2 == 0:
            w_in = w_in_even[j]
            w_main = w_in[:, :main_cols].astype(BF16)
            w_gate = jnp.pad(w_in[:, main_cols:], ((0, 0), (0, V7X_LANES - n_gate))).astype(BF16)
            proj, gates = _in_projection(h, w_main, w_gate, tm=tm)
            gates = gates[:, :n_gate]
            o_a = _hgrn(proj, lb_raw, gn_hgrn[j], batch=batch, seq=seq, layer_j=j)
            h_b = _mlstm(proj, gates, gate_bias_even[j], conv_qk[j], gn_mlstm[j], batch=batch, seq=seq)
            w_out = w_out_even[j].astype(BF16)
            mix_pairs = [(o_a, w_out[:a_width]), (h_b, w_out[a_width:])]
        else:
            qkv = _qkv_projection(h, w_qkv_odd[j].astype(BF16), q_scale=NA_DH ** -0.5 * LOG2_E, tm=tm)
            o = _neighbourhood_attention(qkv, _na_bias_tables(rpb_odd[j]), batch=batch, seq=seq)
            mix_pairs = [(o, w_out_odd[j].astype(BF16))]
        h = _mixer_out_ffn(h, mix_pairs, ln_mix_g[layer], ln_mix_b[layer], ln_ffn_g[layer], ln_ffn_b[layer],
                           w_ffn_gate[layer].astype(BF16), w_ffn_up[layer].astype(BF16),
                           w_ffn_down[layer].astype(BF16), alpha=alpha, tm=tm)
    return h.reshape(batch, seq, d_model)
```

```python
import functools

import numpy as np
import jax
import jax.numpy as jnp
from jax import lax
from jax.experimental import pallas as pl
from jax.experimental.pallas import tpu as pltpu

F32 = jnp.float32
BF16 = jnp.bfloat16

GRID_W = 64
HEAD = 128
N_REC_HEADS = 4
CHUNK = 64
CONV_W = 5
HGRN_STEPS_PER_TRIP = 4
MLSTM_STEPS_PER_TRIP = 4
GATE_GROUP = 8
NA_DH = 32
NA_GROUP = HEAD // NA_DH
WIN_R = 8
WIN_C = 16
LN_EPS = 1e-5
GN_EPS = 1e-6
NEG_BIG = -1e30
LB_FLOOR = 1e-30
LOG2_E = 1.4426950408889634

V7X_LANES = 128
FFN_PART_ROWS = 256
V7X_VMEM_LIMIT_BYTES = 56 * 1024 * 1024

_LEVELS = (32, 16, 8, 4, 2, 1)


def _cparams(n_grid_axes):
    return pltpu.CompilerParams(
        dimension_semantics=("arbitrary",) * n_grid_axes,
        vmem_limit_bytes=V7X_VMEM_LIMIT_BYTES)


def _dot(a, b):
    return jnp.dot(a, b, preferred_element_type=F32)


def _dot_nt(a, b):
    return lax.dot_general(a, b, (((1,), (1,)), ((), ())), preferred_element_type=F32)


def _dot_tn(a, b):
    return lax.dot_general(a, b, (((0,), (0,)), ((), ())), preferred_element_type=F32)


def _split3(x):
    hi = x.astype(BF16)
    r1 = x - hi.astype(F32)
    mid = r1.astype(BF16)
    lo = (r1 - mid.astype(F32)).astype(BF16)
    return hi, mid, lo


def _log_sigmoid(z):
    return jnp.minimum(z, 0.0) - jnp.log1p(jnp.exp(-jnp.abs(z)))


def _sigmoid(z):
    return 1.0 / (1.0 + jnp.exp(-z))


def _layer_norm(t, g, b):
    mu = jnp.mean(t, axis=-1, keepdims=True)
    c = t - mu
    var = jnp.mean(c * c, axis=-1, keepdims=True)
    return c * lax.rsqrt(var + LN_EPS) * g + b


def _resident(shape):
    return pl.BlockSpec(shape, lambda i: (0,) * len(shape), pipeline_mode=pl.Buffered(1))


def _in_proj_kernel(x_ref, w_ref, wg_ref, o_ref, g_ref):
    xb = x_ref[...].astype(BF16)
    o_ref[...] = _dot(xb, w_ref[...]).astype(o_ref.dtype)
    g_ref[...] = _dot(xb, wg_ref[...])


def _in_projection(x, w_main, w_gate, *, tm):
    t, k = x.shape
    assert t % tm == 0
    return pl.pallas_call(
        _in_proj_kernel,
        grid=(t // tm,),
        in_specs=[pl.BlockSpec((tm, k), lambda i: (i, 0)), _resident(w_main.shape), _resident(w_gate.shape)],
        out_specs=[pl.BlockSpec((tm, w_main.shape[1]), lambda i: (i, 0)),
                   pl.BlockSpec((tm, w_gate.shape[1]), lambda i: (i, 0))],
        out_shape=[jax.ShapeDtypeStruct((t, w_main.shape[1]), BF16),
                   jax.ShapeDtypeStruct((t, w_gate.shape[1]), F32)],
        compiler_params=_cparams(1),
    )(x, w_main, w_gate)


def _qkv_kernel(x_ref, w_ref, o_ref, *, q_scale):
    acc = _dot(x_ref[...].astype(BF16), w_ref[...])
    d = acc.shape[1] // 3
    o_ref[:, :d] = (acc[:, :d] * q_scale).astype(o_ref.dtype)
    o_ref[:, d:] = acc[:, d:].astype(o_ref.dtype)


def _qkv_projection(x, w_qkv, *, q_scale, tm):
    t, k = x.shape
    assert t % tm == 0
    return pl.pallas_call(
        functools.partial(_qkv_kernel, q_scale=q_scale),
        grid=(t // tm,),
        in_specs=[pl.BlockSpec((tm, k), lambda i: (i, 0)), _resident(w_qkv.shape)],
        out_specs=pl.BlockSpec((tm, w_qkv.shape[1]), lambda i: (i, 0)),
        out_shape=jax.ShapeDtypeStruct((t, w_qkv.shape[1]), BF16),
        compiler_params=_cparams(1),
    )(x, w_qkv)


def _hgrn_constants():
    L = CHUNK
    t = np.arange(L)
    a_rows, rowsel, masks = [], [], []
    for c in _LEVELS:
        odd = (t // c) % 2 == 1
        rho = (t // (2 * c)) * 2 * c + c - 1
        u = t[None, :]
        a = np.where(odd[:, None], (u > rho[:, None]) & (u <= t[:, None]),
                     (u > t[:, None]) & (u <= rho[:, None]))
        a_rows.append(a.astype(np.float32))
        rowsel.append(np.broadcast_to(odd[:, None], (L, HEAD)).astype(np.float32))
        same = (t[:, None] // (2 * c)) == (t[None, :] // (2 * c))
        masks.append((odd[:, None] & ~odd[None, :] & same).astype(np.float32))
    masks.append(np.eye(L, dtype=np.float32))
    a_rows.append((t[None, :] <= t[:, None]).astype(np.float32))
    a_rows.append((t[None, :] > t[:, None]).astype(np.float32))
    a_f = np.stack(a_rows)
    rs_f = np.stack(rowsel)
    m_f = np.stack(masks)
    a = np.stack([a_f, a_f[:, ::-1, ::-1]]).reshape(2, 8 * L, L)
    a3 = np.concatenate([a, a, a, np.zeros_like(a)], axis=-1)
    rs = np.stack([rs_f, rs_f[:, ::-1]])
    m = np.stack([m_f, m_f[:, ::-1, ::-1]])
    return a3, rs, m


def _hgrn_kernel(q_ref, ff_ref, fb_ref, v_ref, g_ref, lb_ref, gn_ref, a3_ref, rs_ref, mk_ref,
                 o_ref, acc_s, st_s, y_s, sc_s, dec_s, fw_s, inc_s, dec3_s, *, layer_j, nchunk):
    L = CHUNK
    nlev = len(_LEVELS)
    f_refs = (ff_ref, fb_ref)
    Y_QPRE, Y_KSUF, Y_Q, Y_K = nlev, nlev + 1, nlev + 2, nlev + 3
    ns = HGRN_STEPS_PER_TRIP

    lbr = lb_ref[...]
    e = jnp.exp(lbr - jnp.max(lbr, axis=1, keepdims=True))
    soft = e / jnp.sum(e, axis=1, keepdims=True)
    cum = soft[:, 0:1, :]
    for i in range(1, layer_j + 1):
        cum = cum + soft[:, i:i + 1, :]
    lb = cum - soft[:, 0:1, :]
    lb_floor = jnp.maximum(lb, LB_FLOOR)
    one_m_lb = 1.0 - lb

    st_s[...] = jnp.zeros_like(st_s)

    def chunk_rows(d, step):
        step = jnp.minimum(step, nchunk - 1)
        c = step if d == 0 else nchunk - 1 - step
        return pl.ds(pl.multiple_of(c * L, L), L)

    def stage1(d, step0, slot0):
        qs, ks, parts = [], [], []
        for par in range(2):
            rows = chunk_rows(d, step0 + par)
            qs.append(q_ref[rows, :].astype(F32))
            f = lb_floor[d] + one_m_lb[d] * _sigmoid(f_refs[d][rows, :].astype(F32))
            ks.append(1.0 - f)
            hi, mid, lo = _split3(jnp.log(f) * LOG2_E)
            parts.append(jnp.concatenate([hi, mid, lo, jnp.zeros_like(hi)], axis=0))
        dall = _dot(a3_ref[d], jnp.concatenate(parts, axis=1))
        for half in range(2):
            slot = slot0 + half
            q, k = qs[half], ks[half]
            eall = jnp.exp2(dall[:, half * HEAD:(half + 1) * HEAD])
            for li, c in enumerate(_LEVELS):
                if c % 8 == 0:
                    first_q = 1 if d == 0 else 0
                    x = jnp.concatenate([(q if b % 2 == first_q else k)[b * c:(b + 1) * c] for b in range(L // c)],
                                        axis=0)
                else:
                    x = jnp.where(rs_ref[d, li] > 0.5, q, k)
                y_s[slot, d, li] = (x * eall[li * L:(li + 1) * L]).astype(BF16)
            e_pre = eall[nlev * L:(nlev + 1) * L]
            e_suf = eall[(nlev + 1) * L:(nlev + 2) * L]
            y_s[slot, d, Y_QPRE] = (q * e_pre).astype(BF16)
            y_s[slot, d, Y_KSUF] = (k * e_suf).astype(BF16)
            y_s[slot, d, Y_Q] = q.astype(BF16)
            y_s[slot, d, Y_K] = k.astype(BF16)
            last = L - 1 if d == 0 else 0
            dec_s[slot, d] = jnp.broadcast_to(e_pre[last:last + 1, :], (8, HEAD))

    def stage2(d, step, slot):
        scores = _dot_nt(y_s[slot, d, Y_Q], y_s[slot, d, Y_K]) * mk_ref[d, nlev]
        for li in range(nlev):
            y = y_s[slot, d, li]
            scores = scores + _dot_nt(y, y) * mk_ref[d, li]
        sc_s[slot, d] = scores.astype(BF16)
        fw_s[slot, d] = y_s[slot, d, Y_QPRE]
        inc_s[slot, d] = _dot_tn(v_ref[chunk_rows(d, step), :].astype(BF16), y_s[slot, d, Y_KSUF])
        dec3_s[slot, d] = dec_s[slot, d]

    def stage3(d, step, slot):
        rows = chunk_rows(d, step)
        st = st_s[d]
        o = _dot(sc_s[slot, d], v_ref[rows, :].astype(BF16)) + _dot_nt(fw_s[slot, d], st.astype(BF16))
        st_s[d] = st * dec3_s[slot, d][0:1, :] + inc_s[slot, d]
        acc_s[rows, :] = acc_s[rows, :] + o

    def body(it, carry):
        for par in range(ns):
            for d in range(2):
                stage3(d, ns * it + par, par)
        for par in range(ns):
            for d in range(2):
                stage2(d, ns * it + ns + par, par)
        for par in range(0, ns, 2):
            for d in range(2):
                stage1(d, ns * it + 2 * ns + par, par)
        return carry

    acc_s[...] = jnp.zeros_like(acc_s)
    for par in range(0, ns, 2):
        for d in range(2):
            stage1(d, par, par)
    for par in range(ns):
        for d in range(2):
            stage2(d, par, par)
    for par in range(0, ns, 2):
        for d in range(2):
            stage1(d, ns + par, par)
    lax.fori_loop(0, nchunk // ns, body, 0, unroll=4)

    gn = gn_ref[...]
    blk = 8 * L

    def fin(i, carry):
        rows = pl.ds(pl.multiple_of(i * blk, blk), blk)
        o = acc_s[rows, :]
        g = g_ref[rows, :].astype(F32)
        o = o * lax.rsqrt(jnp.mean(o * o, axis=-1, keepdims=True) + GN_EPS)
        o_ref[rows, :] = (o * gn * (g * _sigmoid(g))).astype(o_ref.dtype)
        return carry

    lax.fori_loop(0, (nchunk * L) // blk, fin, 0, unroll=4)


def _hgrn(proj, lb_raw, gn, *, batch, seq, layer_j):
    nchunk = seq // CHUNK
    ns = HGRN_STEPS_PER_TRIP
    assert nchunk % ns == 0 and seq % (8 * CHUNK) == 0
    a3, rs, mk = _hgrn_constants()
    n_even = lb_raw.shape[1]
    nh = N_REC_HEADS

    def col(kind):
        return pl.BlockSpec((seq, HEAD), lambda b, h: (b, kind * nh + h))

    const3 = lambda b, h: (0, 0, 0)
    const4 = lambda b, h: (0, 0, 0, 0)
    return pl.pallas_call(
        functools.partial(_hgrn_kernel, layer_j=layer_j, nchunk=nchunk),
        grid=(batch, nh),
        in_specs=[col(0), col(1), col(2), col(3), col(4),
                  pl.BlockSpec((2, n_even, HEAD), lambda b, h: (0, 0, h)),
                  pl.BlockSpec((1, HEAD), lambda b, h: (0, h)),
                  pl.BlockSpec(a3.shape, const3),
                  pl.BlockSpec(rs.shape, const4),
                  pl.BlockSpec(mk.shape, const4)],
        out_specs=pl.BlockSpec((seq, HEAD), lambda b, h: (b, h)),
        out_shape=jax.ShapeDtypeStruct((batch * seq, nh * HEAD), BF16),
        scratch_shapes=[pltpu.VMEM((seq, HEAD), F32), pltpu.VMEM((2, HEAD, HEAD), F32),
                        pltpu.VMEM((ns, 2, len(_LEVELS) + 4, CHUNK, HEAD), BF16),
                        pltpu.VMEM((ns, 2, CHUNK, CHUNK), BF16),
                        pltpu.VMEM((ns, 2, 8, HEAD), F32),
                        pltpu.VMEM((ns, 2, CHUNK, HEAD), BF16),
                        pltpu.VMEM((ns, 2, HEAD, HEAD), F32),
                        pltpu.VMEM((ns, 2, 8, HEAD), F32)],
        compiler_params=_cparams(2),
    )(proj, proj, proj, proj, proj, lb_raw, gn.reshape(1, -1),
      jnp.asarray(a3, BF16), jnp.asarray(rs), jnp.asarray(mk))


def _mlstm_constants():
    L = CHUNK
    t = np.arange(L)
    ut = (t[:, None] <= t[None, :]).astype(np.float32)
    cum = np.stack([ut, ut[::-1, ::-1]])
    tril = (t[None, :] <= t[:, None]).astype(np.float32)
    causal = np.stack([tril, tril[::-1, ::-1]])
    sel = np.zeros((HEAD, 4 * GATE_GROUP * HEAD), np.float32)
    for q in range(4):
        for p in range(3):
            for j in range(GATE_GROUP):
                sel[(q * 3 + p) * GATE_GROUP + j, (q * GATE_GROUP + j) * HEAD:(q * GATE_GROUP + j + 1) * HEAD] = 1.0
    return cum, causal, sel


def _mlstm_kernel(xq_ref, xk_ref, v_ref, og_ref, gates_ref, gb_ref, cwq_ref, cwk_ref, gn_ref,
                  cum_ref, cm_ref, sel_ref, o_ref, q_s, k_s, acc_s, b_s, c_s, cma_s,
                  bl_s, gm_s, mst_s, mnew_s, wold_s, qkw_s, inc_s, nd_s, wi_s, winter_s, wsc_s, floor_s,
                  *, nchunk):
    L = CHUNK
    pad = CONV_W // 2
    halo = 16

    def conv_chunk(c, carry):
        rows = pl.ds(pl.multiple_of(c * L, L), L)
        prev = pl.ds(pl.multiple_of(jnp.maximum(c * L - halo, 0), halo), halo)
        nxt = pl.ds(pl.multiple_of(jnp.minimum(c * L + L, nchunk * L - halo), halo), halo)
        has_prev = jnp.where(c > 0, 1.0, 0.0).astype(F32)
        has_next = jnp.where(c < nchunk - 1, 1.0, 0.0).astype(F32)
        for x_ref, w_ref, dst, scale in ((xq_ref, cwq_ref, q_s, 1.0), (xk_ref, cwk_ref, k_s, HEAD ** -0.5)):
            win = jnp.concatenate([x_ref[prev, :].astype(F32) * has_prev, x_ref[rows, :].astype(F32),
                                   x_ref[nxt, :].astype(F32) * has_next], axis=0)
            w = w_ref[...]
            acc = win[halo - pad:halo - pad + L] * w[0:1, :]
            for j in range(1, CONV_W):
                acc = acc + win[halo - pad + j:halo - pad + j + L] * w[j:j + 1, :]
            y = acc * _sigmoid(acc)
            dst[rows, :] = y * scale if scale != 1.0 else y
        return carry

    lax.fori_loop(0, nchunk, conv_chunk, 0, unroll=4)

    for d in range(2):
        lf2 = _log_sigmoid(gates_ref[2 + d] + gb_ref[2 + d])
        hi, mid, lo = _split3(lf2)
        cm = cum_ref[d]
        b2 = _dot(hi, cm) + _dot(mid, cm) + _dot(lo, cm)
        b_s[d] = b2
        last = L - 1 if d == 0 else 0
        bl_s[d] = b2[:, last:last + 1]
        li2 = gates_ref[d] + gb_ref[d]
        gm_s[d] = jnp.max(b2[:, last:last + 1] - b2 + li2, axis=1, keepdims=True)
        x = jnp.concatenate([li2 - b2, jnp.full((nchunk, HEAD - L), NEG_BIG, F32)], axis=1)
        lane_x = lax.broadcasted_iota(jnp.int32, x.shape, 1)
        sh = 1
        while sh < L:
            if d == 0:
                x = jnp.maximum(x, jnp.where(lane_x >= sh, pltpu.roll(x, sh, axis=1), NEG_BIG))
            else:
                x = jnp.maximum(x, pltpu.roll(x, HEAD - sh, axis=1))
            sh *= 2
        cma_s[d] = x[:, :L]

    c_s[...] = jnp.zeros_like(c_s)
    lane = lax.broadcasted_iota(jnp.int32, (L, HEAD), 1)
    ones_col = (lane == 0).astype(BF16)

    def stab_step(step, m):
        new = []
        for d in range(2):
            c = step if d == 0 else nchunk - 1 - step
            sl = pl.ds(c, 1)
            mst_s[d, sl, :] = m[d]
            m_new = jnp.maximum(bl_s[d, sl, :] + m[d], gm_s[d, sl, :])
            mnew_s[d, sl, :] = m_new
            new.append(m_new)
        return tuple(new)

    lax.fori_loop(0, nchunk, stab_step, (jnp.zeros((1, 1), F32), jnp.zeros((1, 1), F32)))

    G = GATE_GROUP
    causal = [cm_ref[d] > 0.5 for d in range(2)]

    def gate_weights(g, carry):
        sl = pl.ds(pl.multiple_of(g * G, G), G)
        for d in range(2):
            br = b_s[d, sl, :]
            li = gates_ref[d, sl, :] + gb_ref[d]
            a = li - br
            m_st = mst_s[d, sl, :]
            m_new = mnew_s[d, sl, :]
            last = L - 1 if d == 0 else 0
            b_last = br[:, last:last + 1]
            mx = jnp.maximum(m_st, cma_s[d, sl, :])
            quantities = (mx, jnp.exp(m_st - mx), jnp.exp(-(br + mx)), jnp.exp(b_last - br + li - m_new))
            parts = [p.astype(F32) for qty in quantities for p in _split3(qty)]
            parts.append(jnp.zeros((HEAD - len(parts) * G, L), F32))
            cols = _dot_tn(jnp.concatenate(parts, axis=0).astype(BF16), sel_ref[...])
            for j in range(G):
                rows = pl.ds(pl.multiple_of((g * G + j) * L, L), L)
                mx_col, winter, floor, wsc = (cols[:, (q * G + j) * HEAD:(q * G + j + 1) * HEAD] for q in range(4))
                am = jnp.where(causal[d], jnp.broadcast_to(a[j:j + 1], (L, L)), NEG_BIG)
                wi_s[d, rows, :] = jnp.exp(am - mx_col[:, :L])
                winter_s[d, rows, :] = winter
                floor_s[d, rows, :] = floor[:, 0:1]
                wsc_s[d, rows, :] = wsc
            wold_s[d, sl, :] = jnp.exp(b_last + m_st - m_new)
        return carry

    lax.fori_loop(0, nchunk // G, gate_weights, 0, unroll=8)

    def chunk_index(d, step):
        step = jnp.minimum(step, nchunk - 1)
        return step if d == 0 else nchunk - 1 - step

    def chunk_rows(d, step):
        return pl.ds(pl.multiple_of(chunk_index(d, step) * L, L), L)

    def value_aug(rows):
        return jnp.concatenate([v_ref[rows, :], ones_col], axis=1)

    def stage_a(d, step, slot):
        rows = chunk_rows(d, step)
        k = k_s[rows, :]
        qkw_s[slot, d] = (_dot_nt(q_s[rows, :].astype(BF16), k.astype(BF16)) * wi_s[d, rows, :]).astype(BF16)
        inc_s[slot, d] = _dot_tn((k * wsc_s[d, rows, :]).astype(BF16), value_aug(rows))

    def stage_b(d, step, slot):
        rows = chunk_rows(d, step)
        cst = c_s[d]
        w_inter = winter_s[d, rows, :]
        nd_s[slot, d] = (_dot(qkw_s[slot, d], value_aug(rows))
                         + jnp.concatenate([w_inter, w_inter], axis=1)
                         * _dot(q_s[rows, :].astype(BF16), cst.astype(BF16)))
        c_s[d] = wold_s[d, pl.ds(chunk_index(d, step), 1), :] * cst + inc_s[slot, d]

    def stage_c(d, step, slot):
        rows = chunk_rows(d, step)
        nd = nd_s[slot, d]
        h = nd[:, :HEAD] / jnp.maximum(jnp.abs(nd[:, HEAD:HEAD + 1]), floor_s[d, rows, :])
        acc_s[rows, :] = acc_s[rows, :] + h

    ns = MLSTM_STEPS_PER_TRIP

    def body(it, with_c):
        for par in range(ns):
            for d in range(2):
                if with_c:
                    stage_c(d, ns * it - ns + par, par)
        for par in range(ns):
            for d in range(2):
                stage_b(d, ns * it + par, par)
        for par in range(ns):
            for d in range(2):
                stage_a(d, ns * it + ns + par, par)

    def loop_body(it, carry):
        body(it, True)
        return carry

    acc_s[...] = jnp.zeros_like(acc_s)
    for d in range(2):
        for par in range(ns):
            stage_a(d, par, par)
    body(0, False)
    lax.fori_loop(1, nchunk // ns, loop_body, 0, unroll=15)
    for d in range(2):
        for par in range(ns):
            stage_c(d, nchunk - ns + par, par)

    gn = gn_ref[...]
    blk = 8 * L

    def fin(i, carry):
        rows = pl.ds(pl.multiple_of(i * blk, blk), blk)
        h = acc_s[rows, :]
        mu = jnp.mean(h, axis=-1, keepdims=True)
        cen = h - mu
        hn = cen * lax.rsqrt(jnp.mean(cen * cen, axis=-1, keepdims=True) + GN_EPS)
        o_ref[rows, :] = (hn * gn * _sigmoid(og_ref[rows, :].astype(F32))).astype(o_ref.dtype)
        return carry

    lax.fori_loop(0, (nchunk * L) // blk, fin, 0, unroll=4)


def _mlstm(proj, gates, gate_bias, conv_w, gn, *, batch, seq):
    nchunk = seq // CHUNK
    ns = MLSTM_STEPS_PER_TRIP
    assert nchunk % GATE_GROUP == 0 and nchunk % ns == 0 and seq % (8 * CHUNK) == 0
    nh = N_REC_HEADS
    width = nh * HEAD
    cum, causal, sel = _mlstm_constants()
    g5 = gates.reshape(batch, nchunk, CHUNK, 4, nh).transpose(0, 4, 3, 1, 2)
    gb = jnp.broadcast_to(gate_bias.reshape(4, nh).T[:, :, None, None], (nh, 4, 1, CHUNK))
    cw = jnp.pad(conv_w, ((0, 8 - CONV_W), (0, 0)))
    first_col = 5 * nh

    def col(kind):
        return pl.BlockSpec((seq, HEAD), lambda b, h: (b, first_col + kind * nh + h))

    const2 = lambda b, h: (0, 0)
    const3 = lambda b, h: (0, 0, 0)
    return pl.pallas_call(
        functools.partial(_mlstm_kernel, nchunk=nchunk),
        grid=(batch, nh),
        in_specs=[col(0), col(1), col(2), col(3),
                  pl.BlockSpec((None, None, 4, nchunk, CHUNK), lambda b, h: (b, h, 0, 0, 0)),
                  pl.BlockSpec((None, 4, 1, CHUNK), lambda b, h: (h, 0, 0, 0)),
                  pl.BlockSpec((8, HEAD), lambda b, h: (0, h)),
                  pl.BlockSpec((8, HEAD), lambda b, h: (0, nh + h)),
                  pl.BlockSpec((1, HEAD), lambda b, h: (0, h)),
                  pl.BlockSpec(cum.shape, const3),
                  pl.BlockSpec(causal.shape, const3),
                  pl.BlockSpec(sel.shape, const2)],
        out_specs=pl.BlockSpec((seq, HEAD), lambda b, h: (b, h)),
        out_shape=jax.ShapeDtypeStruct((batch * seq, width), BF16),
        scratch_shapes=[pltpu.VMEM((seq, HEAD), F32), pltpu.VMEM((seq, HEAD), F32),
                        pltpu.VMEM((seq, HEAD), F32), pltpu.VMEM((2, nchunk, CHUNK), F32),
                        pltpu.VMEM((2, HEAD, 2 * HEAD), F32), pltpu.VMEM((2, nchunk, CHUNK), F32)]
                       + [pltpu.VMEM((2, nchunk, 1), F32)] * 5
                       + [pltpu.VMEM((ns, 2, CHUNK, CHUNK), BF16), pltpu.VMEM((ns, 2, HEAD, 2 * HEAD), F32),
                          pltpu.VMEM((ns, 2, CHUNK, 2 * HEAD), F32),
                          pltpu.VMEM((2, seq, CHUNK), F32), pltpu.VMEM((2, seq, HEAD), F32),
                          pltpu.VMEM((2, seq, HEAD), F32), pltpu.VMEM((2, seq, 1), F32)],
        compiler_params=_cparams(2),
    )(proj, proj, proj, proj, g5, gb, cw, cw, gn.reshape(1, -1),
      jnp.asarray(cum, BF16), jnp.asarray(causal), jnp.asarray(sel, BF16))


def _na_bias_tables(rpb):
    w = GRID_W
    qc = np.arange(w)[:, None]
    kc = np.arange(w)[None, :]
    c0 = np.clip(qc - WIN_C // 2, 0, w - WIN_C)
    valid = (kc >= c0) & (kc < c0 + WIN_C)
    cidx = np.clip(kc - qc + WIN_C - 1, 0, 2 * WIN_C - 2)
    tiles = jnp.where(jnp.asarray(valid)[None, None], rpb.astype(F32)[:, :, cidx] * LOG2_E, NEG_BIG)
    return jnp.concatenate([tiles[:, :-1], tiles[:, 1:]], axis=-1)


def _na_kernel(q_ref, k_ref, v_ref, bias_ref, o_ref, s_even, s_odd, *, n_rows):
    w = GRID_W
    nk = WIN_R * w
    lane_head = lax.broadcasted_iota(jnp.int32, (w, HEAD), 1) // NA_DH
    head_masks = [lane_head == hh for hh in range(NA_GROUP)]

    def window_start(r):
        return jnp.clip(r - WIN_R // 2, 0, n_rows - WIN_R)

    def logits(r, dst):
        q = q_ref[pl.ds(pl.multiple_of(r * w, w), w), :]
        zero = jnp.zeros_like(q)
        qm = jnp.concatenate([jnp.where(head_masks[hh], q, zero) for hh in range(NA_GROUP)], axis=0)
        r0 = window_start(r)
        s = _dot_nt(qm, k_ref[pl.ds(pl.multiple_of(r0 * w, w), nk), :])
        first = WIN_R - 1 - (r - r0)
        dst[...] = s + jnp.concatenate(
            [jnp.concatenate([bias_ref[hh, first + 2 * m] for m in range(WIN_R // 2)], axis=1)
             for hh in range(NA_GROUP)], axis=0)

    def attend(r, src):
        s = src[...]
        pexp = jnp.exp2(s - jnp.max(s, axis=1, keepdims=True))
        l = jnp.sum(pexp, axis=1, keepdims=True)
        r0 = window_start(r)
        o = _dot(pexp.astype(BF16), v_ref[pl.ds(pl.multiple_of(r0 * w, w), nk), :]) / l
        out = jnp.where(head_masks[0], o[0:w], 0.0)
        for hh in range(1, NA_GROUP):
            out = out + jnp.where(head_masks[hh], o[hh * w:(hh + 1) * w], 0.0)
        o_ref[pl.ds(pl.multiple_of(r * w, w), w), :] = out.astype(o_ref.dtype)

    logits(0, s_even)

    def two_rows(k, carry):
        r = 2 * k
        logits(r + 1, s_odd)
        attend(r, s_even)
        logits(jnp.minimum(r + 2, n_rows - 1), s_even)
        attend(r + 1, s_odd)
        return carry

    lax.fori_loop(0, n_rows // 2, two_rows, 0, unroll=8)


def _neighbourhood_attention(qkv, bias, *, batch, seq):
    n_rows = seq // GRID_W
    assert n_rows % 2 == 0 and n_rows >= WIN_R
    d_model = qkv.shape[1] // 3
    n_groups = d_model // HEAD
    return pl.pallas_call(
        functools.partial(_na_kernel, n_rows=n_rows),
        grid=(batch, n_groups),
        in_specs=[pl.BlockSpec((seq, HEAD), lambda b, g: (b, g)),
                  pl.BlockSpec((seq, HEAD), lambda b, g: (b, n_groups + g)),
                  pl.BlockSpec((seq, HEAD), lambda b, g: (b, 2 * n_groups + g)),
                  pl.BlockSpec((NA_GROUP, 2 * WIN_R - 2, GRID_W, 2 * GRID_W), lambda b, g: (g, 0, 0, 0))],
        out_specs=pl.BlockSpec((seq, HEAD), lambda b, g: (b, g)),
        out_shape=jax.ShapeDtypeStruct((batch * seq, d_model), BF16),
        scratch_shapes=[pltpu.VMEM((NA_GROUP * GRID_W, WIN_R * GRID_W), F32)] * 2,
        compiler_params=_cparams(2),
    )(qkv, qkv, qkv, bias)


def _ffn_kernel(*refs, n_mix, alpha, n_parts):
    x_ref = refs[0]
    mix_refs = refs[1:1 + 2 * n_mix]
    lmg_ref, lmb_ref, lfg_ref, lfb_ref, wg_ref, wu_ref, wd_ref, o_ref = refs[1 + 2 * n_mix:]
    part = x_ref.shape[0] // n_parts
    rows = [slice(p * part, (p + 1) * part) for p in range(n_parts)]

    def pre(r):
        mix = _dot(mix_refs[0][r, :], mix_refs[1][...])
        for i in range(1, n_mix):
            mix = mix + _dot(mix_refs[2 * i][r, :], mix_refs[2 * i + 1][...])
        return _layer_norm(alpha * x_ref[r, :] + mix, lmg_ref[...], lmb_ref[...])

    def act(h):
        hb = h.astype(BF16)
        g = _dot(hb, wg_ref[...])
        u = _dot(hb, wu_ref[...])
        return (g * _sigmoid(g) * u).astype(BF16)

    h_prev = pre(rows[0])
    a_prev = act(h_prev)
    for p in range(1, n_parts):
        h = pre(rows[p])
        y_prev = _dot(a_prev, wd_ref[...])
        a = act(h)
        o_ref[rows[p - 1], :] = _layer_norm(alpha * h_prev + y_prev, lfg_ref[...], lfb_ref[...])
        h_prev, a_prev = h, a
    y_prev = _dot(a_prev, wd_ref[...])
    o_ref[rows[n_parts - 1], :] = _layer_norm(alpha * h_prev + y_prev, lfg_ref[...], lfb_ref[...])


def _mixer_out_ffn(x, mix_pairs, ln_mix_g, ln_mix_b, ln_ffn_g, ln_ffn_b, wg, wu, wd, *, alpha, tm):
    t, d = x.shape
    assert t % tm == 0 and tm % FFN_PART_ROWS == 0
    row = lambda i: (i, 0)
    in_specs = [pl.BlockSpec((tm, d), row)]
    args = [x]
    for o, w in mix_pairs:
        in_specs += [pl.BlockSpec((tm, o.shape[1]), row), _resident(w.shape)]
        args += [o, w]
    in_specs += [_resident((1, d))] * 4
    args += [ln_mix_g.reshape(1, d), ln_mix_b.reshape(1, d), ln_ffn_g.reshape(1, d), ln_ffn_b.reshape(1, d)]
    in_specs += [_resident(wg.shape), _resident(wu.shape), _resident(wd.shape)]
    args += [wg, wu, wd]
    return pl.pallas_call(
        functools.partial(_ffn_kernel, n_mix=len(mix_pairs), alpha=alpha, n_parts=tm // FFN_PART_ROWS),
        grid=(t // tm,),
        in_specs=in_specs,
        out_specs=pl.BlockSpec((tm, d), row),
        out_shape=jax.ShapeDtypeStruct((t, d), F32),
        compiler_params=_cparams(1),
    )(*args)


def _row_tile(t):
    for tm in (1024, 512, 256, 128):
        if t % tm == 0:
            return tm
    raise ValueError(f"token count {t} is not a multiple of 128")


def kernel(x, w_in_even, gate_bias_even, lb_raw, conv_qk, gn_hgrn, gn_mlstm, w_out_even, w_qkv_odd, rpb_odd,
           w_out_odd, ln_mix_g, ln_mix_b, ln_ffn_g, ln_ffn_b, w_ffn_gate, w_ffn_up, w_ffn_down):
    batch, seq, d_model = x.shape
    depth = ln_mix_g.shape[0]
    alpha = (2.0 * depth) ** 0.25
    t = batch * seq
    tm = min(_row_tile(t), 512)
    a_width = N_REC_HEADS * HEAD
    main_cols = 9 * a_width
    n_gate = 4 * N_REC_HEADS

    h = x.reshape(t, d_model)
    for layer in range(depth):
        j = layer // 2
        if layer % 2 == 0:
            w_in = w_in_even[j]
            w_main = w_in[:, :main_cols].astype(BF16)
            w_gate = jnp.pad(w_in[:, main_cols:], ((0, 0), (0, V7X_LANES - n_gate))).astype(BF16)
            proj, gates = _in_projection(h, w_main, w_gate, tm=tm)
            gates = gates[:, :n_gate]
            o_a = _hgrn(proj, lb_raw, gn_hgrn[j], batch=batch, seq=seq, layer_j=j)
            h_b = _mlstm(proj, gates, gate_bias_even[j], conv_qk[j], gn_mlstm[j], batch=batch, seq=seq)
            w_out = w_out_even[j].astype(BF16)
            mix_pairs = [(o_a, w_out[:a_width]), (h_b, w_out[a_width:])]
        else:
            qkv = _qkv_projection(h, w_qkv_odd[j].astype(BF16), q_scale=NA_DH ** -0.5 * LOG2_E, tm=tm)
            o = _neighbourhood_attention(qkv, _na_bias_tables(rpb_odd[j]), batch=batch, seq=seq)
            mix_pairs = [(o, w_out_odd[j].astype(BF16))]
        h = _mixer_out_ffn(h, mix_pairs, ln_mix_g[layer], ln_mix_b[layer], ln_ffn_g[layer], ln_ffn_b[layer],
                           w_ffn_gate[layer].astype(BF16), w_ffn_up[layer].astype(BF16),
                           w_ffn_down[layer].astype(BF16), alpha=alpha, tm=_row_tile(t))
    return h.reshape(batch, seq, d_model)
```

```python
import functools

import numpy as np
import jax
import jax.numpy as jnp
from jax import lax
from jax.experimental import pallas as pl
from jax.experimental.pallas import tpu as pltpu

F32 = jnp.float32
BF16 = jnp.bfloat16

GRID_W = 64
HEAD = 128
N_REC_HEADS = 4
CHUNK = 64
CONV_W = 5
HGRN_STEPS_PER_TRIP = 2
MLSTM_STEPS_PER_TRIP = 4
GATE_GROUP = 8
NA_DH = 32
NA_GROUP = HEAD // NA_DH
WIN_R = 8
WIN_C = 16
LN_EPS = 1e-5
GN_EPS = 1e-6
NEG_BIG = -1e30
LB_FLOOR = 1e-30
LOG2_E = 1.4426950408889634

V7X_LANES = 128
FFN_PART_ROWS = 256
V7X_VMEM_LIMIT_BYTES = 56 * 1024 * 1024

_LEVELS = (32, 16, 8, 4, 2, 1)


def _cparams(n_grid_axes):
    return pltpu.CompilerParams(
        dimension_semantics=("arbitrary",) * n_grid_axes,
        vmem_limit_bytes=V7X_VMEM_LIMIT_BYTES)


def _dot(a, b):
    return jnp.dot(a, b, preferred_element_type=F32)


def _dot_nt(a, b):
    return lax.dot_general(a, b, (((1,), (1,)), ((), ())), preferred_element_type=F32)


def _dot_tn(a, b):
    return lax.dot_general(a, b, (((0,), (0,)), ((), ())), preferred_element_type=F32)


def _split3(x):
    hi = x.astype(BF16)
    r1 = x - hi.astype(F32)
    mid = r1.astype(BF16)
    lo = (r1 - mid.astype(F32)).astype(BF16)
    return hi, mid, lo


def _log_sigmoid(z):
    return jnp.minimum(z, 0.0) - jnp.log1p(jnp.exp(-jnp.abs(z)))


def _sigmoid(z):
    return 1.0 / (1.0 + jnp.exp(-z))


def _layer_norm(t, g, b):
    mu = jnp.mean(t, axis=-1, keepdims=True)
    c = t - mu
    var = jnp.mean(c * c, axis=-1, keepdims=True)
    return c * lax.rsqrt(var + LN_EPS) * g + b


def _resident(shape):
    return pl.BlockSpec(shape, lambda i: (0,) * len(shape), pipeline_mode=pl.Buffered(1))


def _in_proj_kernel(x_ref, w_ref, wg_ref, o_ref, g_ref):
    xb = x_ref[...].astype(BF16)
    o_ref[...] = _dot(xb, w_ref[...]).astype(o_ref.dtype)
    g_ref[...] = _dot(xb, wg_ref[...])


def _in_projection(x, w_main, w_gate, *, tm):
    t, k = x.shape
    assert t % tm == 0
    return pl.pallas_call(
        _in_proj_kernel,
        grid=(t // tm,),
        in_specs=[pl.BlockSpec((tm, k), lambda i: (i, 0)), _resident(w_main.shape), _resident(w_gate.shape)],
        out_specs=[pl.BlockSpec((tm, w_main.shape[1]), lambda i: (i, 0)),
                   pl.BlockSpec((tm, w_gate.shape[1]), lambda i: (i, 0))],
        out_shape=[jax.ShapeDtypeStruct((t, w_main.shape[1]), BF16),
                   jax.ShapeDtypeStruct((t, w_gate.shape[1]), F32)],
        compiler_params=_cparams(1),
    )(x, w_main, w_gate)


def _qkv_kernel(x_ref, w_ref, o_ref, *, q_scale):
    acc = _dot(x_ref[...].astype(BF16), w_ref[...])
    d = acc.shape[1] // 3
    o_ref[:, :d] = (acc[:, :d] * q_scale).astype(o_ref.dtype)
    o_ref[:, d:] = acc[:, d:].astype(o_ref.dtype)


def _qkv_projection(x, w_qkv, *, q_scale, tm):
    t, k = x.shape
    assert t % tm == 0
    return pl.pallas_call(
        functools.partial(_qkv_kernel, q_scale=q_scale),
        grid=(t // tm,),
        in_specs=[pl.BlockSpec((tm, k), lambda i: (i, 0)), _resident(w_qkv.shape)],
        out_specs=pl.BlockSpec((tm, w_qkv.shape[1]), lambda i: (i, 0)),
        out_shape=jax.ShapeDtypeStruct((t, w_qkv.shape[1]), BF16),
        compiler_params=_cparams(1),
    )(x, w_qkv)


def _hgrn_constants():
    L = CHUNK
    t = np.arange(L)
    a_rows, rowsel, masks = [], [], []
    for c in _LEVELS:
        odd = (t // c) % 2 == 1
        rho = (t // (2 * c)) * 2 * c + c - 1
        u = t[None, :]
        a = np.where(odd[:, None], (u > rho[:, None]) & (u <= t[:, None]),
                     (u > t[:, None]) & (u <= rho[:, None]))
        a_rows.append(a.astype(np.float32))
        rowsel.append(np.broadcast_to(odd[:, None], (L, HEAD)).astype(np.float32))
        same = (t[:, None] // (2 * c)) == (t[None, :] // (2 * c))
        masks.append((odd[:, None] & ~odd[None, :] & same).astype(np.float32))
    masks.append(np.eye(L, dtype=np.float32))
    a_rows.append((t[None, :] <= t[:, None]).astype(np.float32))
    a_rows.append((t[None, :] > t[:, None]).astype(np.float32))
    a_f = np.stack(a_rows)
    rs_f = np.stack(rowsel)
    m_f = np.stack(masks)
    a = np.stack([a_f, a_f[:, ::-1, ::-1]]).reshape(2, 8 * L, L)
    a3 = np.concatenate([a, a, a, np.zeros_like(a)], axis=-1)
    rs = np.stack([rs_f, rs_f[:, ::-1]])
    m = np.stack([m_f, m_f[:, ::-1, ::-1]])
    return a3, rs, m


def _hgrn_kernel(q_ref, ff_ref, fb_ref, v_ref, g_ref, lb_ref, gn_ref, a3_ref, rs_ref, mk_ref,
                 o_ref, acc_s, st_s, y_s, sc_s, dec_s, fw_s, inc_s, dec3_s, *, layer_j, nchunk):
    L = CHUNK
    nlev = len(_LEVELS)
    f_refs = (ff_ref, fb_ref)
    Y_QPRE, Y_KSUF, Y_Q, Y_K = nlev, nlev + 1, nlev + 2, nlev + 3
    ns = HGRN_STEPS_PER_TRIP

    lbr = lb_ref[...]
    e = jnp.exp(lbr - jnp.max(lbr, axis=1, keepdims=True))
    soft = e / jnp.sum(e, axis=1, keepdims=True)
    cum = soft[:, 0:1, :]
    for i in range(1, layer_j + 1):
        cum = cum + soft[:, i:i + 1, :]
    lb = cum - soft[:, 0:1, :]
    lb_floor = jnp.maximum(lb, LB_FLOOR)
    one_m_lb = 1.0 - lb

    st_s[...] = jnp.zeros_like(st_s)

    def chunk_rows(d, step):
        step = jnp.minimum(step, nchunk - 1)
        c = step if d == 0 else nchunk - 1 - step
        return pl.ds(pl.multiple_of(c * L, L), L)

    def stage1(d, step0, slot0):
        qs, ks, parts = [], [], []
        for par in range(2):
            rows = chunk_rows(d, step0 + par)
            qs.append(q_ref[rows, :].astype(F32))
            f = lb_floor[d] + one_m_lb[d] * _sigmoid(f_refs[d][rows, :].astype(F32))
            ks.append(1.0 - f)
            hi, mid, lo = _split3(jnp.log(f) * LOG2_E)
            parts.append(jnp.concatenate([hi, mid, lo, jnp.zeros_like(hi)], axis=0))
        dall = _dot(a3_ref[d], jnp.concatenate(parts, axis=1))
        for half in range(2):
            slot = slot0 + half
            q, k = qs[half], ks[half]
            eall = jnp.exp2(dall[:, half * HEAD:(half + 1) * HEAD])
            for li, c in enumerate(_LEVELS):
                if c % 8 == 0:
                    first_q = 1 if d == 0 else 0
                    x = jnp.concatenate([(q if b % 2 == first_q else k)[b * c:(b + 1) * c] for b in range(L // c)],
                                        axis=0)
                else:
                    x = jnp.where(rs_ref[d, li] > 0.5, q, k)
                y_s[slot, d, li] = (x * eall[li * L:(li + 1) * L]).astype(BF16)
            e_pre = eall[nlev * L:(nlev + 1) * L]
            e_suf = eall[(nlev + 1) * L:(nlev + 2) * L]
            y_s[slot, d, Y_QPRE] = (q * e_pre).astype(BF16)
            y_s[slot, d, Y_KSUF] = (k * e_suf).astype(BF16)
            y_s[slot, d, Y_Q] = q.astype(BF16)
            y_s[slot, d, Y_K] = k.astype(BF16)
            last = L - 1 if d == 0 else 0
            dec_s[slot, d] = jnp.broadcast_to(e_pre[last:last + 1, :], (8, HEAD))

    def stage2(d, step, slot):
        scores = _dot_nt(y_s[slot, d, Y_Q], y_s[slot, d, Y_K]) * mk_ref[d, nlev]
        for li in range(nlev):
            y = y_s[slot, d, li]
            scores = scores + _dot_nt(y, y) * mk_ref[d, li]
        sc_s[slot, d] = scores.astype(BF16)
        fw_s[slot, d] = y_s[slot, d, Y_QPRE]
        inc_s[slot, d] = _dot_tn(v_ref[chunk_rows(d, step), :].astype(BF16), y_s[slot, d, Y_KSUF])
        dec3_s[slot, d] = dec_s[slot, d]

    def stage3(d, step, slot):
        rows = chunk_rows(d, step)
        st = st_s[d]
        o = _dot(sc_s[slot, d], v_ref[rows, :].astype(BF16)) + _dot_nt(fw_s[slot, d], st.astype(BF16))
        st_s[d] = st * dec3_s[slot, d][0:1, :] + inc_s[slot, d]
        acc_s[rows, :] = acc_s[rows, :] + o

    def body(it, carry):
        for par in range(ns):
            for d in range(2):
                stage3(d, ns * it + par, par)
        for par in range(ns):
            for d in range(2):
                stage2(d, ns * it + ns + par, par)
        for par in range(0, ns, 2):
            for d in range(2):
                stage1(d, ns * it + 2 * ns + par, par)
        return carry

    acc_s[...] = jnp.zeros_like(acc_s)
    for par in range(0, ns, 2):
        for d in range(2):
            stage1(d, par, par)
    for par in range(ns):
        for d in range(2):
            stage2(d, par, par)
    for par in range(0, ns, 2):
        for d in range(2):
            stage1(d, ns + par, par)
    lax.fori_loop(0, nchunk // ns, body, 0, unroll=8)

    gn = gn_ref[...]
    blk = 8 * L

    def fin(i, carry):
        rows = pl.ds(pl.multiple_of(i * blk, blk), blk)
        o = acc_s[rows, :]
        g = g_ref[rows, :].astype(F32)
        o = o * lax.rsqrt(jnp.mean(o * o, axis=-1, keepdims=True) + GN_EPS)
        o_ref[rows, :] = (o * gn * (g * _sigmoid(g))).astype(o_ref.dtype)
        return carry

    lax.fori_loop(0, (nchunk * L) // blk, fin, 0, unroll=4)


def _hgrn(proj, lb_raw, gn, *, batch, seq, layer_j):
    nchunk = seq // CHUNK
    ns = HGRN_STEPS_PER_TRIP
    assert nchunk % ns == 0 and seq % (8 * CHUNK) == 0
    a3, rs, mk = _hgrn_constants()
    n_even = lb_raw.shape[1]
    nh = N_REC_HEADS

    def col(kind):
        return pl.BlockSpec((seq, HEAD), lambda b, h: (b, kind * nh + h))

    const3 = lambda b, h: (0, 0, 0)
    const4 = lambda b, h: (0, 0, 0, 0)
    return pl.pallas_call(
        functools.partial(_hgrn_kernel, layer_j=layer_j, nchunk=nchunk),
        grid=(batch, nh),
        in_specs=[col(0), col(1), col(2), col(3), col(4),
                  pl.BlockSpec((2, n_even, HEAD), lambda b, h: (0, 0, h)),
                  pl.BlockSpec((1, HEAD), lambda b, h: (0, h)),
                  pl.BlockSpec(a3.shape, const3),
                  pl.BlockSpec(rs.shape, const4),
                  pl.BlockSpec(mk.shape, const4)],
        out_specs=pl.BlockSpec((seq, HEAD), lambda b, h: (b, h)),
        out_shape=jax.ShapeDtypeStruct((batch * seq, nh * HEAD), BF16),
        scratch_shapes=[pltpu.VMEM((seq, HEAD), F32), pltpu.VMEM((2, HEAD, HEAD), F32),
                        pltpu.VMEM((ns, 2, len(_LEVELS) + 4, CHUNK, HEAD), BF16),
                        pltpu.VMEM((ns, 2, CHUNK, CHUNK), BF16),
                        pltpu.VMEM((ns, 2, 8, HEAD), F32),
                        pltpu.VMEM((ns, 2, CHUNK, HEAD), BF16),
                        pltpu.VMEM((ns, 2, HEAD, HEAD), F32),
                        pltpu.VMEM((ns, 2, 8, HEAD), F32)],
        compiler_params=_cparams(2),
    )(proj, proj, proj, proj, proj, lb_raw, gn.reshape(1, -1),
      jnp.asarray(a3, BF16), jnp.asarray(rs), jnp.asarray(mk))


def _mlstm_constants():
    L = CHUNK
    t = np.arange(L)
    ut = (t[:, None] <= t[None, :]).astype(np.float32)
    cum = np.stack([ut, ut[::-1, ::-1]])
    tril = (t[None, :] <= t[:, None]).astype(np.float32)
    causal = np.stack([tril, tril[::-1, ::-1]])
    sel = np.zeros((HEAD, 4 * GATE_GROUP * HEAD), np.float32)
    for q in range(4):
        for p in range(3):
            for j in range(GATE_GROUP):
                sel[(q * 3 + p) * GATE_GROUP + j, (q * GATE_GROUP + j) * HEAD:(q * GATE_GROUP + j + 1) * HEAD] = 1.0
    return cum, causal, sel


def _mlstm_kernel(xq_ref, xk_ref, v_ref, og_ref, gates_ref, gb_ref, cwq_ref, cwk_ref, gn_ref,
                  cum_ref, cm_ref, sel_ref, o_ref, q_s, k_s, acc_s, b_s, c_s, cma_s,
                  bl_s, gm_s, mst_s, mnew_s, wold_s, qkw_s, inc_s, nd_s, wi_s, winter_s, wsc_s, floor_s,
                  *, nchunk):
    L = CHUNK
    pad = CONV_W // 2
    halo = 16

    def conv_chunk(c, carry):
        rows = pl.ds(pl.multiple_of(c * L, L), L)
        prev = pl.ds(pl.multiple_of(jnp.maximum(c * L - halo, 0), halo), halo)
        nxt = pl.ds(pl.multiple_of(jnp.minimum(c * L + L, nchunk * L - halo), halo), halo)
        has_prev = jnp.where(c > 0, 1.0, 0.0).astype(F32)
        has_next = jnp.where(c < nchunk - 1, 1.0, 0.0).astype(F32)
        for x_ref, w_ref, dst, scale in ((xq_ref, cwq_ref, q_s, 1.0), (xk_ref, cwk_ref, k_s, HEAD ** -0.5)):
            win = jnp.concatenate([x_ref[prev, :].astype(F32) * has_prev, x_ref[rows, :].astype(F32),
                                   x_ref[nxt, :].astype(F32) * has_next], axis=0)
            w = w_ref[...]
            acc = win[halo - pad:halo - pad + L] * w[0:1, :]
            for j in range(1, CONV_W):
                acc = acc + win[halo - pad + j:halo - pad + j + L] * w[j:j + 1, :]
            y = acc * _sigmoid(acc)
            dst[rows, :] = y * scale if scale != 1.0 else y
        return carry

    lax.fori_loop(0, nchunk, conv_chunk, 0, unroll=4)

    for d in range(2):
        lf2 = _log_sigmoid(gates_ref[2 + d] + gb_ref[2 + d])
        hi, mid, lo = _split3(lf2)
        cm = cum_ref[d]
        b2 = _dot(hi, cm) + _dot(mid, cm) + _dot(lo, cm)
        b_s[d] = b2
        last = L - 1 if d == 0 else 0
        bl_s[d] = b2[:, last:last + 1]
        li2 = gates_ref[d] + gb_ref[d]
        gm_s[d] = jnp.max(b2[:, last:last + 1] - b2 + li2, axis=1, keepdims=True)
        x = jnp.concatenate([li2 - b2, jnp.full((nchunk, HEAD - L), NEG_BIG, F32)], axis=1)
        lane_x = lax.broadcasted_iota(jnp.int32, x.shape, 1)
        sh = 1
        while sh < L:
            if d == 0:
                x = jnp.maximum(x, jnp.where(lane_x >= sh, pltpu.roll(x, sh, axis=1), NEG_BIG))
            else:
                x = jnp.maximum(x, pltpu.roll(x, HEAD - sh, axis=1))
            sh *= 2
        cma_s[d] = x[:, :L]

    c_s[...] = jnp.zeros_like(c_s)
    lane = lax.broadcasted_iota(jnp.int32, (L, HEAD), 1)
    ones_col = (lane == 0).astype(BF16)

    def stab_step(step, m):
        new = []
        for d in range(2):
            c = step if d == 0 else nchunk - 1 - step
            sl = pl.ds(c, 1)
            mst_s[d, sl, :] = m[d]
            m_new = jnp.maximum(bl_s[d, sl, :] + m[d], gm_s[d, sl, :])
            mnew_s[d, sl, :] = m_new
            new.append(m_new)
        return tuple(new)

    lax.fori_loop(0, nchunk, stab_step, (jnp.zeros((1, 1), F32), jnp.zeros((1, 1), F32)))

    G = GATE_GROUP
    causal = [cm_ref[d] > 0.5 for d in range(2)]

    def gate_weights(g, carry):
        sl = pl.ds(pl.multiple_of(g * G, G), G)
        for d in range(2):
            br = b_s[d, sl, :]
            li = gates_ref[d, sl, :] + gb_ref[d]
            a = li - br
            m_st = mst_s[d, sl, :]
            m_new = mnew_s[d, sl, :]
            last = L - 1 if d == 0 else 0
            b_last = br[:, last:last + 1]
            mx = jnp.maximum(m_st, cma_s[d, sl, :])
            quantities = (mx, jnp.exp(m_st - mx), jnp.exp(-(br + mx)), jnp.exp(b_last - br + li - m_new))
            parts = [p.astype(F32) for qty in quantities for p in _split3(qty)]
            parts.append(jnp.zeros((HEAD - len(parts) * G, L), F32))
            cols = _dot_tn(jnp.concatenate(parts, axis=0).astype(BF16), sel_ref[...])
            for j in range(G):
                rows = pl.ds(pl.multiple_of((g * G + j) * L, L), L)
                mx_col, winter, floor, wsc = (cols[:, (q * G + j) * HEAD:(q * G + j + 1) * HEAD] for q in range(4))
                am = jnp.where(causal[d], jnp.broadcast_to(a[j:j + 1], (L, L)), NEG_BIG)
                wi_s[d, rows, :] = jnp.exp(am - mx_col[:, :L])
                winter_s[d, rows, :] = winter
                floor_s[d, rows, :] = floor[:, 0:1]
                wsc_s[d, rows, :] = wsc
            wold_s[d, sl, :] = jnp.exp(b_last + m_st - m_new)
        return carry

    lax.fori_loop(0, nchunk // G, gate_weights, 0, unroll=8)

    def chunk_index(d, step):
        step = jnp.minimum(step, nchunk - 1)
        return step if d == 0 else nchunk - 1 - step

    def chunk_rows(d, step):
        return pl.ds(pl.multiple_of(chunk_index(d, step) * L, L), L)

    def value_aug(rows):
        return jnp.concatenate([v_ref[rows, :], ones_col], axis=1)

    def stage_a(d, step, slot):
        rows = chunk_rows(d, step)
        k = k_s[rows, :]
        qkw_s[slot, d] = (_dot_nt(q_s[rows, :].astype(BF16), k.astype(BF16)) * wi_s[d, rows, :]).astype(BF16)
        inc_s[slot, d] = _dot_tn((k * wsc_s[d, rows, :]).astype(BF16), value_aug(rows))

    def stage_b(d, step, slot):
        rows = chunk_rows(d, step)
        cst = c_s[d]
        w_inter = winter_s[d, rows, :]
        nd_s[slot, d] = (_dot(qkw_s[slot, d], value_aug(rows))
                         + jnp.concatenate([w_inter, w_inter], axis=1)
                         * _dot(q_s[rows, :].astype(BF16), cst.astype(BF16)))
        c_s[d] = wold_s[d, pl.ds(chunk_index(d, step), 1), :] * cst + inc_s[slot, d]

    def stage_c(d, step, slot):
        rows = chunk_rows(d, step)
        nd = nd_s[slot, d]
        h = nd[:, :HEAD] / jnp.maximum(jnp.abs(nd[:, HEAD:HEAD + 1]), floor_s[d, rows, :])
        acc_s[rows, :] = acc_s[rows, :] + h

    ns = MLSTM_STEPS_PER_TRIP

    def body(it, with_c):
        for par in range(ns):
            for d in range(2):
                if with_c:
                    stage_c(d, ns * it - ns + par, par)
        for par in range(ns):
            for d in range(2):
                stage_b(d, ns * it + par, par)
        for par in range(ns):
            for d in range(2):
                stage_a(d, ns * it + ns + par, par)

    def loop_body(it, carry):
        body(it, True)
        return carry

    acc_s[...] = jnp.zeros_like(acc_s)
    for d in range(2):
        for par in range(ns):
            stage_a(d, par, par)
    body(0, False)
    lax.fori_loop(1, nchunk // ns, loop_body, 0, unroll=15)
    for d in range(2):
        for par in range(ns):
            stage_c(d, nchunk - ns + par, par)

    gn = gn_ref[...]
    blk = 8 * L

    def fin(i, carry):
        rows = pl.ds(pl.multiple_of(i * blk, blk), blk)
        h = acc_s[rows, :]
        mu = jnp.mean(h, axis=-1, keepdims=True)
        cen = h - mu
        hn = cen * lax.rsqrt(jnp.mean(cen * cen, axis=-1, keepdims=True) + GN_EPS)
        o_ref[rows, :] = (hn * gn * _sigmoid(og_ref[rows, :].astype(F32))).astype(o_ref.dtype)
        return carry

    lax.fori_loop(0, (nchunk * L) // blk, fin, 0, unroll=4)


def _mlstm(proj, gates, gate_bias, conv_w, gn, *, batch, seq):
    nchunk = seq // CHUNK
    ns = MLSTM_STEPS_PER_TRIP
    assert nchunk % GATE_GROUP == 0 and nchunk % ns == 0 and seq % (8 * CHUNK) == 0
    nh = N_REC_HEADS
    width = nh * HEAD
    cum, causal, sel = _mlstm_constants()
    g5 = gates.reshape(batch, nchunk, CHUNK, 4, nh).transpose(0, 4, 3, 1, 2)
    gb = jnp.broadcast_to(gate_bias.reshape(4, nh).T[:, :, None, None], (nh, 4, 1, CHUNK))
    cw = jnp.pad(conv_w, ((0, 8 - CONV_W), (0, 0)))
    first_col = 5 * nh

    def col(kind):
        return pl.BlockSpec((seq, HEAD), lambda b, h: (b, first_col + kind * nh + h))

    const2 = lambda b, h: (0, 0)
    const3 = lambda b, h: (0, 0, 0)
    return pl.pallas_call(
        functools.partial(_mlstm_kernel, nchunk=nchunk),
        grid=(batch, nh),
        in_specs=[col(0), col(1), col(2), col(3),
                  pl.BlockSpec((None, None, 4, nchunk, CHUNK), lambda b, h: (b, h, 0, 0, 0)),
                  pl.BlockSpec((None, 4, 1, CHUNK), lambda b, h: (h, 0, 0, 0)),
                  pl.BlockSpec((8, HEAD), lambda b, h: (0, h)),
                  pl.BlockSpec((8, HEAD), lambda b, h: (0, nh + h)),
                  pl.BlockSpec((1, HEAD), lambda b, h: (0, h)),
                  pl.BlockSpec(cum.shape, const3),
                  pl.BlockSpec(causal.shape, const3),
                  pl.BlockSpec(sel.shape, const2)],
        out_specs=pl.BlockSpec((seq, HEAD), lambda b, h: (b, h)),
        out_shape=jax.ShapeDtypeStruct((batch * seq, width), BF16),
        scratch_shapes=[pltpu.VMEM((seq, HEAD), F32), pltpu.VMEM((seq, HEAD), F32),
                        pltpu.VMEM((seq, HEAD), F32), pltpu.VMEM((2, nchunk, CHUNK), F32),
                        pltpu.VMEM((2, HEAD, 2 * HEAD), F32), pltpu.VMEM((2, nchunk, CHUNK), F32)]
                       + [pltpu.VMEM((2, nchunk, 1), F32)] * 5
                       + [pltpu.VMEM((ns, 2, CHUNK, CHUNK), BF16), pltpu.VMEM((ns, 2, HEAD, 2 * HEAD), F32),
                          pltpu.VMEM((ns, 2, CHUNK, 2 * HEAD), F32),
                          pltpu.VMEM((2, seq, CHUNK), F32), pltpu.VMEM((2, seq, HEAD), F32),
                          pltpu.VMEM((2, seq, HEAD), F32), pltpu.VMEM((2, seq, 1), F32)],
        compiler_params=_cparams(2),
    )(proj, proj, proj, proj, g5, gb, cw, cw, gn.reshape(1, -1),
      jnp.asarray(cum, BF16), jnp.asarray(causal), jnp.asarray(sel, BF16))


def _na_bias_tables(rpb):
    w = GRID_W
    qc = np.arange(w)[:, None]
    kc = np.arange(w)[None, :]
    c0 = np.clip(qc - WIN_C // 2, 0, w - WIN_C)
    valid = (kc >= c0) & (kc < c0 + WIN_C)
    cidx = np.clip(kc - qc + WIN_C - 1, 0, 2 * WIN_C - 2)
    tiles = jnp.where(jnp.asarray(valid)[None, None], rpb.astype(F32)[:, :, cidx] * LOG2_E, NEG_BIG)
    return jnp.concatenate([tiles[:, :-1], tiles[:, 1:]], axis=-1)


def _na_kernel(q_ref, k_ref, v_ref, bias_ref, o_ref, s_even, s_odd, *, n_rows):
    w = GRID_W
    nk = WIN_R * w
    lane_head = lax.broadcasted_iota(jnp.int32, (w, HEAD), 1) // NA_DH
    head_masks = [lane_head == hh for hh in range(NA_GROUP)]

    def window_start(r):
        return jnp.clip(r - WIN_R // 2, 0, n_rows - WIN_R)

    def logits(r, dst):
        q = q_ref[pl.ds(pl.multiple_of(r * w, w), w), :]
        zero = jnp.zeros_like(q)
        qm = jnp.concatenate([jnp.where(head_masks[hh], q, zero) for hh in range(NA_GROUP)], axis=0)
        r0 = window_start(r)
        s = _dot_nt(qm, k_ref[pl.ds(pl.multiple_of(r0 * w, w), nk), :])
        first = WIN_R - 1 - (r - r0)
        dst[...] = s + jnp.concatenate(
            [jnp.concatenate([bias_ref[hh, first + 2 * m] for m in range(WIN_R // 2)], axis=1)
             for hh in range(NA_GROUP)], axis=0)

    def attend(r, src):
        s = src[...]
        pexp = jnp.exp2(s - jnp.max(s, axis=1, keepdims=True))
        l = jnp.sum(pexp, axis=1, keepdims=True)
        r0 = window_start(r)
        o = _dot(pexp.astype(BF16), v_ref[pl.ds(pl.multiple_of(r0 * w, w), nk), :]) / l
        out = jnp.where(head_masks[0], o[0:w], 0.0)
        for hh in range(1, NA_GROUP):
            out = out + jnp.where(head_masks[hh], o[hh * w:(hh + 1) * w], 0.0)
        o_ref[pl.ds(pl.multiple_of(r * w, w), w), :] = out.astype(o_ref.dtype)

    logits(0, s_even)

    def two_rows(k, carry):
        r = 2 * k
        logits(r + 1, s_odd)
        attend(r, s_even)
        logits(jnp.minimum(r + 2, n_rows - 1), s_even)
        attend(r + 1, s_odd)
        return carry

    lax.fori_loop(0, n_rows // 2, two_rows, 0, unroll=16)


def _neighbourhood_attention(qkv, bias, *, batch, seq):
    n_rows = seq // GRID_W
    assert n_rows % 2 == 0 and n_rows >= WIN_R
    d_model = qkv.shape[1] // 3
    n_groups = d_model // HEAD
    return pl.pallas_call(
        functools.partial(_na_kernel, n_rows=n_rows),
        grid=(batch, n_groups),
        in_specs=[pl.BlockSpec((seq, HEAD), lambda b, g: (b, g)),
                  pl.BlockSpec((seq, HEAD), lambda b, g: (b, n_groups + g)),
                  pl.BlockSpec((seq, HEAD), lambda b, g: (b, 2 * n_groups + g)),
                  pl.BlockSpec((NA_GROUP, 2 * WIN_R - 2, GRID_W, 2 * GRID_W), lambda b, g: (g, 0, 0, 0))],
        out_specs=pl.BlockSpec((seq, HEAD), lambda b, g: (b, g)),
        out_shape=jax.ShapeDtypeStruct((batch * seq, d_model), BF16),
        scratch_shapes=[pltpu.VMEM((NA_GROUP * GRID_W, WIN_R * GRID_W), F32)] * 2,
        compiler_params=_cparams(2),
    )(qkv, qkv, qkv, bias)


def _ffn_kernel(*refs, n_mix, alpha, n_parts):
    x_ref = refs[0]
    mix_refs = refs[1:1 + 2 * n_mix]
    lmg_ref, lmb_ref, lfg_ref, lfb_ref, wg_ref, wu_ref, wd_ref, o_ref = refs[1 + 2 * n_mix:]
    part = x_ref.shape[0] // n_parts
    rows = [slice(p * part, (p + 1) * part) for p in range(n_parts)]

    def pre(r):
        mix = _dot(mix_refs[0][r, :], mix_refs[1][...])
        for i in range(1, n_mix):
            mix = mix + _dot(mix_refs[2 * i][r, :], mix_refs[2 * i + 1][...])
        return _layer_norm(alpha * x_ref[r, :] + mix, lmg_ref[...], lmb_ref[...])

    def act(h):
        hb = h.astype(BF16)
        g = _dot(hb, wg_ref[...])
        u = _dot(hb, wu_ref[...])
        return (g * _sigmoid(g) * u).astype(BF16)

    h_prev = pre(rows[0])
    a_prev = act(h_prev)
    for p in range(1, n_parts):
        h = pre(rows[p])
        y_prev = _dot(a_prev, wd_ref[...])
        a = act(h)
        o_ref[rows[p - 1], :] = _layer_norm(alpha * h_prev + y_prev, lfg_ref[...], lfb_ref[...])
        h_prev, a_prev = h, a
    y_prev = _dot(a_prev, wd_ref[...])
    o_ref[rows[n_parts - 1], :] = _layer_norm(alpha * h_prev + y_prev, lfg_ref[...], lfb_ref[...])


def _mixer_out_ffn(x, mix_pairs, ln_mix_g, ln_mix_b, ln_ffn_g, ln_ffn_b, wg, wu, wd, *, alpha, tm):
    t, d = x.shape
    assert t % tm == 0 and tm % FFN_PART_ROWS == 0
    row = lambda i: (i, 0)
    in_specs = [pl.BlockSpec((tm, d), row)]
    args = [x]
    for o, w in mix_pairs:
        in_specs += [pl.BlockSpec((tm, o.shape[1]), row), _resident(w.shape)]
        args += [o, w]
    in_specs += [_resident((1, d))] * 4
    args += [ln_mix_g.reshape(1, d), ln_mix_b.reshape(1, d), ln_ffn_g.reshape(1, d), ln_ffn_b.reshape(1, d)]
    in_specs += [_resident(wg.shape), _resident(wu.shape), _resident(wd.shape)]
    args += [wg, wu, wd]
    return pl.pallas_call(
        functools.partial(_ffn_kernel, n_mix=len(mix_pairs), alpha=alpha, n_parts=tm // FFN_PART_ROWS),
        grid=(t // tm,),
        in_specs=in_specs,
        out_specs=pl.BlockSpec((tm, d), row),
        out_shape=jax.ShapeDtypeStruct((t, d), F32),
        compiler_params=_cparams(1),
    )(*args)


def _row_tile(t):
    for tm in (1024, 512, 256, 128):
        if t % tm == 0:
            return tm
    raise ValueError(f"token count {t} is not a multiple of 128")


def kernel(x, w_in_even, gate_bias_even, lb_raw, conv_qk, gn_hgrn, gn_mlstm, w_out_even, w_qkv_odd, rpb_odd,
           w_out_odd, ln_mix_g, ln_mix_b, ln_ffn_g, ln_ffn_b, w_ffn_gate, w_ffn_up, w_ffn_down):
    batch, seq, d_model = x.shape
    depth = ln_mix_g.shape[0]
    alpha = (2.0 * depth) ** 0.25
    t = batch * seq
    tm = min(_row_tile(t), 512)
    a_width = N_REC_HEADS * HEAD
    main_cols = 9 * a_width
    n_gate = 4 * N_REC_HEADS

    h = x.reshape(t, d_model)
    for layer in range(depth):
        j = layer // 2
        if layer % 2 == 0:
            w_in = w_in_even[j]
            w_main = w_in[:, :main_cols].astype(BF16)
            w_gate = jnp.pad(w_in[:, main_cols:], ((0, 0), (0, V7X_LANES - n_gate))).astype(BF16)
            proj, gates = _in_projection(h, w_main, w_gate, tm=tm)
            gates = gates[:, :n_gate]
            o_a = _hgrn(proj, lb_raw, gn_hgrn[j], batch=batch, seq=seq, layer_j=j)
            h_b = _mlstm(proj, gates, gate_bias_even[j], conv_qk[j], gn_mlstm[j], batch=batch, seq=seq)
            w_out = w_out_even[j].astype(BF16)
            mix_pairs = [(o_a, w_out[:a_width]), (h_b, w_out[a_width:])]
        else:
            qkv = _qkv_projection(h, w_qkv_odd[j].astype(BF16), q_scale=NA_DH ** -0.5 * LOG2_E, tm=tm)
            o = _neighbourhood_attention(qkv, _na_bias_tables(rpb_odd[j]), batch=batch, seq=seq)
            mix_pairs = [(o, w_out_odd[j].astype(BF16))]
        h = _mixer_out_ffn(h, mix_pairs, ln_mix_g[layer], ln_mix_b[layer], ln_ffn_g[layer], ln_ffn_b[layer],
                           w_ffn_gate[layer].astype(BF16), w_ffn_up[layer].astype(BF16),
                           w_ffn_down[layer].astype(BF16), alpha=alpha, tm=_row_tile(t))
    return h.reshape(batch, seq, d_model)
```
